```python
import jax
import jax.numpy as jnp
from jax import lax
import numpy as np

D_MODEL = 2048
BATCH = 1
SEQ = 8192
DEPTH = 4
DEC_BATCH = 16
DEC_SEQ = 64
PAST_LEN = 2048

CHUNK = 64
A_WIDTH = D_MODEL // 2
A_HEADS = 8
A_DIM = A_WIDTH // A_HEADS
A_BLOCK = 128
B_HEAD_DIM = 64
B_WIDTH = D_MODEL // 2
B_HEADS = B_WIDTH // B_HEAD_DIM
SB_QBLOCK = 128
EVEN_IN = 2 * A_WIDTH + 3 * B_WIDTH
EVEN_MIX = A_WIDTH + B_WIDTH
C_HEAD_DIM = 64
C_HEADS = D_MODEL // C_HEAD_DIM
C_DECAY_LORA = 96
C_ICLR_LORA = 96
C_VRES_LORA = 64
C_GATE_LORA = 256
C_GN_EPS = 64e-5
D_FF = ((8 * D_MODEL // 3 + 255) // 256) * 256
CONV_W = 3
PLE_DIM = 256
LN_EPS = 1e-5
DN_ALPHA = (2 * DEPTH) ** 0.25
DN_BETA = (8 * DEPTH) ** -0.25
N_EVEN = (DEPTH + 1) // 2
N_ODD = DEPTH // 2

kernel_name = 'streaming_gmlp_stickbreak_rwkv7_convffn'


def _layernorm(x, g, b, eps=LN_EPS):
    xf = x.astype(jnp.float32)
    mu = xf.mean(-1, keepdims=True)
    var = jnp.square(xf - mu).mean(-1, keepdims=True)
    return ((xf - mu) * lax.rsqrt(var + eps) * g + b).astype(x.dtype)


def _chunk_mask(n):
    i = jnp.arange(n)
    return (i[:, None] // CHUNK) >= (i[None, :] // CHUNK)


def _spatial_gate(u, vn, w_s, b_s):
    n = u.shape[2]
    w = jnp.where(_chunk_mask(n)[None], w_s[:, :n, :n], 0.0)
    mixed = jnp.einsum('hts,bnshc->bnthc', w, vn) + b_s[:, :n].T[None, None, :, :, None]
    return u * mixed


def _even_project(x, w_in, lnv_g, lnv_b):
    bsz, t, _ = x.shape
    h = x @ w_in
    z_a = jax.nn.gelu(h[..., :2 * A_WIDTH])
    u = z_a[..., :A_WIDTH]
    va = _layernorm(z_a[..., A_WIDTH:], lnv_g, lnv_b)
    o = 2 * A_WIDTH
    q = h[..., o:o + B_WIDTH].reshape(bsz, t, B_HEADS, B_HEAD_DIM)
    k = h[..., o + B_WIDTH:o + 2 * B_WIDTH].reshape(bsz, t, B_HEADS, B_HEAD_DIM)
    v = h[..., o + 2 * B_WIDTH:].reshape(bsz, t, B_HEADS, B_HEAD_DIM)
    return u, va, q, k, v


def _stick_breaking(q, k, v, q_pos):
    f32 = jnp.float32
    z = jnp.einsum('bqhd,bkhd->bhqk', q.astype(f32), k.astype(f32)) * (B_HEAD_DIM ** -0.5)
    k_pos = jnp.arange(k.shape[1])
    mask = k_pos[None, :] < q_pos[:, None]
    log_fail = jnp.where(mask, jax.nn.log_sigmoid(-z), 0.0)
    incl = lax.cumsum(log_fail, axis=3, reverse=True)
    after = jnp.concatenate([incl[..., 1:], jnp.zeros_like(incl[..., :1])], axis=-1)
    wts = jnp.where(mask, jnp.exp(jax.nn.log_sigmoid(z) + after), 0.0)
    return jnp.einsum('bhqk,bkhd->bqhd', wts, v.astype(f32)).astype(v.dtype)


def _stick_breaking_prompt(q, k, v):
    bsz, t, h, d = q.shape
    nb = t // SB_QBLOCK
    qb = q.reshape(bsz, nb, SB_QBLOCK, h, d).transpose(1, 0, 2, 3, 4)
    starts = jnp.arange(nb) * SB_QBLOCK

    def one_block(args):
        qi, s = args
        return _stick_breaking(qi, k, v, s + jnp.arange(SB_QBLOCK))

    out = lax.map(one_block, (qb, starts))
    return out.transpose(1, 0, 2, 3, 4).reshape(bsz, t, h, d)


def _even_mixer_prompt(x, w_in, lnv_g, lnv_b, w_s, b_s, w_o):
    bsz, t, _ = x.shape
    u, va, q, k, v = _even_project(x, w_in, lnv_g, lnv_b)
    nb = t // A_BLOCK
    blk = lambda a: a.reshape(bsz, nb, A_BLOCK, A_HEADS, A_DIM)
    a_out = _spatial_gate(blk(u), blk(va), w_s, b_s).reshape(bsz, t, A_WIDTH)
    b_out = _stick_breaking_prompt(q, k, v).reshape(bsz, t, B_WIDTH)
    out = jnp.concatenate([a_out, b_out], axis=-1) @ w_o
    return out, k, v


def _even_mixer_sample(x, k_cache, v_cache, w_in, lnv_g, lnv_b, w_s, b_s, w_o):
    bsz, t, _ = x.shape
    u, va, q, k, v = _even_project(x, w_in, lnv_g, lnv_b)
    blk = lambda a: a.reshape(bsz, 1, t, A_HEADS, A_DIM)
    a_out = _spatial_gate(blk(u), blk(va), w_s, b_s).reshape(bsz, t, A_WIDTH)
    past = k_cache.shape[1]
    k_all = jnp.concatenate([k_cache.astype(k.dtype), k], axis=1)
    v_all = jnp.concatenate([v_cache.astype(v.dtype), v], axis=1)
    b_out = _stick_breaking(q, k_all, v_all, past + jnp.arange(t)).reshape(bsz, t, B_WIDTH)
    out = jnp.concatenate([a_out, b_out], axis=-1) @ w_o
    return out, k, v, va.reshape(bsz, t, A_HEADS, A_DIM)


def _wkv7_scan(r, w, k, v, a, b, s0):
    f32 = jnp.float32

    def step(s, inp):
        r_t, w_t, k_t, v_t, a_t, b_t = inp
        sa = jnp.einsum('bhij,bhj->bhi', s, a_t)
        s = s * w_t[:, :, None, :] + sa[..., None] * b_t[:, :, None, :] + v_t[..., None] * k_t[:, :, None, :]
        return s, jnp.einsum('bhij,bhj->bhi', s, r_t)

    tm = lambda a_: jnp.moveaxis(a_.astype(f32), 1, 0)
    s_fin, y = lax.scan(step, s0.astype(f32), (tm(r), tm(w), tm(k), tm(v), tm(a), tm(b)))
    return jnp.moveaxis(y, 0, 1), s_fin


def _rwkv7(x, x_last, s0, v_first, vres, mu, w_r, w_k, w_v, w_o, w0, w1, w2,
           a0, a1, a2, g1, g2, k_k, k_a, r_k, gn_g, gn_b):
    bsz, t, d = x.shape
    f32 = jnp.float32
    x_prev = jnp.concatenate([x_last[:, None, :].astype(x.dtype), x[:, :-1]], axis=1)
    xx = x_prev - x
    xr, xw, xk, xv, xa, xg = (x + xx * mu[j] for j in range(6))
    r = xr @ w_r
    k = xk @ w_k
    v = xv @ w_v
    log_w = -jax.nn.softplus(-(w0 + jnp.tanh(xw @ w1) @ w2)) - 0.5
    if vres is None:
        v_first = v
    else:
        v0, v1, v2 = vres
        v = v + (v_first - v) * jax.nn.sigmoid(v0 + (xv @ v1) @ v2)
    a = jax.nn.sigmoid(a0 + (xa @ a1) @ a2)
    g = jax.nn.sigmoid(xg @ g1) @ g2
    hv = lambda z: z.astype(f32).reshape(bsz, t, C_HEADS, C_HEAD_DIM)
    kk = hv(k * k_k)
    kk = kk / jnp.maximum(jnp.sqrt(jnp.sum(kk * kk, axis=-1, keepdims=True)), 1e-12)
    k = k * (1 + (a - 1) * k_a)
    rh, kh, vh, ah = hv(r), hv(k), hv(v), hv(a)
    decay = jnp.exp(-jnp.exp(hv(log_w)))
    y, s_fin = _wkv7_scan(rh, decay, kh, vh, -kk, kk * ah, s0)
    mu_y = y.mean(-1, keepdims=True)
    var_y = jnp.square(y - mu_y).mean(-1, keepdims=True)
    y = ((y - mu_y) * lax.rsqrt(var_y + C_GN_EPS)).reshape(bsz, t, d) * gn_g + gn_b
    bonus = jnp.sum(rh * kh * r_k, axis=-1, keepdims=True) * vh
    y = y + bonus.reshape(bsz, t, d)
    out = (y * g).astype(x.dtype) @ w_o
    return out, s_fin, x[:, -1], v_first


def _conv_ffn(x, conv_prev, w_in, conv_w, conv_b, w_out):
    t = x.shape[1]
    h = x @ w_in
    hg, hu = h[..., :D_FF], h[..., D_FF:]
    hp = jnp.concatenate([conv_prev.astype(hg.dtype), hg], axis=1)
    hc = conv_b
    for j in range(CONV_W):
        hc = hc + hp[:, j:j + t] * conv_w[j]
    out = (jax.nn.gelu(hc) * hu) @ w_out
    return out, hp[:, t:]


def _trunk(x, p, W, sb_k_cache, sb_v_cache, wkv0, shift0, conv0):
    is_prompt = sb_k_cache is None
    new_k, new_v, new_va, new_wkv, new_shift, new_conv = [], [], [], [], [], []
    v_first = None
    for i in range(DEPTH):
        j = i // 2
        if i % 2 == 0:
            args = (W['even_w_in'][j], W['even_lnv_g'][j], W['even_lnv_b'][j],
                    W['even_w_s'][j], W['even_b_s'][j], W['even_w_o'][j])
            if is_prompt:
                mix, k, v = _even_mixer_prompt(x, *args)
            else:
                mix, k, v, va = _even_mixer_sample(x, sb_k_cache[j], sb_v_cache[j], *args)
                new_va.append(va)
            new_k.append(k)
            new_v.append(v)
        else:
            vres = None if j == 0 else (W['c_v0'][j - 1], W['c_v1'][j - 1], W['c_v2'][j - 1])
            mix, s_fin, last, v_first = _rwkv7(
                x, shift0[j], wkv0[j], v_first, vres, W['c_mu'][j], W['c_w_r'][j], W['c_w_k'][j],
                W['c_w_v'][j], W['c_w_o'][j], W['c_w0'][j], W['c_w1'][j], W['c_w2'][j],
                W['c_a0'][j], W['c_a1'][j], W['c_a2'][j], W['c_g1'][j], W['c_g2'][j],
                W['c_k_k'][j], W['c_k_a'][j], W['c_r_k'][j], W['c_gn_g'][j], W['c_gn_b'][j])
            new_wkv.append(s_fin)
            new_shift.append(last)
        x = _layernorm(DN_ALPHA * x + mix, W['ln1_g'][i], W['ln1_b'][i])
        f, conv_rows = _conv_ffn(x, conv0[i], W['ffn_w_in'][i], W['ffn_conv_w'][i],
                                 W['ffn_conv_b'][i], W['ffn_w_out'][i])
        new_conv.append(conv_rows)
        x = _layernorm(DN_ALPHA * x + f, W['ln2_g'][i], W['ln2_b'][i])
        x = x + jax.nn.sigmoid(x @ W['ple_gate'][i]) * (p[i] @ W['ple_proj'][i])
    st = lambda l: jnp.stack(l) if l else None
    return x, st(new_k), st(new_v), st(new_va), st(new_wkv), st(new_shift), st(new_conv)


def setup_inputs(seed: int = 0) -> dict:
    key = jax.random.key(seed)
    keys = jax.random.split(key, 64)
    ctr = [0]

    def nk():
        kk = keys[ctr[0]]
        ctr[0] += 1
        return kk

    def nrm(shape, scale=1.0):
        return jax.random.normal(nk(), shape, jnp.float32) * scale

    def gain(shape):
        return 1.0 + nrm(shape, 0.02)

    f = D_MODEL ** -0.5
    in_col_scale = jnp.concatenate([jnp.ones((2 * A_WIDTH + 2 * B_WIDTH,), jnp.float32),
                                    jnp.full((B_WIDTH,), DN_BETA, jnp.float32)])
    return {
        'x_prompt': nrm((BATCH, SEQ, D_MODEL)),
        'x_sample': nrm((DEC_BATCH, DEC_SEQ, D_MODEL)),
        'cache_sb_k': nrm((N_EVEN, DEC_BATCH, PAST_LEN, B_HEADS, B_HEAD_DIM)),
        'cache_sb_v': nrm((N_EVEN, DEC_BATCH, PAST_LEN, B_HEADS, B_HEAD_DIM), DN_BETA),
        'state_rwkv_wkv': nrm((N_ODD, DEC_BATCH, C_HEADS, C_HEAD_DIM, C_HEAD_DIM), 0.3),
        'state_rwkv_shift': nrm((N_ODD, DEC_BATCH, D_MODEL)),
        'state_ffn_conv': nrm((DEPTH, DEC_BATCH, CONV_W - 1, D_FF), DN_BETA),
        'p_prompt': nrm((DEPTH, BATCH, SEQ, PLE_DIM)),
        'p_sample': nrm((DEPTH, DEC_BATCH, DEC_SEQ, PLE_DIM)),
        'even_w_in': nrm((N_EVEN, D_MODEL, EVEN_IN), f) * in_col_scale,
        'even_lnv_g': gain((N_EVEN, A_WIDTH)),
        'even_lnv_b': nrm((N_EVEN, A_WIDTH), 0.02),
        'even_w_s': nrm((N_EVEN, A_HEADS, A_BLOCK, A_BLOCK), 0.5 * A_BLOCK ** -0.5),
        'even_b_s': gain((N_EVEN, A_HEADS, A_BLOCK)),
        'even_w_o': nrm((N_EVEN, EVEN_MIX, D_MODEL), DN_BETA * EVEN_MIX ** -0.5),
        'c_mu': jax.random.uniform(nk(), (N_ODD, 6, D_MODEL), jnp.float32),
        'c_w_r': nrm((N_ODD, D_MODEL, D_MODEL), f),
        'c_w_k': nrm((N_ODD, D_MODEL, D_MODEL), f),
        'c_w_v': nrm((N_ODD, D_MODEL, D_MODEL), f * DN_BETA),
        'c_w_o': nrm((N_ODD, D_MODEL, D_MODEL), f * DN_BETA),
        'c_w0': nrm((N_ODD, D_MODEL), 1.0),
        'c_w1': nrm((N_ODD, D_MODEL, C_DECAY_LORA), f),
        'c_w2': nrm((N_ODD, C_DECAY_LORA, D_MODEL), 0.5 * C_DECAY_LORA ** -0.5),
        'c_a0': nrm((N_ODD, D_MODEL), 0.5),
        'c_a1': nrm((N_ODD, D_MODEL, C_ICLR_LORA), f),
        'c_a2': nrm((N_ODD, C_ICLR_LORA, D_MODEL), 0.5 * C_ICLR_LORA ** -0.5),
        'c_v0': nrm((N_ODD - 1, D_MODEL), 0.5),
        'c_v1': nrm((N_ODD - 1, D_MODEL, C_VRES_LORA), f),
        'c_v2': nrm((N_ODD - 1, C_VRES_LORA, D_MODEL), 0.5 * C_VRES_LORA ** -0.5),
        'c_g1': nrm((N_ODD, D_MODEL, C_GATE_LORA), f),
        'c_g2': nrm((N_ODD, C_GATE_LORA, D_MODEL), C_GATE_LORA ** -0.5),
        'c_k_k': 0.85 + nrm((N_ODD, D_MODEL), 0.02),
        'c_k_a': gain((N_ODD, D_MODEL)),
        'c_r_k': nrm((N_ODD, C_HEADS, C_HEAD_DIM), 0.1),
        'c_gn_g': gain((N_ODD, D_MODEL)),
        'c_gn_b': nrm((N_ODD, D_MODEL), 0.02),
        'ffn_w_in': nrm((DEPTH, D_MODEL, 2 * D_FF), f * DN_BETA),
        'ffn_conv_w': nrm((DEPTH, CONV_W, D_FF), CONV_W ** -0.5),
        'ffn_conv_b': nrm((DEPTH, D_FF), 0.02),
        'ffn_w_out': nrm((DEPTH, D_FF, D_MODEL), DN_BETA * D_FF ** -0.5),
        'ln1_g': gain((DEPTH, D_MODEL)),
        'ln1_b': nrm((DEPTH, D_MODEL), 0.02),
        'ln2_g': gain((DEPTH, D_MODEL)),
        'ln2_b': nrm((DEPTH, D_MODEL), 0.02),
        'ple_proj': nrm((DEPTH, PLE_DIM, D_MODEL), PLE_DIM ** -0.5),
        'ple_gate': nrm((DEPTH, D_MODEL, D_MODEL), f),
    }


def reference(x_prompt, x_sample, cache_sb_k, cache_sb_v, state_rwkv_wkv, state_rwkv_shift,
              state_ffn_conv, p_prompt, p_sample, even_w_in, even_lnv_g, even_lnv_b, even_w_s,
              even_b_s, even_w_o, c_mu, c_w_r, c_w_k, c_w_v, c_w_o, c_w0, c_w1, c_w2, c_a0, c_a1,
              c_a2, c_v0, c_v1, c_v2, c_g1, c_g2, c_k_k, c_k_a, c_r_k, c_gn_g, c_gn_b, ffn_w_in,
              ffn_conv_w, ffn_conv_b, ffn_w_out, ln1_g, ln1_b, ln2_g, ln2_b, ple_proj, ple_gate):
    W = dict(even_w_in=even_w_in, even_lnv_g=even_lnv_g, even_lnv_b=even_lnv_b, even_w_s=even_w_s,
             even_b_s=even_b_s, even_w_o=even_w_o, c_mu=c_mu, c_w_r=c_w_r, c_w_k=c_w_k,
             c_w_v=c_w_v, c_w_o=c_w_o, c_w0=c_w0, c_w1=c_w1, c_w2=c_w2, c_a0=c_a0, c_a1=c_a1,
             c_a2=c_a2, c_v0=c_v0, c_v1=c_v1, c_v2=c_v2, c_g1=c_g1, c_g2=c_g2, c_k_k=c_k_k,
             c_k_a=c_k_a, c_r_k=c_r_k, c_gn_g=c_gn_g, c_gn_b=c_gn_b, ffn_w_in=ffn_w_in,
             ffn_conv_w=ffn_conv_w, ffn_conv_b=ffn_conv_b, ffn_w_out=ffn_w_out, ln1_g=ln1_g,
             ln1_b=ln1_b, ln2_g=ln2_g, ln2_b=ln2_b, ple_proj=ple_proj, ple_gate=ple_gate)
    bp = x_prompt.shape[0]
    wkv_zero = jnp.zeros((N_ODD, bp, C_HEADS, C_HEAD_DIM, C_HEAD_DIM), jnp.float32)
    shift_zero = jnp.zeros((N_ODD, bp, D_MODEL), x_prompt.dtype)
    conv_zero = jnp.zeros((DEPTH, bp, CONV_W - 1, D_FF), x_prompt.dtype)
    y_prompt, k_p, v_p, _, wkv_p, shift_p, conv_p = _trunk(
        x_prompt, p_prompt, W, None, None, wkv_zero, shift_zero, conv_zero)
    y_sample, k_s, v_s, va_s, wkv_s, shift_s, conv_s = _trunk(
        x_sample, p_sample, W, cache_sb_k, cache_sb_v, state_rwkv_wkv, state_rwkv_shift, state_ffn_conv)
    return (y_prompt, y_sample, k_p, v_p, wkv_p, shift_p, conv_p, k_s, v_s, va_s, wkv_s, shift_s, conv_s)
```

```python
import functools

import jax
import jax.numpy as jnp
from jax import lax
from jax.experimental import pallas as pl
from jax.experimental.pallas import tpu as pltpu

F32 = jnp.float32
BF = jnp.bfloat16

D_MODEL = 2048
DEPTH = 4
CHUNK = 64
A_WIDTH = D_MODEL // 2
A_HEADS = 8
A_DIM = A_WIDTH // A_HEADS
A_BLOCK = 128
B_HEAD_DIM = 64
B_WIDTH = D_MODEL // 2
B_HEADS = B_WIDTH // B_HEAD_DIM
C_HEAD_DIM = 64
C_HEADS = D_MODEL // C_HEAD_DIM
C_GN_EPS = 64e-5
D_FF = 5632
CONV_W = 3
LN_EPS = 1e-5
DN_ALPHA = (2 * DEPTH) ** 0.25

LANES = 128
SUBLANES = 8
HEAD_PAIR = 2 * C_HEAD_DIM
N_PAIRS = D_MODEL // HEAD_PAIR
WKV_CHUNK = 64
VMEM_CAP = 56 * 1024 * 1024

NT_DIMS = (((1,), (1,)), ((), ()))
TN_DIMS = (((0,), (0,)), ((), ()))


def _tile(m, pref):
    return pref if m % pref == 0 else m


def _params(sem, est_bytes):
    limit = int(min(max(est_bytes * 5 // 4 + (4 << 20), 16 << 20), VMEM_CAP))
    return pltpu.CompilerParams(dimension_semantics=sem, vmem_limit_bytes=limit)


def _nbytes(shape, dtype):
    n = 1
    for s in shape:
        n *= s
    return n * jnp.dtype(dtype).itemsize


def _layernorm(x, g, b, eps=LN_EPS):
    mu = jnp.mean(x, axis=-1, keepdims=True)
    d = x - mu
    var = jnp.mean(d * d, axis=-1, keepdims=True)
    return d * lax.rsqrt(var + eps) * g + b


def _bdot(a, b):
    return jnp.dot(a.astype(BF), b.astype(BF), preferred_element_type=F32)


def _mm_body(*refs, nprod, ntile, nrow, nout, nk, epilogue):
    xs = refs[:nprod]
    ws = refs[nprod:2 * nprod]
    ts = refs[2 * nprod:2 * nprod + ntile]
    rs = refs[2 * nprod + ntile:2 * nprod + ntile + nrow]
    outs = refs[2 * nprod + ntile + nrow:2 * nprod + ntile + nrow + nout]
    accs = refs[2 * nprod + ntile + nrow + nout:]

    def finish(vals):
        res = epilogue(vals, [t[...] for t in ts], [r[...] for r in rs])
        for o_ref, o in zip(outs, res):
            o_ref[...] = o.astype(o_ref.dtype)

    prods = [jnp.dot(x[...].astype(BF), w[...], preferred_element_type=F32) for x, w in zip(xs, ws)]
    if nk == 1:
        finish(prods)
        return
    k = pl.program_id(2)

    @pl.when(k == 0)
    def _():
        for a, p in zip(accs, prods):
            a[...] = p

    @pl.when(k > 0)
    def _():
        for a, p in zip(accs, prods):
            a[...] += p

    @pl.when(k == nk - 1)
    def _():
        finish([a[...] for a in accs])


def _mm(prods, epilogue, out_dtypes, *, n, tm, tn, tk=None, tiles=(), rows=()):
    m = prods[0][0].shape[0]
    tm = _tile(m, tm)
    ks = [x.shape[1] for x, _, _ in prods]
    nk = 1 if tk is None else ks[0] // tk
    if nk > 1:
        assert all(k == ks[0] for k in ks) and ks[0] % tk == 0
    assert n % tn == 0 and m % tm == 0
    grid = (n // tn, m // tm, nk)
    in_specs, est = [], 0
    for k_dim in ks:
        bk = k_dim if nk == 1 else tk
        in_specs.append(pl.BlockSpec((tm, bk), lambda j, i, k: (i, k)))
    for (x, w, off), k_dim in zip(prods, ks):
        assert off % tn == 0
        bk = k_dim if nk == 1 else tk
        in_specs.append(pl.BlockSpec((bk, tn), lambda j, i, k, ob=off // tn: (k, j + ob)))
        est += 2 * (_nbytes((tm, bk), x.dtype) + _nbytes((bk, tn), w.dtype))
    for t in tiles:
        in_specs.append(pl.BlockSpec((tm, tn), lambda j, i, k: (i, j)))
        est += 2 * _nbytes((tm, tn), t.dtype)
    for r in rows:
        in_specs.append(pl.BlockSpec((1, tn), lambda j, i, k: (0, j)))
    out_specs = [pl.BlockSpec((tm, tn), lambda j, i, k: (i, j)) for _ in out_dtypes]
    out_shape = [jax.ShapeDtypeStruct((m, n), dt) for dt in out_dtypes]
    est += sum(2 * _nbytes((tm, tn), dt) for dt in out_dtypes)
    est += (len(prods) + 2) * _nbytes((tm, tn), F32)
    scratch = [pltpu.VMEM((tm, tn), F32) for _ in prods] if nk > 1 else []
    body = functools.partial(_mm_body, nprod=len(prods), ntile=len(tiles), nrow=len(rows),
                             nout=len(out_dtypes), nk=nk, epilogue=epilogue)
    return pl.pallas_call(
        body, grid=grid, in_specs=in_specs, out_specs=out_specs, out_shape=out_shape,
        scratch_shapes=scratch,
        compiler_params=_params(("parallel", "parallel", "arbitrary"), est),
    )(*[p[0] for p in prods], *[p[1] for p in prods], *tiles, *rows)


def _row(v):
    return v.reshape(1, -1).astype(F32)


def _gate_body(u_ref, va_ref, ws_ref, bs_ref, o_ref, *, blk, nblk):
    ti = lax.broadcasted_iota(jnp.int32, (blk, blk), 0) // CHUNK
    si = lax.broadcasted_iota(jnp.int32, (blk, blk), 1) // CHUNK
    allowed = ti >= si
    for h in range(A_HEADS):
        w = jnp.where(allowed, ws_ref[h], 0.0).astype(BF)
        bias = bs_ref[h]
        for nb in range(nblk):
            rs = slice(nb * blk, (nb + 1) * blk)
            cs = slice(h * A_DIM, (h + 1) * A_DIM)
            mixed = jnp.dot(w, va_ref[rs, cs], preferred_element_type=F32) + bias
            o_ref[rs, cs] = (u_ref[rs, cs].astype(F32) * mixed).astype(o_ref.dtype)


def _spatial_gate(u, va, w_s, b_s, blk):
    m = u.shape[0]
    tm = _tile(m, 512)
    nblk = tm // blk
    est = 2 * 3 * _nbytes((tm, A_WIDTH), BF) + 2 * _nbytes((A_HEADS, blk, blk), F32)
    return pl.pallas_call(
        functools.partial(_gate_body, blk=blk, nblk=nblk),
        grid=(m // tm,),
        in_specs=[pl.BlockSpec((tm, A_WIDTH), lambda i: (i, 0)),
                  pl.BlockSpec((tm, A_WIDTH), lambda i: (i, 0)),
                  pl.BlockSpec((A_HEADS, blk, blk), lambda i: (0, 0, 0)),
                  pl.BlockSpec((A_HEADS, blk, 1), lambda i: (0, 0, 0))],
        out_specs=pl.BlockSpec((tm, A_WIDTH), lambda i: (i, 0)),
        out_shape=jax.ShapeDtypeStruct((m, A_WIDTH), BF),
        compiler_params=_params(("parallel",), est),
    )(u, va, w_s.astype(F32), b_s.astype(F32)[:, :, None])


def _sb_body(q_ref, kd_ref, vd_ref, kp_ref, vp_ref, o_ref, *, tq, tk, past_blocks):
    scale = B_HEAD_DIM ** -0.5
    q = q_ref[...]
    head_a_q = lax.broadcasted_iota(jnp.int32, (tq, LANES), 1) < B_HEAD_DIM
    zero_q = jnp.zeros_like(q)
    q_heads = (jnp.where(head_a_q, q, zero_q), jnp.where(head_a_q, zero_q, q))

    def block(k_blk, v_blk, carries, acc, diagonal):
        nkeys = k_blk.shape[0]
        kb = k_blk.astype(BF)
        vb = v_blk.astype(BF)
        jj = lax.broadcasted_iota(jnp.int32, (nkeys, nkeys), 0)
        ss = lax.broadcasted_iota(jnp.int32, (nkeys, nkeys), 1)
        later = jnp.where(jj > ss, 1.0, 0.0).astype(BF)
        if diagonal:
            tt = lax.broadcasted_iota(jnp.int32, (tq, nkeys), 0)
            sk = lax.broadcasted_iota(jnp.int32, (tq, nkeys), 1)
            valid = sk < tt
        wts, new_carries = [], []
        for qh, c in zip(q_heads, carries):
            z = lax.dot_general(qh, kb, NT_DIMS, preferred_element_type=F32) * scale
            softplus = jnp.maximum(z, 0.0) + jnp.log(1.0 + jnp.exp(-jnp.abs(z)))
            log_fail = -softplus
            if diagonal:
                log_fail = jnp.where(valid, log_fail, 0.0)
            hi = log_fail.astype(BF)
            lo = (log_fail - hi.astype(F32)).astype(BF)
            after = (jnp.dot(hi, later, preferred_element_type=F32)
                     + jnp.dot(lo, later, preferred_element_type=F32))
            w = jnp.exp(z - softplus + after + c)
            if diagonal:
                w = jnp.where(valid, w, 0.0)
            wts.append(w.astype(BF))
            new_carries.append(c + jnp.sum(log_fail, axis=-1, keepdims=True))
        head_a_k = lax.broadcasted_iota(jnp.int32, (nkeys, LANES), 1) < B_HEAD_DIM
        zero_v = jnp.zeros_like(vb)
        v_stack = jnp.concatenate([jnp.where(head_a_k, vb, zero_v), jnp.where(head_a_k, zero_v, vb)], axis=0)
        acc = acc + jnp.dot(jnp.concatenate(wts, axis=1), v_stack, preferred_element_type=F32)
        return tuple(new_carries), acc

    zeros_c = jnp.zeros((tq, 1), F32)
    carries, acc = block(kd_ref[...], vd_ref[...], (zeros_c, zeros_c),
                         jnp.zeros((tq, LANES), F32), True)
    n_past = past_blocks(pl.program_id(2))

    def step(it, state):
        ca, cb, acc = state
        off = pl.multiple_of((n_past - 1 - it) * tk, tk)
        (ca, cb), acc = block(kp_ref[pl.ds(off, tk), :], vp_ref[pl.ds(off, tk), :], (ca, cb), acc, False)
        return ca, cb, acc

    _, _, acc = lax.fori_loop(0, n_past, step, (carries[0], carries[1], acc))
    o_ref[...] = acc.astype(o_ref.dtype)


def _stick_breaking(q, k_new, v_new, k_past, v_past, *, tq, tk, causal_past):
    bsz, t, _ = q.shape
    p_len = k_past.shape[1]
    if causal_past:
        past_blocks = lambda qi: qi * (tq // tk)
    else:
        past_blocks = lambda qi: p_len // tk
    est = (2 * 2 * _nbytes((p_len, LANES), k_past.dtype) + 2 * 4 * _nbytes((tq, LANES), F32)
           + 12 * _nbytes((tq, max(tq, tk)), F32))
    return pl.pallas_call(
        functools.partial(_sb_body, tq=tq, tk=tk, past_blocks=past_blocks),
        grid=(bsz, B_WIDTH // LANES, t // tq),
        in_specs=[pl.BlockSpec((None, tq, LANES), lambda b, p, i: (b, i, p)),
                  pl.BlockSpec((None, tq, LANES), lambda b, p, i: (b, i, p)),
                  pl.BlockSpec((None, tq, LANES), lambda b, p, i: (b, i, p)),
                  pl.BlockSpec((None, p_len, LANES), lambda b, p, i: (b, 0, p)),
                  pl.BlockSpec((None, p_len, LANES), lambda b, p, i: (b, 0, p))],
        out_specs=pl.BlockSpec((None, tq, LANES), lambda b, p, i: (b, i, p)),
        out_shape=jax.ShapeDtypeStruct((bsz, t, B_WIDTH), BF),
        compiler_params=_params(("parallel", "parallel", "arbitrary"), est),
    )(q, k_new, v_new, k_past, v_past)


def _shift_body(x_ref, last_ref, mu_ref, *rest, tiles_per_seq):
    outs, carry = rest[:6], rest[6]
    i = pl.program_id(1)

    @pl.when(i % tiles_per_seq == 0)
    def _():
        carry[...] = jnp.broadcast_to(last_ref[...], carry.shape)

    x = x_ref[...]
    rows = lax.broadcasted_iota(jnp.int32, x.shape, 0)
    x_prev = jnp.where(rows == 0, carry[SUBLANES - 1:SUBLANES, :], pltpu.roll(x, 1, 0))
    carry[...] = x[x.shape[0] - SUBLANES:, :]
    xx = x_prev - x
    for j, o_ref in enumerate(outs):
        o_ref[...] = (x + xx * mu_ref[j:j + 1, :]).astype(o_ref.dtype)


def _token_shift(x, x_last, mu):
    bsz, t, d = x.shape
    tm = _tile(t, 256)
    est = 2 * _nbytes((tm, d), F32) + 12 * _nbytes((tm, d), BF) + 4 * _nbytes((tm, d), F32)
    return pl.pallas_call(
        functools.partial(_shift_body, tiles_per_seq=t // tm),
        grid=(bsz, t // tm),
        in_specs=[pl.BlockSpec((None, tm, d), lambda b, i: (b, i, 0)),
                  pl.BlockSpec((None, 1, d), lambda b, i: (b, 0, 0)),
                  pl.BlockSpec((6, d), lambda b, i: (0, 0))],
        out_specs=[pl.BlockSpec((None, tm, d), lambda b, i: (b, i, 0)) for _ in range(6)],
        out_shape=[jax.ShapeDtypeStruct((bsz, t, d), BF) for _ in range(6)],
        scratch_shapes=[pltpu.VMEM((SUBLANES, d), F32)],
        compiler_params=_params(("parallel", "arbitrary"), est),
    )(x, x_last[:, None, :].astype(F32), mu.astype(F32))


def _wkv_body(r_ref, k_ref, v_ref, lw_ref, a_ref, g_ref, kk_ref, ka_ref, rk_ref, gg_ref, gb_ref,
              s0_ref, y_ref, s_ref):
    c = WKV_CHUNK
    n = 2 * c

    @pl.when(pl.program_id(1) == 0)
    def _():
        s_ref[...] = s0_ref[...]

    head_a = lax.broadcasted_iota(jnp.int32, (c, LANES), 1) < C_HEAD_DIM
    ri = lax.broadcasted_iota(jnp.int32, (n, n), 0)
    ci = lax.broadcasted_iota(jnp.int32, (n, n), 1)
    same_head = (ri // c) == (ci // c)
    strict = same_head & (ci < ri)
    inclusive = same_head & (ci <= ri)
    diag16 = (ri // 16) == (ci // 16)
    diag32 = (ri // 32) == (ci // 32)
    eye = jnp.where(ri == ci, 1.0, 0.0).astype(F32)
    seg_ones = jnp.where(same_head, 1.0, 0.0).astype(BF)
    ti = lax.broadcasted_iota(jnp.int32, (c, c), 0)
    si = lax.broadcasted_iota(jnp.int32, (c, c), 1)
    prefix = jnp.where(si <= ti, 1.0, 0.0).astype(BF)

    def split2(x):
        hi = x.astype(BF)
        return hi, (x - hi.astype(F32)).astype(BF)

    def seg_sum(x):
        hi, lo = split2(x)
        return (jnp.dot(hi, seg_ones, preferred_element_type=F32)
                + jnp.dot(lo, seg_ones, preferred_element_type=F32))

    def cumsum_time(x):
        hi, rest = split2(x)
        rest_f = x - hi.astype(F32)
        mid = rest_f.astype(BF)
        lo = (rest_f - mid.astype(F32)).astype(BF)
        return (jnp.dot(prefix, hi, preferred_element_type=F32)
                + jnp.dot(prefix, mid, preferred_element_type=F32)
                + jnp.dot(prefix, lo, preferred_element_type=F32))

    def stack(x):
        zero = jnp.zeros_like(x)
        return jnp.concatenate([jnp.where(head_a, x, zero), jnp.where(head_a, zero, x)], axis=0)

    def pair(p, carry):
        sl = pl.ds(pl.multiple_of(p * LANES, LANES), LANES)
        r = r_ref[:, sl]
        k = k_ref[:, sl]
        v = v_ref[:, sl]
        lw = lw_ref[:, sl]
        a = a_ref[:, sl]
        kkp = k * kk_ref[:, sl]
        norm = jnp.sqrt(seg_sum(kkp * kkp))
        kk = kkp / jnp.maximum(norm, 1e-12)
        k2 = k * (1.0 + (a - 1.0) * ka_ref[:, sl])
        cs = cumsum_time(lw)
        e_pos = jnp.exp(cs)
        e_neg = jnp.exp(-cs)
        a_t = -kk * e_pos * jnp.exp(-lw)
        b_t = kk * a * e_neg
        k_t = k2 * e_neg
        r_t = r * e_pos
        la, lr = stack(a_t), stack(r_t)
        rb, rk = stack(b_t), stack(k_t)
        vs = stack(v)
        q = lax.dot_general(jnp.concatenate([la, lr], axis=0).astype(BF),
                            jnp.concatenate([rb, rk], axis=0).astype(BF),
                            NT_DIMS, preferred_element_type=F32)
        p_ab = jnp.where(strict, q[:n, :n], 0.0)
        a_ak = jnp.where(strict, q[:n, n:], 0.0)
        a_rb = jnp.where(inclusive, q[n:, :n], 0.0)
        a_rk = jnp.where(inclusive, q[n:, n:], 0.0)
        pd = jnp.where(diag16, p_ab, 0.0)
        inv = eye + pd
        pw = pd
        for _ in range(3):
            pw = _bdot(pw, pw)
            inv = inv + _bdot(pw, inv)
        e1 = jnp.where(diag32 & (~diag16), p_ab, 0.0)
        inv = inv + _bdot(inv, _bdot(e1, inv))
        e2 = jnp.where(diag32, 0.0, p_ab)
        inv = inv + _bdot(inv, _bdot(e2, inv))
        akv = _bdot(a_ak, vs)
        wu = _bdot(inv, jnp.concatenate([la, akv], axis=1))
        w_m, u_m = wu[:, :LANES], wu[:, LANES:]
        s = s_ref[p]
        wr = lax.dot_general(jnp.concatenate([w_m, lr], axis=0).astype(BF), s.astype(BF),
                             NT_DIMS, preferred_element_type=F32)
        sa = wr[:n] + u_m
        sv = jnp.concatenate([sa, vs], axis=0)
        y_st = wr[n:] + _bdot(jnp.concatenate([a_rb, a_rk], axis=1), sv)
        y = y_st[:c] + y_st[c:]
        g_end = e_pos[c - 1:c, :]
        rg = jnp.concatenate([rb, rk], axis=0) * g_end
        s_ref[p] = s * g_end + lax.dot_general(sv.astype(BF), rg.astype(BF), TN_DIMS,
                                               preferred_element_type=F32)
        mean = seg_sum(y) * (1.0 / C_HEAD_DIM)
        d = y - mean
        var = seg_sum(d * d) * (1.0 / C_HEAD_DIM)
        yn = d * lax.rsqrt(var + C_GN_EPS) * gg_ref[:, sl] + gb_ref[:, sl]
        bonus = seg_sum(r * k2 * rk_ref[:, sl]) * v
        y_ref[:, sl] = ((yn + bonus) * g_ref[:, sl]).astype(y_ref.dtype)
        return carry

    lax.fori_loop(0, N_PAIRS, pair, 0)


def _pair_state(s):
    bsz = s.shape[0]
    s = s.reshape(bsz, N_PAIRS, 2, C_HEAD_DIM, C_HEAD_DIM).astype(F32)
    z = jnp.zeros_like(s[:, :, 0])
    top = jnp.concatenate([s[:, :, 0], z], axis=-1)
    bot = jnp.concatenate([z, s[:, :, 1]], axis=-1)
    return jnp.concatenate([top, bot], axis=-2)


def _unpair_state(sp):
    bsz = sp.shape[0]
    a = sp[:, :, :C_HEAD_DIM, :C_HEAD_DIM]
    b = sp[:, :, C_HEAD_DIM:, C_HEAD_DIM:]
    return jnp.stack([a, b], axis=2).reshape(bsz, C_HEADS, C_HEAD_DIM, C_HEAD_DIM)


def _wkv(r, k, v, lw, a, g, s0, k_k, k_a, r_k, gn_g, gn_b):
    bsz, t, d = r.shape
    c = WKV_CHUNK
    seq = pl.BlockSpec((None, c, d), lambda b, i: (b, i, 0))
    vec = pl.BlockSpec((1, d), lambda b, i: (0, 0))
    st = pl.BlockSpec((None, N_PAIRS, HEAD_PAIR, HEAD_PAIR), lambda b, i: (b, 0, 0, 0))
    est = 2 * 6 * _nbytes((c, d), F32) + 4 * _nbytes((N_PAIRS, HEAD_PAIR, HEAD_PAIR), F32) + (8 << 20)
    y, s_new = pl.pallas_call(
        _wkv_body,
        grid=(bsz, t // c),
        in_specs=[seq] * 6 + [vec] * 5 + [st],
        out_specs=[seq, st],
        out_shape=[jax.ShapeDtypeStruct((bsz, t, d), BF),
                   jax.ShapeDtypeStruct((bsz, N_PAIRS, HEAD_PAIR, HEAD_PAIR), F32)],
        compiler_params=_params(("parallel", "arbitrary"), est),
    )(r, k, v, lw, a, g, _row(k_k), _row(k_a), _row(r_k), _row(gn_g), _row(gn_b), _pair_state(s0))
    return y, _unpair_state(s_new)


def _ffn_in_body(x_ref, wg_ref, wu_ref, cw_ref, cb_ref, prev_ref, act_ref, st_ref, tail_ref, *,
                 seq_len, tm):
    x = x_ref[...]
    hg = jnp.dot(x, wg_ref[...], preferred_element_type=F32)
    hu = jnp.dot(x, wu_ref[...], preferred_element_type=F32)
    tn = hg.shape[1]
    rows = lax.broadcasted_iota(jnp.int32, hg.shape, 0)
    last2, last1 = SUBLANES - 2, SUBLANES - 1
    if seq_len >= tm:
        @pl.when(pl.program_id(1) % (seq_len // tm) == 0)
        def _():
            tail_ref[...] = prev_ref[0]
        tail = tail_ref[...]
        p0, p1 = tail[last2:last2 + 1, :], tail[last1:last1 + 1, :]
        pos = rows
        new_tail = hg[tm - SUBLANES:, :]
        tail_ref[...] = new_tail
        st_ref[0] = new_tail
    else:
        nseq = tm // seq_len
        prev = prev_ref[...]
        spread = lambda rw: jnp.broadcast_to(rw, (nseq, seq_len, tn)).reshape(tm, tn)
        p0, p1 = spread(prev[:, last2:last2 + 1, :]), spread(prev[:, last1:last1 + 1, :])
        pos = rows % seq_len
        st_ref[...] = hg.reshape(nseq, seq_len, tn)[:, seq_len - SUBLANES:, :]
    s1 = jnp.where(pos == 0, p1, pltpu.roll(hg, 1, 0))
    s2 = jnp.where(pos == 0, p0, jnp.where(pos == 1, p1, pltpu.roll(hg, 2, 0)))
    cw = cw_ref[...]
    hc = cb_ref[...] + s2 * cw[0:1, :] + s1 * cw[1:2, :] + hg * cw[2:3, :]
    act_ref[...] = (jax.nn.gelu(hc) * hu).astype(act_ref.dtype)


def _ffn_in(xb, w_in, conv_w, conv_b, conv_prev, seq_len):
    m, d = xb.shape
    nseq_total = m // seq_len
    tn = 512
    tm = _tile(m, 512)
    nj = D_FF // tn
    prev8 = jnp.pad(conv_prev.astype(F32), ((0, 0), (SUBLANES - (CONV_W - 1), 0), (0, 0)))
    if seq_len >= tm:
        per = seq_len // tm
        nb = 1
        st_map = lambda j, i: (i // per, 0, j)
    else:
        nb = tm // seq_len
        st_map = lambda j, i: (i, 0, j)
    st_spec = pl.BlockSpec((nb, SUBLANES, tn), st_map)
    est = (2 * _nbytes((tm, d), BF) + 4 * _nbytes((d, tn), BF) + 2 * _nbytes((tm, tn), BF)
           + 8 * _nbytes((tm, tn), F32))
    act, tails = pl.pallas_call(
        functools.partial(_ffn_in_body, seq_len=seq_len, tm=tm),
        grid=(nj, m // tm),
        in_specs=[pl.BlockSpec((tm, d), lambda j, i: (i, 0)),
                  pl.BlockSpec((d, tn), lambda j, i: (0, j)),
                  pl.BlockSpec((d, tn), lambda j, i: (0, j + nj)),
                  pl.BlockSpec((CONV_W, tn), lambda j, i: (0, j)),
                  pl.BlockSpec((1, tn), lambda j, i: (0, j)),
                  st_spec],
        out_specs=[pl.BlockSpec((tm, tn), lambda j, i: (i, j)), st_spec],
        out_shape=[jax.ShapeDtypeStruct((m, D_FF), BF),
                   jax.ShapeDtypeStruct((nseq_total, SUBLANES, D_FF), F32)],
        scratch_shapes=[pltpu.VMEM((SUBLANES, tn), F32)],
        compiler_params=_params(("parallel", "arbitrary"), est),
    )(xb, w_in, w_in, conv_w.astype(F32), _row(conv_b), prev8)
    return act, tails[:, SUBLANES - (CONV_W - 1):, :]


def _ln_residual_epilogue(accs, tiles, rows):
    mix = accs[0]
    for extra in accs[1:]:
        mix = mix + extra
    y = _layernorm(DN_ALPHA * tiles[0] + mix, rows[0], rows[1])
    return y, y


def _identity_epilogue(accs, tiles, rows):
    return (accs[0],)


def _even_mixer(x, xb, seq_len, w_in, lnv_g, lnv_b, w_s, b_s, w_o, ln_g, ln_b, cache):
    m = x.shape[0]
    bsz = m // seq_len
    proj = lambda off, ep, dts, tn=1024, rows=(): _mm([(xb, w_in, off)], ep, dts, n=1024, tm=512,
                                                     tn=tn, rows=rows)
    (u,) = proj(0, lambda a, t, r: (jax.nn.gelu(a[0]),), [BF])

    def va_epilogue(a, t, r):
        va = _layernorm(jax.nn.gelu(a[0]), r[0], r[1])
        return va, va
    va, vab = proj(A_WIDTH, va_epilogue, [F32, BF], rows=(_row(lnv_g), _row(lnv_b)))
    o = 2 * A_WIDTH
    (q,) = proj(o, _identity_epilogue, [BF])
    (k,) = proj(o + B_WIDTH, _identity_epilogue, [F32])
    (v,) = proj(o + 2 * B_WIDTH, _identity_epilogue, [F32])

    blk = min(seq_len, A_BLOCK)
    a_out = _spatial_gate(u, vab, w_s[:, :blk, :blk], b_s[:, :blk], blk)

    q3 = q.reshape(bsz, seq_len, B_WIDTH)
    k3 = k.reshape(bsz, seq_len, B_WIDTH)
    v3 = v.reshape(bsz, seq_len, B_WIDTH)
    if cache is None:
        tq = _tile(seq_len, 128)
        b_out = _stick_breaking(q3, k3, v3, k3, v3, tq=tq, tk=tq, causal_past=True)
    else:
        k_cache, v_cache = cache
        past = k_cache.shape[1]
        b_out = _stick_breaking(q3, k3, v3, k_cache.reshape(bsz, past, B_WIDTH),
                                v_cache.reshape(bsz, past, B_WIDTH),
                                tq=seq_len, tk=_tile(past, 128), causal_past=False)
    b_out = b_out.reshape(m, B_WIDTH)

    x1, x1b = _mm([(a_out, w_o, 0), (b_out, w_o[A_WIDTH:], 0)], _ln_residual_epilogue, [F32, BF],
                  n=D_MODEL, tm=256, tn=D_MODEL, tiles=(x,), rows=(_row(ln_g), _row(ln_b)))
    return x1, x1b, k, v, va


def _rwkv_mixer(x, seq_len, x_last, s0, v_first, P, ln_g, ln_b):
    m = x.shape[0]
    bsz = m // seq_len
    mixes = _token_shift(x.reshape(bsz, seq_len, D_MODEL), x_last, P['mu'])
    xr, xw, xk, xv, xa, xg = [a.reshape(m, D_MODEL) for a in mixes]
    big = lambda xin, w: _mm([(xin, w, 0)], _identity_epilogue, [F32], n=D_MODEL, tm=512, tn=1024)[0]
    r = big(xr, P['w_r'])
    k = big(xk, P['w_k'])

    def lora_in(xin, w, act):
        n = w.shape[1]
        return _mm([(xin, w, 0)], lambda a, t, r_: (act(a[0]),), [BF], n=n, tm=512, tn=n)[0]

    def lora_out(h, w, bias, act):
        return _mm([(h, w, 0)], lambda a, t, r_: (act(r_[0] + a[0]),), [F32], n=D_MODEL, tm=512,
                   tn=1024, rows=(_row(bias),))[0]

    hw = lora_in(xw, P['w1'], jnp.tanh)
    lw = lora_out(hw, P['w2'], P['w0'],
                  lambda pre: -jnp.exp(-(jnp.maximum(-pre, 0.0) + jnp.log(1.0 + jnp.exp(-jnp.abs(pre)))) - 0.5))
    ha = lora_in(xa, P['a1'], lambda z: z)
    a = lora_out(ha, P['a2'], P['a0'], jax.nn.sigmoid)
    hg = lora_in(xg, P['g1'], jax.nn.sigmoid)
    g = _mm([(hg, P['g2'], 0)], _identity_epilogue, [F32], n=D_MODEL, tm=512, tn=1024)[0]
    if P['v1'] is None:
        v = big(xv, P['w_v'])
        v_first = v
    else:
        hv = lora_in(xv, P['v1'], lambda z: z)

        def v_epilogue(accs, tiles, rows):
            v_new = accs[0]
            return (v_new + (tiles[0] - v_new) * jax.nn.sigmoid(rows[0] + accs[1]),)
        v = _mm([(xv, P['w_v'], 0), (hv, P['v2'], 0)], v_epilogue, [F32], n=D_MODEL, tm=512, tn=1024,
                tiles=(v_first,), rows=(_row(P['v0']),))[0]
    to3 = lambda z: z.reshape(bsz, seq_len, D_MODEL)
    yg, s_new = _wkv(to3(r), to3(k), to3(v), to3(lw), to3(a), to3(g), s0,
                     P['k_k'], P['k_a'], P['r_k'], P['gn_g'], P['gn_b'])
    x1, x1b = _mm([(yg.reshape(m, D_MODEL), P['w_o'], 0)], _ln_residual_epilogue, [F32, BF],
                  n=D_MODEL, tm=256, tn=D_MODEL, tiles=(x,), rows=(_row(ln_g), _row(ln_b)))
    return x1, x1b, s_new, v_first


def _ffn_ple(x1, x1b, seq_len, p, conv_prev, w_in, conv_w, conv_b, w_out, ln_g, ln_b, ple_gate, ple_proj):
    act, conv_rows = _ffn_in(x1b, w_in, conv_w, conv_b, conv_prev, seq_len)
    x2, x2b = _mm([(act, w_out, 0)], _ln_residual_epilogue, [F32, BF], n=D_MODEL, tm=512, tn=D_MODEL,
                  tk=512, tiles=(x1,), rows=(_row(ln_g), _row(ln_b)))

    def ple_epilogue(accs, tiles, rows):
        y = tiles[0] + jax.nn.sigmoid(accs[0]) * accs[1]
        return y, y
    x3, x3b = _mm([(x2b, ple_gate, 0), (p, ple_proj, 0)], ple_epilogue, [F32, BF], n=D_MODEL, tm=512,
                  tn=1024, tiles=(x2,))
    return x3, x3b, conv_rows


def _trunk(x, p, W, caches, wkv0, shift0, conv0):
    bsz, seq_len, _ = x.shape
    m = bsz * seq_len
    x = x.reshape(m, D_MODEL).astype(F32)
    xb = x.astype(BF)
    new_k, new_v, new_va, new_wkv, new_shift, new_conv = [], [], [], [], [], []
    v_first = None
    for i in range(DEPTH):
        j = i // 2
        if i % 2 == 0:
            cache = None if caches is None else (caches[0][j], caches[1][j])
            x1, x1b, k, v, va = _even_mixer(
                x, xb, seq_len, W['even_w_in'][j], W['even_lnv_g'][j], W['even_lnv_b'][j],
                W['even_w_s'][j], W['even_b_s'][j], W['even_w_o'][j], W['ln1_g'][i], W['ln1_b'][i], cache)
            new_k.append(k.reshape(bsz, seq_len, B_HEADS, B_HEAD_DIM))
            new_v.append(v.reshape(bsz, seq_len, B_HEADS, B_HEAD_DIM))
            new_va.append(va.reshape(bsz, seq_len, A_HEADS, A_DIM))
        else:
            P = dict(mu=W['c_mu'][j], w_r=W['c_w_r'][j], w_k=W['c_w_k'][j], w_v=W['c_w_v'][j],
                     w_o=W['c_w_o'][j], w0=W['c_w0'][j], w1=W['c_w1'][j], w2=W['c_w2'][j],
                     a0=W['c_a0'][j], a1=W['c_a1'][j], a2=W['c_a2'][j], g1=W['c_g1'][j], g2=W['c_g2'][j],
                     k_k=W['c_k_k'][j], k_a=W['c_k_a'][j], r_k=W['c_r_k'][j], gn_g=W['c_gn_g'][j],
                     gn_b=W['c_gn_b'][j],
                     v0=None if j == 0 else W['c_v0'][j - 1], v1=None if j == 0 else W['c_v1'][j - 1],
                     v2=None if j == 0 else W['c_v2'][j - 1])
            x1, x1b, s_new, v_first = _rwkv_mixer(x, seq_len, shift0[j], wkv0[j], v_first, P,
                                                  W['ln1_g'][i], W['ln1_b'][i])
            new_wkv.append(s_new)
            new_shift.append(x.reshape(bsz, seq_len, D_MODEL)[:, -1])
        x, xb, conv_rows = _ffn_ple(
            x1, x1b, seq_len, p[i].reshape(m, -1), conv0[i], W['ffn_w_in'][i], W['ffn_conv_w'][i],
            W['ffn_conv_b'][i], W['ffn_w_out'][i], W['ln2_g'][i], W['ln2_b'][i], W['ple_gate'][i],
            W['ple_proj'][i])
        new_conv.append(conv_rows)
    st = lambda l: jnp.stack(l) if l else None
    return (x.reshape(bsz, seq_len, D_MODEL), st(new_k), st(new_v), st(new_va), st(new_wkv),
            st(new_shift), st(new_conv))


_MATRICES = ('even_w_in', 'even_w_o', 'c_w_r', 'c_w_k', 'c_w_v', 'c_w_o', 'c_w1', 'c_w2', 'c_a1', 'c_a2',
             'c_v1', 'c_v2', 'c_g1', 'c_g2', 'ffn_w_in', 'ffn_w_out', 'ple_proj', 'ple_gate')


def kernel(x_prompt, x_sample, cache_sb_k, cache_sb_v, state_rwkv_wkv, state_rwkv_shift, state_ffn_conv, p_prompt, p_sample, even_w_in, even_lnv_g, even_lnv_b, even_w_s, even_b_s, even_w_o, c_mu, c_w_r, c_w_k, c_w_v, c_w_o, c_w0, c_w1, c_w2, c_a0, c_a1, c_a2, c_v0, c_v1, c_v2, c_g1, c_g2, c_k_k, c_k_a, c_r_k, c_gn_g, c_gn_b, ffn_w_in, ffn_conv_w, ffn_conv_b, ffn_w_out, ln1_g, ln1_b, ln2_g, ln2_b, ple_proj, ple_gate):
    W = dict(even_w_in=even_w_in, even_lnv_g=even_lnv_g, even_lnv_b=even_lnv_b, even_w_s=even_w_s,
             even_b_s=even_b_s, even_w_o=even_w_o, c_mu=c_mu, c_w_r=c_w_r, c_w_k=c_w_k,
             c_w_v=c_w_v, c_w_o=c_w_o, c_w0=c_w0, c_w1=c_w1, c_w2=c_w2, c_a0=c_a0, c_a1=c_a1,
             c_a2=c_a2, c_v0=c_v0, c_v1=c_v1, c_v2=c_v2, c_g1=c_g1, c_g2=c_g2, c_k_k=c_k_k,
             c_k_a=c_k_a, c_r_k=c_r_k, c_gn_g=c_gn_g, c_gn_b=c_gn_b, ffn_w_in=ffn_w_in,
             ffn_conv_w=ffn_conv_w, ffn_conv_b=ffn_conv_b, ffn_w_out=ffn_w_out, ln1_g=ln1_g,
             ln1_b=ln1_b, ln2_g=ln2_g, ln2_b=ln2_b, ple_proj=ple_proj, ple_gate=ple_gate)
    for name in _MATRICES:
        W[name] = W[name].astype(BF)
    bp = x_prompt.shape[0]
    n_odd, n_even = state_rwkv_wkv.shape[0], cache_sb_k.shape[0]
    wkv_zero = jnp.zeros((n_odd, bp, C_HEADS, C_HEAD_DIM, C_HEAD_DIM), F32)
    shift_zero = jnp.zeros((n_odd, bp, D_MODEL), F32)
    conv_zero = jnp.zeros((DEPTH, bp, CONV_W - 1, D_FF), F32)
    y_p, k_p, v_p, _, wkv_p, shift_p, conv_p = _trunk(
        x_prompt, p_prompt, W, None, wkv_zero, shift_zero, conv_zero)
    y_s, k_s, v_s, va_s, wkv_s, shift_s, conv_s = _trunk(
        x_sample, p_sample, W, (cache_sb_k, cache_sb_v), state_rwkv_wkv, state_rwkv_shift, state_ffn_conv)
    return (y_p, y_s, k_p, v_p, wkv_p, shift_p, conv_p, k_s, v_s, va_s, wkv_s, shift_s, conv_s)
```

```python
import functools

import jax
import jax.numpy as jnp
from jax import lax
from jax.experimental import pallas as pl
from jax.experimental.pallas import tpu as pltpu

F32 = jnp.float32
BF = jnp.bfloat16

D_MODEL = 2048
DEPTH = 4
CHUNK = 64
A_WIDTH = D_MODEL // 2
A_HEADS = 8
A_DIM = A_WIDTH // A_HEADS
A_BLOCK = 128
B_HEAD_DIM = 64
B_WIDTH = D_MODEL // 2
B_HEADS = B_WIDTH // B_HEAD_DIM
C_HEAD_DIM = 64
C_HEADS = D_MODEL // C_HEAD_DIM
C_GN_EPS = 64e-5
D_FF = 5632
CONV_W = 3
LN_EPS = 1e-5
DN_ALPHA = (2 * DEPTH) ** 0.25

LANES = 128
SUBLANES = 8
HEAD_PAIR = 2 * C_HEAD_DIM
N_PAIRS = D_MODEL // HEAD_PAIR
WKV_CHUNK = 64
WKV_PAIRS_PER_STEP = 16
SB_BLOCK = 256
SB_CUMSUM_PASSES = 1
VMEM_CAP = 56 * 1024 * 1024

NT_DIMS = (((1,), (1,)), ((), ()))
TN_DIMS = (((0,), (0,)), ((), ()))


def _tile(m, pref):
    return pref if m % pref == 0 else m


def _params(sem, est_bytes):
    limit = int(min(max(est_bytes * 5 // 4 + (4 << 20), 16 << 20), VMEM_CAP))
    return pltpu.CompilerParams(dimension_semantics=sem, vmem_limit_bytes=limit)


def _nbytes(shape, dtype):
    n = 1
    for s in shape:
        n *= s
    return n * jnp.dtype(dtype).itemsize


def _layernorm(x, g, b, eps=LN_EPS):
    mu = jnp.mean(x, axis=-1, keepdims=True)
    d = x - mu
    var = jnp.mean(d * d, axis=-1, keepdims=True)
    return d * lax.rsqrt(var + eps) * g + b


def _bdot(a, b):
    return jnp.dot(a.astype(BF), b.astype(BF), preferred_element_type=F32)


def _run_in_lockstep(generators):
    results = [None] * len(generators)
    live = list(range(len(generators)))
    while live:
        for idx in list(live):
            try:
                next(generators[idx])
            except StopIteration as done:
                results[idx] = done.value
                live.remove(idx)
    return results


def _mm_body(*refs, nprod, ntile, nrow, nout, nk, epilogue):
    xs = refs[:nprod]
    ws = refs[nprod:2 * nprod]
    ts = refs[2 * nprod:2 * nprod + ntile]
    rs = refs[2 * nprod + ntile:2 * nprod + ntile + nrow]
    outs = refs[2 * nprod + ntile + nrow:2 * nprod + ntile + nrow + nout]
    accs = refs[2 * nprod + ntile + nrow + nout:]

    def finish(vals):
        res = epilogue(vals, [t[...] for t in ts], [r[...] for r in rs])
        for o_ref, o in zip(outs, res):
            o_ref[...] = o.astype(o_ref.dtype)

    prods = [jnp.dot(x[...].astype(BF), w[...], preferred_element_type=F32) for x, w in zip(xs, ws)]
    if nk == 1:
        finish(prods)
        return
    k = pl.program_id(2)

    @pl.when(k == 0)
    def _():
        for a, p in zip(accs, prods):
            a[...] = p

    @pl.when(k > 0)
    def _():
        for a, p in zip(accs, prods):
            a[...] += p

    @pl.when(k == nk - 1)
    def _():
        finish([a[...] for a in accs])


def _mm(prods, epilogue, out_dtypes, *, n, tm, tn, tk=None, tiles=(), rows=(), name="mm"):
    m = prods[0][0].shape[0]
    tm = _tile(m, tm)
    ks = [x.shape[1] for x, _, _ in prods]
    nk = 1 if tk is None else ks[0] // tk
    if nk > 1:
        assert all(k == ks[0] for k in ks) and ks[0] % tk == 0
    assert n % tn == 0 and m % tm == 0
    grid = (n // tn, m // tm, nk)
    in_specs, est = [], 0
    for k_dim in ks:
        bk = k_dim if nk == 1 else tk
        in_specs.append(pl.BlockSpec((tm, bk), lambda j, i, k: (i, k)))
    for (x, w, off), k_dim in zip(prods, ks):
        assert off % tn == 0
        bk = k_dim if nk == 1 else tk
        in_specs.append(pl.BlockSpec((bk, tn), lambda j, i, k, ob=off // tn: (k, j + ob)))
        est += 2 * (_nbytes((tm, bk), x.dtype) + _nbytes((bk, tn), w.dtype))
    for t in tiles:
        in_specs.append(pl.BlockSpec((tm, tn), lambda j, i, k: (i, j)))
        est += 2 * _nbytes((tm, tn), t.dtype)
    for r in rows:
        in_specs.append(pl.BlockSpec((1, tn), lambda j, i, k: (0, j)))
    out_specs = [pl.BlockSpec((tm, tn), lambda j, i, k: (i, j)) for _ in out_dtypes]
    out_shape = [jax.ShapeDtypeStruct((m, n), dt) for dt in out_dtypes]
    est += sum(2 * _nbytes((tm, tn), dt) for dt in out_dtypes)
    est += (len(prods) + 2) * _nbytes((tm, tn), F32)
    scratch = [pltpu.VMEM((tm, tn), F32) for _ in prods] if nk > 1 else []
    body = functools.partial(_mm_body, nprod=len(prods), ntile=len(tiles), nrow=len(rows),
                             nout=len(out_dtypes), nk=nk, epilogue=epilogue)
    return pl.pallas_call(
        body, name=name, grid=grid, in_specs=in_specs, out_specs=out_specs, out_shape=out_shape,
        scratch_shapes=scratch,
        compiler_params=_params(("parallel", "parallel", "arbitrary"), est),
    )(*[p[0] for p in prods], *[p[1] for p in prods], *tiles, *rows)


def _row(v):
    return v.reshape(1, -1).astype(F32)


def _gate_body(u_ref, va_ref, ws_ref, bs_ref, o_ref, *, blk, nblk):
    ti = lax.broadcasted_iota(jnp.int32, (blk, blk), 0) // CHUNK
    si = lax.broadcasted_iota(jnp.int32, (blk, blk), 1) // CHUNK
    allowed = ti >= si
    for h in range(A_HEADS):
        w = jnp.where(allowed, ws_ref[h], 0.0).astype(BF)
        bias = bs_ref[h]
        for nb in range(nblk):
            rs = slice(nb * blk, (nb + 1) * blk)
            cs = slice(h * A_DIM, (h + 1) * A_DIM)
            mixed = jnp.dot(w, va_ref[rs, cs], preferred_element_type=F32) + bias
            o_ref[rs, cs] = (u_ref[rs, cs].astype(F32) * mixed).astype(o_ref.dtype)


def _spatial_gate(u, va, w_s, b_s, blk):
    m = u.shape[0]
    tm = _tile(m, 512)
    nblk = tm // blk
    est = 2 * 3 * _nbytes((tm, A_WIDTH), BF) + 2 * _nbytes((A_HEADS, blk, blk), F32)
    return pl.pallas_call(
        functools.partial(_gate_body, blk=blk, nblk=nblk),
        name="spatial_gate",
        grid=(m // tm,),
        in_specs=[pl.BlockSpec((tm, A_WIDTH), lambda i: (i, 0)),
                  pl.BlockSpec((tm, A_WIDTH), lambda i: (i, 0)),
                  pl.BlockSpec((A_HEADS, blk, blk), lambda i: (0, 0, 0)),
                  pl.BlockSpec((A_HEADS, blk, 1), lambda i: (0, 0, 0))],
        out_specs=pl.BlockSpec((tm, A_WIDTH), lambda i: (i, 0)),
        out_shape=jax.ShapeDtypeStruct((m, A_WIDTH), BF),
        compiler_params=_params(("parallel",), est),
    )(u, va, w_s.astype(F32), b_s.astype(F32)[:, :, None])


def _sb_body(q_ref, kd_ref, vd_ref, kp_ref, vp_ref, o_ref, *, tq, tk, past_blocks, unroll):
    q = q_ref[...]
    head_a_q = lax.broadcasted_iota(jnp.int32, (tq, LANES), 1) < B_HEAD_DIM
    zero_q = jnp.zeros_like(q)
    q_heads = (jnp.where(head_a_q, q, zero_q), jnp.where(head_a_q, zero_q, q))

    def block(k_blk, v_blk, carries, acc, diagonal):
        nkeys = k_blk.shape[0]
        kb = k_blk.astype(BF)
        vb = v_blk.astype(BF)
        jj = lax.broadcasted_iota(jnp.int32, (nkeys, nkeys), 0)
        ss = lax.broadcasted_iota(jnp.int32, (nkeys, nkeys), 1)
        later = jnp.where(jj > ss, 1.0, 0.0).astype(BF)
        if diagonal:
            tt = lax.broadcasted_iota(jnp.int32, (tq, nkeys), 0)
            sk = lax.broadcasted_iota(jnp.int32, (tq, nkeys), 1)
            valid = sk < tt
        def head(qh, c):
            z = lax.dot_general(qh, kb, NT_DIMS, preferred_element_type=F32)
            yield
            neg_z = -z
            log_fail = jnp.minimum(neg_z, 0.0) - jnp.log(1.0 + jnp.exp(jnp.minimum(z, neg_z)))
            log_hit = z + log_fail
            if diagonal:
                log_fail = jnp.where(valid, log_fail, 0.0)
            part = log_fail.astype(BF)
            after = jnp.dot(part, later, preferred_element_type=F32)
            rest = log_fail
            for _ in range(SB_CUMSUM_PASSES - 1):
                rest = rest - part.astype(F32)
                part = rest.astype(BF)
                after = after + jnp.dot(part, later, preferred_element_type=F32)
            yield
            w = jnp.exp(log_hit + after + c)
            if diagonal:
                w = jnp.where(valid, w, 0.0)
            return w.astype(BF), c + after[:, 0:1] + log_fail[:, 0:1]

        per_head = _run_in_lockstep([head(qh, c) for qh, c in zip(q_heads, carries)])
        wts = [w for w, _ in per_head]
        new_carries = [c for _, c in per_head]
        head_a_k = lax.broadcasted_iota(jnp.int32, (nkeys, LANES), 1) < B_HEAD_DIM
        zero_v = jnp.zeros_like(vb)
        v_stack = jnp.concatenate([jnp.where(head_a_k, vb, zero_v), jnp.where(head_a_k, zero_v, vb)], axis=0)
        acc = acc + jnp.dot(jnp.concatenate(wts, axis=1), v_stack, preferred_element_type=F32)
        return tuple(new_carries), acc

    zeros_c = jnp.zeros((tq, 1), F32)
    carries, acc = block(kd_ref[...], vd_ref[...], (zeros_c, zeros_c),
                         jnp.zeros((tq, LANES), F32), True)
    n_past = past_blocks(pl.program_id(2))

    def step(it, state):
        ca, cb, acc = state
        off = pl.multiple_of((n_past - 1 - it) * tk, tk)
        (ca, cb), acc = block(kp_ref[pl.ds(off, tk), :], vp_ref[pl.ds(off, tk), :], (ca, cb), acc, False)
        return ca, cb, acc

    _, _, acc = lax.fori_loop(0, n_past, step, (carries[0], carries[1], acc), unroll=unroll)
    o_ref[...] = acc.astype(o_ref.dtype)


def _stick_breaking(q, k_new, v_new, k_past, v_past, *, tq, tk, causal_past):
    bsz, t, _ = q.shape
    p_len = k_past.shape[1]
    if causal_past:
        past_blocks, unroll = (lambda qi: qi * (tq // tk)), None
    else:
        past_blocks, unroll = (lambda qi: p_len // tk), True
    est = (2 * 2 * _nbytes((p_len, LANES), k_past.dtype) + 2 * 4 * _nbytes((tq, LANES), F32)
           + 16 * _nbytes((tq, max(tq, tk)), F32))
    return pl.pallas_call(
        functools.partial(_sb_body, tq=tq, tk=tk, past_blocks=past_blocks, unroll=unroll),
        name="stick_breaking_past" if causal_past else "stick_breaking_cache",
        grid=(bsz, B_WIDTH // LANES, t // tq),
        in_specs=[pl.BlockSpec((None, tq, LANES), lambda b, p, i: (b, i, p)),
                  pl.BlockSpec((None, tq, LANES), lambda b, p, i: (b, i, p)),
                  pl.BlockSpec((None, tq, LANES), lambda b, p, i: (b, i, p)),
                  pl.BlockSpec((None, p_len, LANES), lambda b, p, i: (b, 0, p)),
                  pl.BlockSpec((None, p_len, LANES), lambda b, p, i: (b, 0, p))],
        out_specs=pl.BlockSpec((None, tq, LANES), lambda b, p, i: (b, i, p)),
        out_shape=jax.ShapeDtypeStruct((bsz, t, B_WIDTH), BF),
        compiler_params=_params(("parallel", "parallel", "arbitrary"), est),
    )(q, k_new, v_new, k_past, v_past)


def _shift_body(x_ref, last_ref, mu_ref, *rest, tiles_per_seq):
    outs, carry = rest[:6], rest[6]
    i = pl.program_id(1)

    @pl.when(i % tiles_per_seq == 0)
    def _():
        carry[...] = jnp.broadcast_to(last_ref[...], carry.shape)

    x = x_ref[...]
    rows = lax.broadcasted_iota(jnp.int32, x.shape, 0)
    x_prev = jnp.where(rows == 0, carry[SUBLANES - 1:SUBLANES, :], pltpu.roll(x, 1, 0))
    carry[...] = x[x.shape[0] - SUBLANES:, :]
    xx = x_prev - x
    for j, o_ref in enumerate(outs):
        o_ref[...] = (x + xx * mu_ref[j:j + 1, :]).astype(o_ref.dtype)


def _token_shift(x, x_last, mu):
    bsz, t, d = x.shape
    tm = _tile(t, 256)
    est = 2 * _nbytes((tm, d), F32) + 12 * _nbytes((tm, d), BF) + 4 * _nbytes((tm, d), F32)
    return pl.pallas_call(
        functools.partial(_shift_body, tiles_per_seq=t // tm),
        name="token_shift",
        grid=(bsz, t // tm),
        in_specs=[pl.BlockSpec((None, tm, d), lambda b, i: (b, i, 0)),
                  pl.BlockSpec((None, 1, d), lambda b, i: (b, 0, 0)),
                  pl.BlockSpec((6, d), lambda b, i: (0, 0))],
        out_specs=[pl.BlockSpec((None, tm, d), lambda b, i: (b, i, 0)) for _ in range(6)],
        out_shape=[jax.ShapeDtypeStruct((bsz, t, d), BF) for _ in range(6)],
        scratch_shapes=[pltpu.VMEM((SUBLANES, d), F32)],
        compiler_params=_params(("parallel", "arbitrary"), est),
    )(x, x_last[:, None, :].astype(F32), mu.astype(F32))


def _wkv_body(r_ref, k_ref, v_ref, lw_ref, a_ref, g_ref, kk_ref, ka_ref, rk_ref, gg_ref, gb_ref,
              s0_ref, y_ref, s_ref):
    c = WKV_CHUNK
    n = 2 * c

    @pl.when(pl.program_id(1) == 0)
    def _():
        s_ref[...] = s0_ref[...]

    head_a = lax.broadcasted_iota(jnp.int32, (c, LANES), 1) < C_HEAD_DIM
    ri = lax.broadcasted_iota(jnp.int32, (n, n), 0)
    ci = lax.broadcasted_iota(jnp.int32, (n, n), 1)
    same_head = (ri // c) == (ci // c)
    strict = same_head & (ci < ri)
    inclusive = same_head & (ci <= ri)
    diag16 = (ri // 16) == (ci // 16)
    diag32 = (ri // 32) == (ci // 32)
    eye = jnp.where(ri == ci, 1.0, 0.0).astype(F32)
    seg_ones = jnp.where(same_head, 1.0, 0.0).astype(BF)
    ti = lax.broadcasted_iota(jnp.int32, (c, c), 0)
    si = lax.broadcasted_iota(jnp.int32, (c, c), 1)
    prefix = jnp.where(si <= ti, 1.0, 0.0).astype(BF)

    def split2(x):
        hi = x.astype(BF)
        return hi, (x - hi.astype(F32)).astype(BF)

    def seg_sum(x):
        hi, lo = split2(x)
        return (jnp.dot(hi, seg_ones, preferred_element_type=F32)
                + jnp.dot(lo, seg_ones, preferred_element_type=F32))

    def cumsum_time(x):
        hi, rest = split2(x)
        rest_f = x - hi.astype(F32)
        mid = rest_f.astype(BF)
        lo = (rest_f - mid.astype(F32)).astype(BF)
        return (jnp.dot(prefix, hi, preferred_element_type=F32)
                + jnp.dot(prefix, mid, preferred_element_type=F32)
                + jnp.dot(prefix, lo, preferred_element_type=F32))

    def stack(x):
        zero = jnp.zeros_like(x)
        return jnp.concatenate([jnp.where(head_a, x, zero), jnp.where(head_a, zero, x)], axis=0)

    def pair(p, s):
        sl = pl.ds(pl.multiple_of(p * LANES, LANES), LANES)
        r = r_ref[:, sl]
        k = k_ref[:, sl]
        v = v_ref[:, sl]
        lw = lw_ref[:, sl]
        a = a_ref[:, sl]
        kkp = k * kk_ref[:, sl]
        norm = jnp.sqrt(seg_sum(kkp * kkp))
        yield
        kk = kkp / jnp.maximum(norm, 1e-12)
        k2 = k * (1.0 + (a - 1.0) * ka_ref[:, sl])
        bonus = seg_sum(r * k2 * rk_ref[:, sl]) * v
        yield
        cs = cumsum_time(lw)
        yield
        e_pos = jnp.exp(cs)
        e_neg = jnp.exp(-cs)
        a_t = -kk * e_pos * jnp.exp(-lw)
        b_t = kk * a * e_neg
        k_t = k2 * e_neg
        r_t = r * e_pos
        la, lr = stack(a_t), stack(r_t)
        rb, rk = stack(b_t), stack(k_t)
        vs = stack(v)
        q = lax.dot_general(jnp.concatenate([la, lr], axis=0).astype(BF),
                            jnp.concatenate([rb, rk], axis=0).astype(BF),
                            NT_DIMS, preferred_element_type=F32)
        yield
        p_ab = jnp.where(strict, q[:n, :n], 0.0)
        a_ak = jnp.where(strict, q[:n, n:], 0.0)
        a_rb = jnp.where(inclusive, q[n:, :n], 0.0)
        a_rk = jnp.where(inclusive, q[n:, n:], 0.0)
        akv = _bdot(a_ak, vs)
        yield
        pd = jnp.where(diag16, p_ab, 0.0)
        inv = eye + pd
        pw = pd
        for _ in range(3):
            pw = _bdot(pw, pw)
            yield
            inv = inv + _bdot(pw, inv)
            yield
        for off_diag in (jnp.where(diag32 & (~diag16), p_ab, 0.0), jnp.where(diag32, 0.0, p_ab)):
            right = _bdot(off_diag, inv)
            yield
            inv = inv + _bdot(inv, right)
            yield
        wu = _bdot(inv, jnp.concatenate([la, akv], axis=1))
        yield
        w_m, u_m = wu[:, :LANES], wu[:, LANES:]
        wr = lax.dot_general(jnp.concatenate([w_m, lr], axis=0).astype(BF), s.astype(BF),
                             NT_DIMS, preferred_element_type=F32)
        yield
        sa = wr[:n] + u_m
        sv = jnp.concatenate([sa, vs], axis=0)
        y_st = wr[n:] + _bdot(jnp.concatenate([a_rb, a_rk], axis=1), sv)
        g_end = e_pos[c - 1:c, :]
        rg = jnp.concatenate([rb, rk], axis=0) * g_end
        s_new = s * g_end + lax.dot_general(sv.astype(BF), rg.astype(BF), TN_DIMS,
                                            preferred_element_type=F32)
        yield
        y = y_st[:c] + y_st[c:]
        mean = seg_sum(y) * (1.0 / C_HEAD_DIM)
        yield
        d = y - mean
        var = seg_sum(d * d) * (1.0 / C_HEAD_DIM)
        yield
        yn = d * lax.rsqrt(var + C_GN_EPS) * gg_ref[:, sl] + gb_ref[:, sl]
        return s_new, ((yn + bonus) * g_ref[:, sl]).astype(y_ref.dtype)

    def group(it, carry):
        pairs = [it * WKV_PAIRS_PER_STEP + u for u in range(WKV_PAIRS_PER_STEP)]
        states = [s_ref[p] for p in pairs]
        results = _run_in_lockstep([pair(p, s) for p, s in zip(pairs, states)])
        for p, (s_new, y_out) in zip(pairs, results):
            s_ref[p] = s_new
            y_ref[:, pl.ds(pl.multiple_of(p * LANES, LANES), LANES)] = y_out
        return carry

    lax.fori_loop(0, N_PAIRS // WKV_PAIRS_PER_STEP, group, 0)


def _pair_state(s):
    bsz = s.shape[0]
    s = s.reshape(bsz, N_PAIRS, 2, C_HEAD_DIM, C_HEAD_DIM).astype(F32)
    z = jnp.zeros_like(s[:, :, 0])
    top = jnp.concatenate([s[:, :, 0], z], axis=-1)
    bot = jnp.concatenate([z, s[:, :, 1]], axis=-1)
    return jnp.concatenate([top, bot], axis=-2)


def _unpair_state(sp):
    bsz = sp.shape[0]
    a = sp[:, :, :C_HEAD_DIM, :C_HEAD_DIM]
    b = sp[:, :, C_HEAD_DIM:, C_HEAD_DIM:]
    return jnp.stack([a, b], axis=2).reshape(bsz, C_HEADS, C_HEAD_DIM, C_HEAD_DIM)


def _wkv(r, k, v, lw, a, g, s0, k_k, k_a, r_k, gn_g, gn_b):
    bsz, t, d = r.shape
    c = WKV_CHUNK
    seq = pl.BlockSpec((None, c, d), lambda b, i: (b, i, 0))
    vec = pl.BlockSpec((1, d), lambda b, i: (0, 0))
    st = pl.BlockSpec((None, N_PAIRS, HEAD_PAIR, HEAD_PAIR), lambda b, i: (b, 0, 0, 0))
    est = 2 * 6 * _nbytes((c, d), F32) + 4 * _nbytes((N_PAIRS, HEAD_PAIR, HEAD_PAIR), F32) + (8 << 20)
    y, s_new = pl.pallas_call(
        _wkv_body,
        name="wkv7_chunked",
        grid=(bsz, t // c),
        in_specs=[seq] * 6 + [vec] * 5 + [st],
        out_specs=[seq, st],
        out_shape=[jax.ShapeDtypeStruct((bsz, t, d), BF),
                   jax.ShapeDtypeStruct((bsz, N_PAIRS, HEAD_PAIR, HEAD_PAIR), F32)],
        compiler_params=_params(("parallel", "arbitrary"), est),
    )(r, k, v, lw, a, g, _row(k_k), _row(k_a), _row(r_k), _row(gn_g), _row(gn_b), _pair_state(s0))
    return y, _unpair_state(s_new)


def _ffn_in_body(x_ref, wg_ref, wu_ref, cw_ref, cb_ref, prev_ref, act_ref, st_ref, tail_ref, *,
                 seq_len, tm):
    x = x_ref[...]
    hg = jnp.dot(x, wg_ref[...], preferred_element_type=F32)
    hu = jnp.dot(x, wu_ref[...], preferred_element_type=F32)
    tn = hg.shape[1]
    rows = lax.broadcasted_iota(jnp.int32, hg.shape, 0)
    last2, last1 = SUBLANES - 2, SUBLANES - 1
    if seq_len >= tm:
        @pl.when(pl.program_id(1) % (seq_len // tm) == 0)
        def _():
            tail_ref[...] = prev_ref[0]
        tail = tail_ref[...]
        p0, p1 = tail[last2:last2 + 1, :], tail[last1:last1 + 1, :]
        pos = rows
        new_tail = hg[tm - SUBLANES:, :]
        tail_ref[...] = new_tail
        st_ref[0] = new_tail
    else:
        nseq = tm // seq_len
        prev = prev_ref[...]
        spread = lambda rw: jnp.broadcast_to(rw, (nseq, seq_len, tn)).reshape(tm, tn)
        p0, p1 = spread(prev[:, last2:last2 + 1, :]), spread(prev[:, last1:last1 + 1, :])
        pos = rows % seq_len
        st_ref[...] = hg.reshape(nseq, seq_len, tn)[:, seq_len - SUBLANES:, :]
    s1 = jnp.where(pos == 0, p1, pltpu.roll(hg, 1, 0))
    s2 = jnp.where(pos == 0, p0, jnp.where(pos == 1, p1, pltpu.roll(hg, 2, 0)))
    cw = cw_ref[...]
    hc = cb_ref[...] + s2 * cw[0:1, :] + s1 * cw[1:2, :] + hg * cw[2:3, :]
    act_ref[...] = (jax.nn.gelu(hc) * hu).astype(act_ref.dtype)


def _ffn_in(xb, w_in, conv_w, conv_b, conv_prev, seq_len):
    m, d = xb.shape
    nseq_total = m // seq_len
    tn = 512
    tm = _tile(m, 512)
    nj = D_FF // tn
    prev8 = jnp.pad(conv_prev.astype(F32), ((0, 0), (SUBLANES - (CONV_W - 1), 0), (0, 0)))
    if seq_len >= tm:
        per = seq_len // tm
        nb = 1
        st_map = lambda j, i: (i // per, 0, j)
    else:
        nb = tm // seq_len
        st_map = lambda j, i: (i, 0, j)
    st_spec = pl.BlockSpec((nb, SUBLANES, tn), st_map)
    est = (2 * _nbytes((tm, d), BF) + 4 * _nbytes((d, tn), BF) + 2 * _nbytes((tm, tn), BF)
           + 8 * _nbytes((tm, tn), F32))
    act, tails = pl.pallas_call(
        functools.partial(_ffn_in_body, seq_len=seq_len, tm=tm),
        name="ffn_in_conv",
        grid=(nj, m // tm),
        in_specs=[pl.BlockSpec((tm, d), lambda j, i: (i, 0)),
                  pl.BlockSpec((d, tn), lambda j, i: (0, j)),
                  pl.BlockSpec((d, tn), lambda j, i: (0, j + nj)),
                  pl.BlockSpec((CONV_W, tn), lambda j, i: (0, j)),
                  pl.BlockSpec((1, tn), lambda j, i: (0, j)),
                  st_spec],
        out_specs=[pl.BlockSpec((tm, tn), lambda j, i: (i, j)), st_spec],
        out_shape=[jax.ShapeDtypeStruct((m, D_FF), BF),
                   jax.ShapeDtypeStruct((nseq_total, SUBLANES, D_FF), F32)],
        scratch_shapes=[pltpu.VMEM((SUBLANES, tn), F32)],
        compiler_params=_params(("parallel", "arbitrary"), est),
    )(xb, w_in, w_in, conv_w.astype(F32), _row(conv_b), prev8)
    return act, tails[:, SUBLANES - (CONV_W - 1):, :]


def _ln_residual_epilogue(accs, tiles, rows):
    mix = accs[0]
    for extra in accs[1:]:
        mix = mix + extra
    y = _layernorm(DN_ALPHA * tiles[0] + mix, rows[0], rows[1])
    return y, y


def _identity_epilogue(accs, tiles, rows):
    return (accs[0],)


def _even_mixer(x, xb, seq_len, w_in, lnv_g, lnv_b, w_s, b_s, w_o, ln_g, ln_b, cache):
    m = x.shape[0]
    bsz = m // seq_len
    proj = lambda name, off, ep, dts, rows=(): _mm([(xb, w_in, off)], ep, dts, n=1024, tm=512,
                                                   tn=1024, rows=rows, name=name)
    (u,) = proj("even_u", 0, lambda a, t, r: (jax.nn.gelu(a[0]),), [BF])

    def va_epilogue(a, t, r):
        va = _layernorm(jax.nn.gelu(a[0]), r[0], r[1])
        return va, va
    va, vab = proj("even_va", A_WIDTH, va_epilogue, [F32, BF], rows=(_row(lnv_g), _row(lnv_b)))
    o = 2 * A_WIDTH
    (q,) = proj("even_q", o, lambda a, t, r: (a[0] * (B_HEAD_DIM ** -0.5),), [BF])
    (k,) = proj("even_k", o + B_WIDTH, _identity_epilogue, [F32])
    (v,) = proj("even_v", o + 2 * B_WIDTH, _identity_epilogue, [F32])

    blk = min(seq_len, A_BLOCK)
    a_out = _spatial_gate(u, vab, w_s[:, :blk, :blk], b_s[:, :blk], blk)

    q3 = q.reshape(bsz, seq_len, B_WIDTH)
    k3 = k.reshape(bsz, seq_len, B_WIDTH)
    v3 = v.reshape(bsz, seq_len, B_WIDTH)
    if cache is None:
        tq = _tile(seq_len, SB_BLOCK)
        b_out = _stick_breaking(q3, k3, v3, k3, v3, tq=tq, tk=tq, causal_past=True)
    else:
        k_cache, v_cache = cache
        past = k_cache.shape[1]
        b_out = _stick_breaking(q3, k3, v3, k_cache.reshape(bsz, past, B_WIDTH),
                                v_cache.reshape(bsz, past, B_WIDTH),
                                tq=seq_len, tk=_tile(past, SB_BLOCK), causal_past=False)
    b_out = b_out.reshape(m, B_WIDTH)

    x1, x1b = _mm([(a_out, w_o, 0), (b_out, w_o[A_WIDTH:], 0)], _ln_residual_epilogue, [F32, BF],
                  n=D_MODEL, tm=256, tn=D_MODEL, tiles=(x,), rows=(_row(ln_g), _row(ln_b)),
                  name="even_out_ln")
    return x1, x1b, k, v, va


def _rwkv_mixer(x, seq_len, x_last, s0, v_first, P, ln_g, ln_b):
    m = x.shape[0]
    bsz = m // seq_len
    mixes = _token_shift(x.reshape(bsz, seq_len, D_MODEL), x_last, P['mu'])
    xr, xw, xk, xv, xa, xg = [a.reshape(m, D_MODEL) for a in mixes]
    big = lambda xin, w: _mm([(xin, w, 0)], _identity_epilogue, [F32], n=D_MODEL, tm=512, tn=1024,
                             name="rwkv_proj")[0]
    r = big(xr, P['w_r'])
    k = big(xk, P['w_k'])

    def lora_in(xin, w, act):
        n = w.shape[1]
        return _mm([(xin, w, 0)], lambda a, t, r_: (act(a[0]),), [BF], n=n, tm=512, tn=n,
                   name="rwkv_lora_in")[0]

    def lora_out(h, w, bias, act):
        return _mm([(h, w, 0)], lambda a, t, r_: (act(r_[0] + a[0]),), [F32], n=D_MODEL, tm=512,
                   tn=1024, rows=(_row(bias),), name="rwkv_lora_out")[0]

    hw = lora_in(xw, P['w1'], jnp.tanh)
    lw = lora_out(hw, P['w2'], P['w0'],
                  lambda pre: -jnp.exp(-(jnp.maximum(-pre, 0.0) + jnp.log(1.0 + jnp.exp(-jnp.abs(pre)))) - 0.5))
    ha = lora_in(xa, P['a1'], lambda z: z)
    a = lora_out(ha, P['a2'], P['a0'], jax.nn.sigmoid)
    hg = lora_in(xg, P['g1'], jax.nn.sigmoid)
    g = _mm([(hg, P['g2'], 0)], _identity_epilogue, [F32], n=D_MODEL, tm=512, tn=1024, name="rwkv_gate")[0]
    if P['v1'] is None:
        v = big(xv, P['w_v'])
        v_first = v
    else:
        hv = lora_in(xv, P['v1'], lambda z: z)

        def v_epilogue(accs, tiles, rows):
            v_new = accs[0]
            return (v_new + (tiles[0] - v_new) * jax.nn.sigmoid(rows[0] + accs[1]),)
        v = _mm([(xv, P['w_v'], 0), (hv, P['v2'], 0)], v_epilogue, [F32], n=D_MODEL, tm=512, tn=1024,
                tiles=(v_first,), rows=(_row(P['v0']),), name="rwkv_v_residual")[0]
    to3 = lambda z: z.reshape(bsz, seq_len, D_MODEL)
    yg, s_new = _wkv(to3(r), to3(k), to3(v), to3(lw), to3(a), to3(g), s0,
                     P['k_k'], P['k_a'], P['r_k'], P['gn_g'], P['gn_b'])
    x1, x1b = _mm([(yg.reshape(m, D_MODEL), P['w_o'], 0)], _ln_residual_epilogue, [F32, BF],
                  n=D_MODEL, tm=256, tn=D_MODEL, tiles=(x,), rows=(_row(ln_g), _row(ln_b)),
                  name="rwkv_out_ln")
    return x1, x1b, s_new, v_first


def _ffn_ple(x1, x1b, seq_len, p, conv_prev, w_in, conv_w, conv_b, w_out, ln_g, ln_b, ple_gate, ple_proj):
    act, conv_rows = _ffn_in(x1b, w_in, conv_w, conv_b, conv_prev, seq_len)
    x2, x2b = _mm([(act, w_out, 0)], _ln_residual_epilogue, [F32, BF], n=D_MODEL, tm=512, tn=D_MODEL,
                  tk=D_FF // 4, tiles=(x1,), rows=(_row(ln_g), _row(ln_b)), name="ffn_out_ln")

    def ple_epilogue(accs, tiles, rows):
        y = tiles[0] + jax.nn.sigmoid(accs[0]) * accs[1]
        return y, y
    x3, x3b = _mm([(x2b, ple_gate, 0), (p, ple_proj, 0)], ple_epilogue, [F32, BF], n=D_MODEL, tm=512,
                  tn=1024, tiles=(x2,), name="ple")
    return x3, x3b, conv_rows


def _trunk(x, p, W, caches, wkv0, shift0, conv0):
    bsz, seq_len, _ = x.shape
    m = bsz * seq_len
    x = x.reshape(m, D_MODEL).astype(F32)
    xb = x.astype(BF)
    new_k, new_v, new_va, new_wkv, new_shift, new_conv = [], [], [], [], [], []
    v_first = None
    for i in range(DEPTH):
        j = i // 2
        if i % 2 == 0:
            cache = None if caches is None else (caches[0][j], caches[1][j])
            x1, x1b, k, v, va = _even_mixer(
                x, xb, seq_len, W['even_w_in'][j], W['even_lnv_g'][j], W['even_lnv_b'][j],
                W['even_w_s'][j], W['even_b_s'][j], W['even_w_o'][j], W['ln1_g'][i], W['ln1_b'][i], cache)
            new_k.append(k.reshape(bsz, seq_len, B_HEADS, B_HEAD_DIM))
            new_v.append(v.reshape(bsz, seq_len, B_HEADS, B_HEAD_DIM))
            new_va.append(va.reshape(bsz, seq_len, A_HEADS, A_DIM))
        else:
            P = dict(mu=W['c_mu'][j], w_r=W['c_w_r'][j], w_k=W['c_w_k'][j], w_v=W['c_w_v'][j],
                     w_o=W['c_w_o'][j], w0=W['c_w0'][j], w1=W['c_w1'][j], w2=W['c_w2'][j],
                     a0=W['c_a0'][j], a1=W['c_a1'][j], a2=W['c_a2'][j], g1=W['c_g1'][j], g2=W['c_g2'][j],
                     k_k=W['c_k_k'][j], k_a=W['c_k_a'][j], r_k=W['c_r_k'][j], gn_g=W['c_gn_g'][j],
                     gn_b=W['c_gn_b'][j],
                     v0=None if j == 0 else W['c_v0'][j - 1], v1=None if j == 0 else W['c_v1'][j - 1],
                     v2=None if j == 0 else W['c_v2'][j - 1])
            x1, x1b, s_new, v_first = _rwkv_mixer(x, seq_len, shift0[j], wkv0[j], v_first, P,
                                                  W['ln1_g'][i], W['ln1_b'][i])
            new_wkv.append(s_new)
            new_shift.append(x.reshape(bsz, seq_len, D_MODEL)[:, -1])
        x, xb, conv_rows = _ffn_ple(
            x1, x1b, seq_len, p[i].reshape(m, -1), conv0[i], W['ffn_w_in'][i], W['ffn_conv_w'][i],
            W['ffn_conv_b'][i], W['ffn_w_out'][i], W['ln2_g'][i], W['ln2_b'][i], W['ple_gate'][i],
            W['ple_proj'][i])
        new_conv.append(conv_rows)
    st = lambda l: jnp.stack(l) if l else None
    return (x.reshape(bsz, seq_len, D_MODEL), st(new_k), st(new_v), st(new_va), st(new_wkv),
            st(new_shift), st(new_conv))


_MATRICES = ('even_w_in', 'even_w_o', 'c_w_r', 'c_w_k', 'c_w_v', 'c_w_o', 'c_w1', 'c_w2', 'c_a1', 'c_a2',
             'c_v1', 'c_v2', 'c_g1', 'c_g2', 'ffn_w_in', 'ffn_w_out', 'ple_proj', 'ple_gate')


def kernel(x_prompt, x_sample, cache_sb_k, cache_sb_v, state_rwkv_wkv, state_rwkv_shift, state_ffn_conv, p_prompt, p_sample, even_w_in, even_lnv_g, even_lnv_b, even_w_s, even_b_s, even_w_o, c_mu, c_w_r, c_w_k, c_w_v, c_w_o, c_w0, c_w1, c_w2, c_a0, c_a1, c_a2, c_v0, c_v1, c_v2, c_g1, c_g2, c_k_k, c_k_a, c_r_k, c_gn_g, c_gn_b, ffn_w_in, ffn_conv_w, ffn_conv_b, ffn_w_out, ln1_g, ln1_b, ln2_g, ln2_b, ple_proj, ple_gate):
    W = dict(even_w_in=even_w_in, even_lnv_g=even_lnv_g, even_lnv_b=even_lnv_b, even_w_s=even_w_s,
             even_b_s=even_b_s, even_w_o=even_w_o, c_mu=c_mu, c_w_r=c_w_r, c_w_k=c_w_k,
             c_w_v=c_w_v, c_w_o=c_w_o, c_w0=c_w0, c_w1=c_w1, c_w2=c_w2, c_a0=c_a0, c_a1=c_a1,
             c_a2=c_a2, c_v0=c_v0, c_v1=c_v1, c_v2=c_v2, c_g1=c_g1, c_g2=c_g2, c_k_k=c_k_k,
             c_k_a=c_k_a, c_r_k=c_r_k, c_gn_g=c_gn_g, c_gn_b=c_gn_b, ffn_w_in=ffn_w_in,
             ffn_conv_w=ffn_conv_w, ffn_conv_b=ffn_conv_b, ffn_w_out=ffn_w_out, ln1_g=ln1_g,
             ln1_b=ln1_b, ln2_g=ln2_g, ln2_b=ln2_b, ple_proj=ple_proj, ple_gate=ple_gate)
    for name in _MATRICES:
        W[name] = W[name].astype(BF)
    bp = x_prompt.shape[0]
    n_odd, n_even = state_rwkv_wkv.shape[0], cache_sb_k.shape[0]
    wkv_zero = jnp.zeros((n_odd, bp, C_HEADS, C_HEAD_DIM, C_HEAD_DIM), F32)
    shift_zero = jnp.zeros((n_odd, bp, D_MODEL), F32)
    conv_zero = jnp.zeros((DEPTH, bp, CONV_W - 1, D_FF), F32)
    y_p, k_p, v_p, _, wkv_p, shift_p, conv_p = _trunk(
        x_prompt, p_prompt, W, None, wkv_zero, shift_zero, conv_zero)
    y_s, k_s, v_s, va_s, wkv_s, shift_s, conv_s = _trunk(
        x_sample, p_sample, W, (cache_sb_k, cache_sb_v), state_rwkv_wkv, state_rwkv_shift, state_ffn_conv)
    return (y_p, y_s, k_p, v_p, wkv_p, shift_p, conv_p, k_s, v_s, va_s, wkv_s, shift_s, conv_s)
```

```python
import functools

import jax
import jax.numpy as jnp
from jax import lax
from jax.experimental import pallas as pl
from jax.experimental.pallas import tpu as pltpu

F32 = jnp.float32
BF = jnp.bfloat16

D_MODEL = 2048
DEPTH = 4
CHUNK = 64
A_WIDTH = D_MODEL // 2
A_HEADS = 8
A_DIM = A_WIDTH // A_HEADS
A_BLOCK = 128
B_HEAD_DIM = 64
B_WIDTH = D_MODEL // 2
B_HEADS = B_WIDTH // B_HEAD_DIM
C_HEAD_DIM = 64
C_HEADS = D_MODEL // C_HEAD_DIM
C_GN_EPS = 64e-5
D_FF = 5632
CONV_W = 3
LN_EPS = 1e-5
DN_ALPHA = (2 * DEPTH) ** 0.25

MXU_WIDTH = 256
LANES = 128
SUBLANES = 8
HEAD_PAIR = 2 * C_HEAD_DIM
N_PAIRS = D_MODEL // HEAD_PAIR
WKV_CHUNK = 64
WKV_PAIRS_PER_STEP = 16
SB_BLOCK = 256
SB_CUMSUM_PASSES = 1
VMEM_CAP = 56 * 1024 * 1024

NT_DIMS = (((1,), (1,)), ((), ()))
TN_DIMS = (((0,), (0,)), ((), ()))


def _tile(m, pref):
    return pref if m % pref == 0 else m


def _params(sem, est_bytes):
    limit = int(min(max(est_bytes * 5 // 4 + (4 << 20), 16 << 20), VMEM_CAP))
    return pltpu.CompilerParams(dimension_semantics=sem, vmem_limit_bytes=limit)


def _nbytes(shape, dtype):
    n = 1
    for s in shape:
        n *= s
    return n * jnp.dtype(dtype).itemsize


def _layernorm(x, g, b, eps=LN_EPS):
    mu = jnp.mean(x, axis=-1, keepdims=True)
    d = x - mu
    var = jnp.mean(d * d, axis=-1, keepdims=True)
    return d * lax.rsqrt(var + eps) * g + b


def _bdot(a, b):
    return jnp.dot(a.astype(BF), b.astype(BF), preferred_element_type=F32)


def _run_in_lockstep(generators):
    results = [None] * len(generators)
    live = list(range(len(generators)))
    while live:
        for idx in list(live):
            try:
                next(generators[idx])
            except StopIteration as done:
                results[idx] = done.value
                live.remove(idx)
    return results


def _mm_body(*refs, nprod, ntile, nrow, nout, nk, epilogue, col_chunk):
    xs = refs[:nprod]
    ws = refs[nprod:2 * nprod]
    ts = refs[2 * nprod:2 * nprod + ntile]
    rs = refs[2 * nprod + ntile:2 * nprod + ntile + nrow]
    outs = refs[2 * nprod + ntile + nrow:2 * nprod + ntile + nrow + nout]
    accs = refs[2 * nprod + ntile + nrow + nout:]

    def finish(vals, cols=slice(None)):
        res = epilogue(vals, [t[:, cols] for t in ts], [r[:, cols] for r in rs])
        for o_ref, o in zip(outs, res):
            o_ref[:, cols] = o.astype(o_ref.dtype)

    tn = outs[0].shape[1]
    if nk == 1 and col_chunk is not None and tn > col_chunk and tn % col_chunk == 0:
        lhs = [x[...].astype(BF) for x in xs]
        pending = None
        for ch in range(tn // col_chunk):
            cols = slice(ch * col_chunk, (ch + 1) * col_chunk)
            cur = [jnp.dot(a, w[:, cols], preferred_element_type=F32) for a, w in zip(lhs, ws)]
            if pending is not None:
                finish(*pending)
            pending = (cur, cols)
        finish(*pending)
        return

    prods = [jnp.dot(x[...].astype(BF), w[...], preferred_element_type=F32) for x, w in zip(xs, ws)]
    if nk == 1:
        finish(prods)
        return
    k = pl.program_id(2)

    @pl.when(k == 0)
    def _():
        for a, p in zip(accs, prods):
            a[...] = p

    @pl.when(k > 0)
    def _():
        for a, p in zip(accs, prods):
            a[...] += p

    @pl.when(k == nk - 1)
    def _():
        finish([a[...] for a in accs])


def _mm(prods, epilogue, out_dtypes, *, n, tm, tn, tk=None, tiles=(), rows=(), name="mm",
        col_chunk=MXU_WIDTH):
    m = prods[0][0].shape[0]
    tm = _tile(m, tm)
    ks = [x.shape[1] for x, _, _ in prods]
    nk = 1 if tk is None else ks[0] // tk
    if nk > 1:
        assert all(k == ks[0] for k in ks) and ks[0] % tk == 0
    assert n % tn == 0 and m % tm == 0
    grid = (n // tn, m // tm, nk)
    in_specs, est = [], 0
    for k_dim in ks:
        bk = k_dim if nk == 1 else tk
        in_specs.append(pl.BlockSpec((tm, bk), lambda j, i, k: (i, k)))
    for (x, w, off), k_dim in zip(prods, ks):
        assert off % tn == 0
        bk = k_dim if nk == 1 else tk
        in_specs.append(pl.BlockSpec((bk, tn), lambda j, i, k, ob=off // tn: (k, j + ob)))
        est += 2 * (_nbytes((tm, bk), x.dtype) + _nbytes((bk, tn), w.dtype))
    for t in tiles:
        in_specs.append(pl.BlockSpec((tm, tn), lambda j, i, k: (i, j)))
        est += 2 * _nbytes((tm, tn), t.dtype)
    for r in rows:
        in_specs.append(pl.BlockSpec((1, tn), lambda j, i, k: (0, j)))
    out_specs = [pl.BlockSpec((tm, tn), lambda j, i, k: (i, j)) for _ in out_dtypes]
    out_shape = [jax.ShapeDtypeStruct((m, n), dt) for dt in out_dtypes]
    est += sum(2 * _nbytes((tm, tn), dt) for dt in out_dtypes)
    est += (len(prods) + 2) * _nbytes((tm, tn), F32)
    scratch = [pltpu.VMEM((tm, tn), F32) for _ in prods] if nk > 1 else []
    body = functools.partial(_mm_body, nprod=len(prods), ntile=len(tiles), nrow=len(rows),
                             nout=len(out_dtypes), nk=nk, epilogue=epilogue, col_chunk=col_chunk)
    return pl.pallas_call(
        body, name=name, grid=grid, in_specs=in_specs, out_specs=out_specs, out_shape=out_shape,
        scratch_shapes=scratch,
        compiler_params=_params(("parallel", "parallel", "arbitrary"), est),
    )(*[p[0] for p in prods], *[p[1] for p in prods], *tiles, *rows)


def _row(v):
    return v.reshape(1, -1).astype(F32)


def _gate_body(u_ref, va_ref, ws_ref, bs_ref, o_ref, *, blk, nblk):
    ti = lax.broadcasted_iota(jnp.int32, (blk, blk), 0) // CHUNK
    si = lax.broadcasted_iota(jnp.int32, (blk, blk), 1) // CHUNK
    allowed = ti >= si
    for h in range(A_HEADS):
        w = jnp.where(allowed, ws_ref[h], 0.0).astype(BF)
        bias = bs_ref[h]
        for nb in range(nblk):
            rs = slice(nb * blk, (nb + 1) * blk)
            cs = slice(h * A_DIM, (h + 1) * A_DIM)
            mixed = jnp.dot(w, va_ref[rs, cs], preferred_element_type=F32) + bias
            o_ref[rs, cs] = (u_ref[rs, cs].astype(F32) * mixed).astype(o_ref.dtype)


def _spatial_gate(u, va, w_s, b_s, blk):
    m = u.shape[0]
    tm = _tile(m, 512)
    nblk = tm // blk
    est = 2 * 3 * _nbytes((tm, A_WIDTH), BF) + 2 * _nbytes((A_HEADS, blk, blk), F32)
    return pl.pallas_call(
        functools.partial(_gate_body, blk=blk, nblk=nblk),
        name="spatial_gate",
        grid=(m // tm,),
        in_specs=[pl.BlockSpec((tm, A_WIDTH), lambda i: (i, 0)),
                  pl.BlockSpec((tm, A_WIDTH), lambda i: (i, 0)),
                  pl.BlockSpec((A_HEADS, blk, blk), lambda i: (0, 0, 0)),
                  pl.BlockSpec((A_HEADS, blk, 1), lambda i: (0, 0, 0))],
        out_specs=pl.BlockSpec((tm, A_WIDTH), lambda i: (i, 0)),
        out_shape=jax.ShapeDtypeStruct((m, A_WIDTH), BF),
        compiler_params=_params(("parallel",), est),
    )(u, va, w_s.astype(F32), b_s.astype(F32)[:, :, None])


def _sb_body(q_ref, kd_ref, vd_ref, kp_ref, vp_ref, o_ref, *, sub, nq, tk, blocks_per_iter, past_iters,
             unroll):
    head_a_q = lax.broadcasted_iota(jnp.int32, (sub, LANES), 1) < B_HEAD_DIM
    q_subs = []
    for i in range(nq):
        q = q_ref[i * sub:(i + 1) * sub, :]
        zero_q = jnp.zeros_like(q)
        q_subs.append((jnp.where(head_a_q, q, zero_q), jnp.where(head_a_q, zero_q, q)))

    def chain(qh, kb, valid, later):
        z = lax.dot_general(qh, kb, NT_DIMS, preferred_element_type=F32)
        yield
        neg_z = -z
        log_fail = jnp.minimum(neg_z, 0.0) - jnp.log(1.0 + jnp.exp(jnp.minimum(z, neg_z)))
        log_hit = z + log_fail
        if valid is not None:
            log_fail = jnp.where(valid, log_fail, 0.0)
        part = log_fail.astype(BF)
        after = jnp.dot(part, later, preferred_element_type=F32)
        rest = log_fail
        for _ in range(SB_CUMSUM_PASSES - 1):
            rest = rest - part.astype(F32)
            part = rest.astype(BF)
            after = after + jnp.dot(part, later, preferred_element_type=F32)
        yield
        return log_hit + after, after[:, 0:1] + log_fail[:, 0:1]

    def attend(k_blocks, v_blocks, masks, carries, accs):
        kbs = [k.astype(BF) for k in k_blocks]
        laters, v_stacks = {}, []
        for kb, v_blk in zip(kbs, v_blocks):
            nkeys = kb.shape[0]
            if nkeys not in laters:
                jj = lax.broadcasted_iota(jnp.int32, (nkeys, nkeys), 0)
                ss = lax.broadcasted_iota(jnp.int32, (nkeys, nkeys), 1)
                laters[nkeys] = jnp.where(jj > ss, 1.0, 0.0).astype(BF)
            vb = v_blk.astype(BF)
            head_a_k = lax.broadcasted_iota(jnp.int32, (nkeys, LANES), 1) < B_HEAD_DIM
            zero_v = jnp.zeros_like(vb)
            v_stacks.append(jnp.concatenate([jnp.where(head_a_k, vb, zero_v), jnp.where(head_a_k, zero_v, vb)],
                                            axis=0))
        keys, gens = [], []
        for i in range(nq):
            for j, kb in enumerate(kbs):
                if isinstance(masks[i][j], str):
                    continue
                for h in range(2):
                    keys.append((i, j, h))
                    gens.append(chain(q_subs[i][h], kb, masks[i][j], laters[kb.shape[0]]))
        scores = dict(zip(keys, _run_in_lockstep(gens)))
        new_carries, new_accs = [], []
        for i in range(nq):
            cs = list(carries[i])
            wts, vals = [], []
            for j in range(len(kbs)):
                if isinstance(masks[i][j], str):
                    continue
                for h in range(2):
                    logit, total = scores[(i, j, h)]
                    w = jnp.exp(logit + cs[h])
                    if masks[i][j] is not None:
                        w = jnp.where(masks[i][j], w, 0.0)
                    cs[h] = cs[h] + total
                    wts.append(w.astype(BF))
                vals.append(v_stacks[j])
            new_carries.append(cs)
            new_accs.append(accs[i] + jnp.dot(jnp.concatenate(wts, axis=1), jnp.concatenate(vals, axis=0),
                                              preferred_element_type=F32))
        return new_carries, new_accs

    k_diag = [kd_ref[j * sub:(j + 1) * sub, :] for j in reversed(range(nq))]
    v_diag = [vd_ref[j * sub:(j + 1) * sub, :] for j in reversed(range(nq))]
    tt = lax.broadcasted_iota(jnp.int32, (sub, sub), 0)
    sk = lax.broadcasted_iota(jnp.int32, (sub, sub), 1)
    causal = sk < tt
    masks = [[causal if j == i else (None if j < i else "skip") for j in reversed(range(nq))]
             for i in range(nq)]
    zeros_c = jnp.zeros((sub, 1), F32)
    carries, accs = attend(k_diag, v_diag, masks, [[zeros_c, zeros_c] for _ in range(nq)],
                           [jnp.zeros((sub, LANES), F32) for _ in range(nq)])
    n_iters = past_iters(pl.program_id(2))
    open_masks = [[None] * blocks_per_iter for _ in range(nq)]

    def step(it, state):
        carries = [list(state[2 * i:2 * i + 2]) for i in range(nq)]
        accs = list(state[2 * nq:])
        newest = n_iters - 1 - it
        offs = [pl.multiple_of((newest * blocks_per_iter + j) * tk, tk)
                for j in reversed(range(blocks_per_iter))]
        carries, accs = attend([kp_ref[pl.ds(o, tk), :] for o in offs], [vp_ref[pl.ds(o, tk), :] for o in offs],
                               open_masks, carries, accs)
        return tuple(c for pair in carries for c in pair) + tuple(accs)

    state = lax.fori_loop(0, n_iters, step, tuple(c for pair in carries for c in pair) + tuple(accs),
                          unroll=unroll)
    for i in range(nq):
        o_ref[i * sub:(i + 1) * sub, :] = state[2 * nq + i].astype(o_ref.dtype)


def _stick_breaking(q, k_new, v_new, k_past, v_past, *, tq, tk, causal_past):
    bsz, t, _ = q.shape
    p_len = k_past.shape[1]
    sub = min(tq, SB_BLOCK)
    nq = tq // sub
    if causal_past:
        blocks_per_iter, unroll = 1, None
        past_iters = lambda qi: qi * (tq // tk)
    else:
        n_blocks = p_len // tk
        blocks_per_iter, unroll = (2 if n_blocks % 2 == 0 else 1), True
        past_iters = lambda qi: n_blocks // blocks_per_iter
    est = (2 * 2 * _nbytes((p_len, LANES), k_past.dtype) + 2 * 4 * _nbytes((tq, LANES), F32)
           + 12 * 2 * nq * blocks_per_iter * _nbytes((sub, max(sub, tk)), F32))
    return pl.pallas_call(
        functools.partial(_sb_body, sub=sub, nq=nq, tk=tk, blocks_per_iter=blocks_per_iter,
                          past_iters=past_iters, unroll=unroll),
        name="stick_breaking_past" if causal_past else "stick_breaking_cache",
        grid=(bsz, B_WIDTH // LANES, t // tq),
        in_specs=[pl.BlockSpec((None, tq, LANES), lambda b, p, i: (b, i, p)),
                  pl.BlockSpec((None, tq, LANES), lambda b, p, i: (b, i, p)),
                  pl.BlockSpec((None, tq, LANES), lambda b, p, i: (b, i, p)),
                  pl.BlockSpec((None, p_len, LANES), lambda b, p, i: (b, 0, p)),
                  pl.BlockSpec((None, p_len, LANES), lambda b, p, i: (b, 0, p))],
        out_specs=pl.BlockSpec((None, tq, LANES), lambda b, p, i: (b, i, p)),
        out_shape=jax.ShapeDtypeStruct((bsz, t, B_WIDTH), BF),
        compiler_params=_params(("parallel", "parallel", "arbitrary"), est),
    )(q, k_new, v_new, k_past, v_past)


def _shift_body(x_ref, last_ref, mu_ref, *rest, tiles_per_seq):
    outs, carry = rest[:6], rest[6]
    i = pl.program_id(1)

    @pl.when(i % tiles_per_seq == 0)
    def _():
        carry[...] = jnp.broadcast_to(last_ref[...], carry.shape)

    x = x_ref[...]
    rows = lax.broadcasted_iota(jnp.int32, x.shape, 0)
    x_prev = jnp.where(rows == 0, carry[SUBLANES - 1:SUBLANES, :], pltpu.roll(x, 1, 0))
    carry[...] = x[x.shape[0] - SUBLANES:, :]
    xx = x_prev - x
    for j, o_ref in enumerate(outs):
        o_ref[...] = (x + xx * mu_ref[j:j + 1, :]).astype(o_ref.dtype)


def _token_shift(x, x_last, mu):
    bsz, t, d = x.shape
    tm = _tile(t, 256)
    est = 2 * _nbytes((tm, d), F32) + 12 * _nbytes((tm, d), BF) + 4 * _nbytes((tm, d), F32)
    return pl.pallas_call(
        functools.partial(_shift_body, tiles_per_seq=t // tm),
        name="token_shift",
        grid=(bsz, t // tm),
        in_specs=[pl.BlockSpec((None, tm, d), lambda b, i: (b, i, 0)),
                  pl.BlockSpec((None, 1, d), lambda b, i: (b, 0, 0)),
                  pl.BlockSpec((6, d), lambda b, i: (0, 0))],
        out_specs=[pl.BlockSpec((None, tm, d), lambda b, i: (b, i, 0)) for _ in range(6)],
        out_shape=[jax.ShapeDtypeStruct((bsz, t, d), BF) for _ in range(6)],
        scratch_shapes=[pltpu.VMEM((SUBLANES, d), F32)],
        compiler_params=_params(("parallel", "arbitrary"), est),
    )(x, x_last[:, None, :].astype(F32), mu.astype(F32))


def _wkv_body(r_ref, k_ref, v_ref, lw_ref, a_ref, g_ref, kk_ref, ka_ref, rk_ref, gg_ref, gb_ref,
              s0_ref, y_ref, s_ref):
    c = WKV_CHUNK
    n = 2 * c

    @pl.when(pl.program_id(1) == 0)
    def _():
        s_ref[...] = s0_ref[...]

    head_a = lax.broadcasted_iota(jnp.int32, (c, LANES), 1) < C_HEAD_DIM
    ri = lax.broadcasted_iota(jnp.int32, (n, n), 0)
    ci = lax.broadcasted_iota(jnp.int32, (n, n), 1)
    same_head = (ri // c) == (ci // c)
    strict = same_head & (ci < ri)
    inclusive = same_head & (ci <= ri)
    diag16 = (ri // 16) == (ci // 16)
    diag32 = (ri // 32) == (ci // 32)
    eye = jnp.where(ri == ci, 1.0, 0.0).astype(F32)
    seg_ones = jnp.where(same_head, 1.0, 0.0).astype(BF)
    ti = lax.broadcasted_iota(jnp.int32, (c, c), 0)
    si = lax.broadcasted_iota(jnp.int32, (c, c), 1)
    prefix = jnp.where(si <= ti, 1.0, 0.0).astype(BF)

    def split2(x):
        hi = x.astype(BF)
        return hi, (x - hi.astype(F32)).astype(BF)

    def seg_sum(x):
        hi, lo = split2(x)
        return (jnp.dot(hi, seg_ones, preferred_element_type=F32)
                + jnp.dot(lo, seg_ones, preferred_element_type=F32))

    def cumsum_time(x):
        hi, rest = split2(x)
        rest_f = x - hi.astype(F32)
        mid = rest_f.astype(BF)
        lo = (rest_f - mid.astype(F32)).astype(BF)
        return (jnp.dot(prefix, hi, preferred_element_type=F32)
                + jnp.dot(prefix, mid, preferred_element_type=F32)
                + jnp.dot(prefix, lo, preferred_element_type=F32))

    def stack(x):
        zero = jnp.zeros_like(x)
        return jnp.concatenate([jnp.where(head_a, x, zero), jnp.where(head_a, zero, x)], axis=0)

    def pair(p, s):
        sl = pl.ds(pl.multiple_of(p * LANES, LANES), LANES)
        r = r_ref[:, sl]
        k = k_ref[:, sl]
        v = v_ref[:, sl]
        lw = lw_ref[:, sl]
        a = a_ref[:, sl]
        kkp = k * kk_ref[:, sl]
        norm = jnp.sqrt(seg_sum(kkp * kkp))
        yield
        kk = kkp / jnp.maximum(norm, 1e-12)
        k2 = k * (1.0 + (a - 1.0) * ka_ref[:, sl])
        bonus = seg_sum(r * k2 * rk_ref[:, sl]) * v
        yield
        cs = cumsum_time(lw)
        yield
        e_pos = jnp.exp(cs)
        e_neg = jnp.exp(-cs)
        a_t = -kk * e_pos * jnp.exp(-lw)
        b_t = kk * a * e_neg
        k_t = k2 * e_neg
        r_t = r * e_pos
        la, lr = stack(a_t), stack(r_t)
        rb, rk = stack(b_t), stack(k_t)
        vs = stack(v)
        q = lax.dot_general(jnp.concatenate([la, lr], axis=0).astype(BF),
                            jnp.concatenate([rb, rk], axis=0).astype(BF),
                            NT_DIMS, preferred_element_type=F32)
        yield
        p_ab = jnp.where(strict, q[:n, :n], 0.0)
        a_ak = jnp.where(strict, q[:n, n:], 0.0)
        a_rb = jnp.where(inclusive, q[n:, :n], 0.0)
        a_rk = jnp.where(inclusive, q[n:, n:], 0.0)
        akv = _bdot(a_ak, vs)
        yield
        pd = jnp.where(diag16, p_ab, 0.0)
        inv = eye + pd
        pw = pd
        for _ in range(3):
            pw = _bdot(pw, pw)
            yield
            inv = inv + _bdot(pw, inv)
            yield
        for off_diag in (jnp.where(diag32 & (~diag16), p_ab, 0.0), jnp.where(diag32, 0.0, p_ab)):
            right = _bdot(off_diag, inv)
            yield
            inv = inv + _bdot(inv, right)
            yield
        wu = _bdot(inv, jnp.concatenate([la, akv], axis=1))
        yield
        w_m, u_m = wu[:, :LANES], wu[:, LANES:]
        wr = lax.dot_general(jnp.concatenate([w_m, lr], axis=0).astype(BF), s.astype(BF),
                             NT_DIMS, preferred_element_type=F32)
        yield
        sa = wr[:n] + u_m
        sv = jnp.concatenate([sa, vs], axis=0)
        y_st = wr[n:] + _bdot(jnp.concatenate([a_rb, a_rk], axis=1), sv)
        g_end = e_pos[c - 1:c, :]
        rg = jnp.concatenate([rb, rk], axis=0) * g_end
        s_new = s * g_end + lax.dot_general(sv.astype(BF), rg.astype(BF), TN_DIMS,
                                            preferred_element_type=F32)
        yield
        y = y_st[:c] + y_st[c:]
        mean = seg_sum(y) * (1.0 / C_HEAD_DIM)
        yield
        d = y - mean
        var = seg_sum(d * d) * (1.0 / C_HEAD_DIM)
        yield
        yn = d * lax.rsqrt(var + C_GN_EPS) * gg_ref[:, sl] + gb_ref[:, sl]
        return s_new, ((yn + bonus) * g_ref[:, sl]).astype(y_ref.dtype)

    def group(it, carry):
        pairs = [it * WKV_PAIRS_PER_STEP + u for u in range(WKV_PAIRS_PER_STEP)]
        states = [s_ref[p] for p in pairs]
        results = _run_in_lockstep([pair(p, s) for p, s in zip(pairs, states)])
        for p, (s_new, y_out) in zip(pairs, results):
            s_ref[p] = s_new
            y_ref[:, pl.ds(pl.multiple_of(p * LANES, LANES), LANES)] = y_out
        return carry

    lax.fori_loop(0, N_PAIRS // WKV_PAIRS_PER_STEP, group, 0)


def _pair_state(s):
    bsz = s.shape[0]
    s = s.reshape(bsz, N_PAIRS, 2, C_HEAD_DIM, C_HEAD_DIM).astype(F32)
    z = jnp.zeros_like(s[:, :, 0])
    top = jnp.concatenate([s[:, :, 0], z], axis=-1)
    bot = jnp.concatenate([z, s[:, :, 1]], axis=-1)
    return jnp.concatenate([top, bot], axis=-2)


def _unpair_state(sp):
    bsz = sp.shape[0]
    a = sp[:, :, :C_HEAD_DIM, :C_HEAD_DIM]
    b = sp[:, :, C_HEAD_DIM:, C_HEAD_DIM:]
    return jnp.stack([a, b], axis=2).reshape(bsz, C_HEADS, C_HEAD_DIM, C_HEAD_DIM)


def _wkv(r, k, v, lw, a, g, s0, k_k, k_a, r_k, gn_g, gn_b):
    bsz, t, d = r.shape
    c = WKV_CHUNK
    seq = pl.BlockSpec((None, c, d), lambda b, i: (b, i, 0))
    vec = pl.BlockSpec((1, d), lambda b, i: (0, 0))
    st = pl.BlockSpec((None, N_PAIRS, HEAD_PAIR, HEAD_PAIR), lambda b, i: (b, 0, 0, 0))
    est = 2 * 6 * _nbytes((c, d), F32) + 4 * _nbytes((N_PAIRS, HEAD_PAIR, HEAD_PAIR), F32) + (8 << 20)
    y, s_new = pl.pallas_call(
        _wkv_body,
        name="wkv7_chunked",
        grid=(bsz, t // c),
        in_specs=[seq] * 6 + [vec] * 5 + [st],
        out_specs=[seq, st],
        out_shape=[jax.ShapeDtypeStruct((bsz, t, d), BF),
                   jax.ShapeDtypeStruct((bsz, N_PAIRS, HEAD_PAIR, HEAD_PAIR), F32)],
        compiler_params=_params(("parallel", "arbitrary"), est),
    )(r, k, v, lw, a, g, _row(k_k), _row(k_a), _row(r_k), _row(gn_g), _row(gn_b), _pair_state(s0))
    return y, _unpair_state(s_new)


def _ffn_in_body(x_ref, wg_ref, wu_ref, cw_ref, cb_ref, prev_ref, act_ref, st_ref, tail_ref, *,
                 seq_len, tm):
    x = x_ref[...]
    tn = act_ref.shape[1]
    cn = min(tn, MXU_WIDTH)
    last2, last1 = SUBLANES - 2, SUBLANES - 1
    carried = seq_len >= tm
    if carried:
        @pl.when(pl.program_id(1) % (seq_len // tm) == 0)
        def _():
            tail_ref[...] = prev_ref[0]

    def conv_gate(hg, hu, cols):
        rows = lax.broadcasted_iota(jnp.int32, hg.shape, 0)
        if carried:
            tail = tail_ref[:, cols]
            p0, p1 = tail[last2:last2 + 1, :], tail[last1:last1 + 1, :]
            pos = rows
            new_tail = hg[tm - SUBLANES:, :]
            tail_ref[:, cols] = new_tail
            st_ref[0, :, cols] = new_tail
        else:
            nseq = tm // seq_len
            prev = prev_ref[:, :, cols]
            spread = lambda rw: jnp.broadcast_to(rw, (nseq, seq_len, cn)).reshape(tm, cn)
            p0, p1 = spread(prev[:, last2:last2 + 1, :]), spread(prev[:, last1:last1 + 1, :])
            pos = rows % seq_len
            st_ref[:, :, cols] = hg.reshape(nseq, seq_len, cn)[:, seq_len - SUBLANES:, :]
        s1 = jnp.where(pos == 0, p1, pltpu.roll(hg, 1, 0))
        s2 = jnp.where(pos == 0, p0, jnp.where(pos == 1, p1, pltpu.roll(hg, 2, 0)))
        cw = cw_ref[:, cols]
        hc = cb_ref[:, cols] + s2 * cw[0:1, :] + s1 * cw[1:2, :] + hg * cw[2:3, :]
        act_ref[:, cols] = (jax.nn.gelu(hc) * hu).astype(act_ref.dtype)

    pending = None
    for ch in range(tn // cn):
        cols = slice(ch * cn, (ch + 1) * cn)
        hg = jnp.dot(x, wg_ref[:, cols], preferred_element_type=F32)
        hu = jnp.dot(x, wu_ref[:, cols], preferred_element_type=F32)
        if pending is not None:
            conv_gate(*pending)
        pending = (hg, hu, cols)
    conv_gate(*pending)


def _ffn_in(xb, w_in, conv_w, conv_b, conv_prev, seq_len):
    m, d = xb.shape
    nseq_total = m // seq_len
    tn = 512
    tm = _tile(m, 512)
    nj = D_FF // tn
    prev8 = jnp.pad(conv_prev.astype(F32), ((0, 0), (SUBLANES - (CONV_W - 1), 0), (0, 0)))
    if seq_len >= tm:
        per = seq_len // tm
        nb = 1
        st_map = lambda j, i: (i // per, 0, j)
    else:
        nb = tm // seq_len
        st_map = lambda j, i: (i, 0, j)
    st_spec = pl.BlockSpec((nb, SUBLANES, tn), st_map)
    est = (2 * _nbytes((tm, d), BF) + 4 * _nbytes((d, tn), BF) + 2 * _nbytes((tm, tn), BF)
           + 8 * _nbytes((tm, tn), F32))
    act, tails = pl.pallas_call(
        functools.partial(_ffn_in_body, seq_len=seq_len, tm=tm),
        name="ffn_in_conv",
        grid=(nj, m // tm),
        in_specs=[pl.BlockSpec((tm, d), lambda j, i: (i, 0)),
                  pl.BlockSpec((d, tn), lambda j, i: (0, j)),
                  pl.BlockSpec((d, tn), lambda j, i: (0, j + nj)),
                  pl.BlockSpec((CONV_W, tn), lambda j, i: (0, j)),
                  pl.BlockSpec((1, tn), lambda j, i: (0, j)),
                  st_spec],
        out_specs=[pl.BlockSpec((tm, tn), lambda j, i: (i, j)), st_spec],
        out_shape=[jax.ShapeDtypeStruct((m, D_FF), BF),
                   jax.ShapeDtypeStruct((nseq_total, SUBLANES, D_FF), F32)],
        scratch_shapes=[pltpu.VMEM((SUBLANES, tn), F32)],
        compiler_params=_params(("parallel", "arbitrary"), est),
    )(xb, w_in, w_in, conv_w.astype(F32), _row(conv_b), prev8)
    return act, tails[:, SUBLANES - (CONV_W - 1):, :]


def _ln_residual_epilogue(accs, tiles, rows):
    mix = accs[0]
    for extra in accs[1:]:
        mix = mix + extra
    y = _layernorm(DN_ALPHA * tiles[0] + mix, rows[0], rows[1])
    return y, y


def _identity_epilogue(accs, tiles, rows):
    return (accs[0],)


def _even_mixer(x, xb, seq_len, w_in, lnv_g, lnv_b, w_s, b_s, w_o, ln_g, ln_b, cache):
    m = x.shape[0]
    bsz = m // seq_len
    proj = lambda name, off, ep, dts, rows=(), col_chunk=MXU_WIDTH: _mm(
        [(xb, w_in, off)], ep, dts, n=1024, tm=512, tn=1024, rows=rows, name=name, col_chunk=col_chunk)
    (u,) = proj("even_u", 0, lambda a, t, r: (jax.nn.gelu(a[0]),), [BF])

    def va_epilogue(a, t, r):
        va = _layernorm(jax.nn.gelu(a[0]), r[0], r[1])
        return va, va
    va, vab = proj("even_va", A_WIDTH, va_epilogue, [F32, BF], rows=(_row(lnv_g), _row(lnv_b)),
                   col_chunk=None)
    o = 2 * A_WIDTH
    (q,) = proj("even_q", o, lambda a, t, r: (a[0] * (B_HEAD_DIM ** -0.5),), [BF])
    (k,) = proj("even_k", o + B_WIDTH, _identity_epilogue, [F32])
    (v,) = proj("even_v", o + 2 * B_WIDTH, _identity_epilogue, [F32])

    blk = min(seq_len, A_BLOCK)
    a_out = _spatial_gate(u, vab, w_s[:, :blk, :blk], b_s[:, :blk], blk)

    q3 = q.reshape(bsz, seq_len, B_WIDTH)
    k3 = k.reshape(bsz, seq_len, B_WIDTH)
    v3 = v.reshape(bsz, seq_len, B_WIDTH)
    if cache is None:
        tq = _tile(seq_len, 2 * SB_BLOCK)
        b_out = _stick_breaking(q3, k3, v3, k3, v3, tq=tq, tk=min(tq, SB_BLOCK), causal_past=True)
    else:
        k_cache, v_cache = cache
        past = k_cache.shape[1]
        b_out = _stick_breaking(q3, k3, v3, k_cache.reshape(bsz, past, B_WIDTH),
                                v_cache.reshape(bsz, past, B_WIDTH),
                                tq=seq_len, tk=_tile(past, SB_BLOCK), causal_past=False)
    b_out = b_out.reshape(m, B_WIDTH)

    x1, x1b = _mm([(a_out, w_o, 0), (b_out, w_o[A_WIDTH:], 0)], _ln_residual_epilogue, [F32, BF],
                  n=D_MODEL, tm=256, tn=D_MODEL, tiles=(x,), rows=(_row(ln_g), _row(ln_b)),
                  name="even_out_ln", col_chunk=None)
    return x1, x1b, k, v, va


def _rwkv_mixer(x, seq_len, x_last, s0, v_first, P, ln_g, ln_b):
    m = x.shape[0]
    bsz = m // seq_len
    mixes = _token_shift(x.reshape(bsz, seq_len, D_MODEL), x_last, P['mu'])
    xr, xw, xk, xv, xa, xg = [a.reshape(m, D_MODEL) for a in mixes]
    big = lambda xin, w: _mm([(xin, w, 0)], _identity_epilogue, [F32], n=D_MODEL, tm=512, tn=1024,
                             name="rwkv_proj")[0]
    r = big(xr, P['w_r'])
    k = big(xk, P['w_k'])

    def lora_in(xin, w, act):
        n = w.shape[1]
        return _mm([(xin, w, 0)], lambda a, t, r_: (act(a[0]),), [BF], n=n, tm=512, tn=n,
                   name="rwkv_lora_in")[0]

    def lora_out(h, w, bias, act):
        return _mm([(h, w, 0)], lambda a, t, r_: (act(r_[0] + a[0]),), [F32], n=D_MODEL, tm=512,
                   tn=1024, rows=(_row(bias),), name="rwkv_lora_out")[0]

    hw = lora_in(xw, P['w1'], jnp.tanh)
    lw = lora_out(hw, P['w2'], P['w0'],
                  lambda pre: -jnp.exp(-(jnp.maximum(-pre, 0.0) + jnp.log(1.0 + jnp.exp(-jnp.abs(pre)))) - 0.5))
    ha = lora_in(xa, P['a1'], lambda z: z)
    a = lora_out(ha, P['a2'], P['a0'], jax.nn.sigmoid)
    hg = lora_in(xg, P['g1'], jax.nn.sigmoid)
    g = _mm([(hg, P['g2'], 0)], _identity_epilogue, [F32], n=D_MODEL, tm=512, tn=1024, name="rwkv_gate")[0]
    if P['v1'] is None:
        v = big(xv, P['w_v'])
        v_first = v
    else:
        hv = lora_in(xv, P['v1'], lambda z: z)

        def v_epilogue(accs, tiles, rows):
            v_new = accs[0]
            return (v_new + (tiles[0] - v_new) * jax.nn.sigmoid(rows[0] + accs[1]),)
        v = _mm([(xv, P['w_v'], 0), (hv, P['v2'], 0)], v_epilogue, [F32], n=D_MODEL, tm=512, tn=1024,
                tiles=(v_first,), rows=(_row(P['v0']),), name="rwkv_v_residual")[0]
    to3 = lambda z: z.reshape(bsz, seq_len, D_MODEL)
    yg, s_new = _wkv(to3(r), to3(k), to3(v), to3(lw), to3(a), to3(g), s0,
                     P['k_k'], P['k_a'], P['r_k'], P['gn_g'], P['gn_b'])
    x1, x1b = _mm([(yg.reshape(m, D_MODEL), P['w_o'], 0)], _ln_residual_epilogue, [F32, BF],
                  n=D_MODEL, tm=256, tn=D_MODEL, tiles=(x,), rows=(_row(ln_g), _row(ln_b)),
                  name="rwkv_out_ln", col_chunk=None)
    return x1, x1b, s_new, v_first


def _ffn_ple(x1, x1b, seq_len, p, conv_prev, w_in, conv_w, conv_b, w_out, ln_g, ln_b, ple_gate, ple_proj):
    act, conv_rows = _ffn_in(x1b, w_in, conv_w, conv_b, conv_prev, seq_len)
    x2, x2b = _mm([(act, w_out, 0)], _ln_residual_epilogue, [F32, BF], n=D_MODEL, tm=512, tn=D_MODEL,
                  tk=D_FF // 4, tiles=(x1,), rows=(_row(ln_g), _row(ln_b)), name="ffn_out_ln", col_chunk=None)

    def ple_epilogue(accs, tiles, rows):
        y = tiles[0] + jax.nn.sigmoid(accs[0]) * accs[1]
        return y, y
    x3, x3b = _mm([(x2b, ple_gate, 0), (p, ple_proj, 0)], ple_epilogue, [F32, BF], n=D_MODEL, tm=512,
                  tn=1024, tiles=(x2,), name="ple")
    return x3, x3b, conv_rows


def _trunk(x, p, W, caches, wkv0, shift0, conv0):
    bsz, seq_len, _ = x.shape
    m = bsz * seq_len
    x = x.reshape(m, D_MODEL).astype(F32)
    xb = x.astype(BF)
    new_k, new_v, new_va, new_wkv, new_shift, new_conv = [], [], [], [], [], []
    v_first = None
    for i in range(DEPTH):
        j = i // 2
        if i % 2 == 0:
            cache = None if caches is None else (caches[0][j], caches[1][j])
            x1, x1b, k, v, va = _even_mixer(
                x, xb, seq_len, W['even_w_in'][j], W['even_lnv_g'][j], W['even_lnv_b'][j],
                W['even_w_s'][j], W['even_b_s'][j], W['even_w_o'][j], W['ln1_g'][i], W['ln1_b'][i], cache)
            new_k.append(k.reshape(bsz, seq_len, B_HEADS, B_HEAD_DIM))
            new_v.append(v.reshape(bsz, seq_len, B_HEADS, B_HEAD_DIM))
            new_va.append(va.reshape(bsz, seq_len, A_HEADS, A_DIM))
        else:
            P = dict(mu=W['c_mu'][j], w_r=W['c_w_r'][j], w_k=W['c_w_k'][j], w_v=W['c_w_v'][j],
                     w_o=W['c_w_o'][j], w0=W['c_w0'][j], w1=W['c_w1'][j], w2=W['c_w2'][j],
                     a0=W['c_a0'][j], a1=W['c_a1'][j], a2=W['c_a2'][j], g1=W['c_g1'][j], g2=W['c_g2'][j],
                     k_k=W['c_k_k'][j], k_a=W['c_k_a'][j], r_k=W['c_r_k'][j], gn_g=W['c_gn_g'][j],
                     gn_b=W['c_gn_b'][j],
                     v0=None if j == 0 else W['c_v0'][j - 1], v1=None if j == 0 else W['c_v1'][j - 1],
                     v2=None if j == 0 else W['c_v2'][j - 1])
            x1, x1b, s_new, v_first = _rwkv_mixer(x, seq_len, shift0[j], wkv0[j], v_first, P,
                                                  W['ln1_g'][i], W['ln1_b'][i])
            new_wkv.append(s_new)
            new_shift.append(x.reshape(bsz, seq_len, D_MODEL)[:, -1])
        x, xb, conv_rows = _ffn_ple(
            x1, x1b, seq_len, p[i].reshape(m, -1), conv0[i], W['ffn_w_in'][i], W['ffn_conv_w'][i],
            W['ffn_conv_b'][i], W['ffn_w_out'][i], W['ln2_g'][i], W['ln2_b'][i], W['ple_gate'][i],
            W['ple_proj'][i])
        new_conv.append(conv_rows)
    st = lambda l: jnp.stack(l) if l else None
    return (x.reshape(bsz, seq_len, D_MODEL), st(new_k), st(new_v), st(new_va), st(new_wkv),
            st(new_shift), st(new_conv))


_MATRICES = ('even_w_in', 'even_w_o', 'c_w_r', 'c_w_k', 'c_w_v', 'c_w_o', 'c_w1', 'c_w2', 'c_a1', 'c_a2',
             'c_v1', 'c_v2', 'c_g1', 'c_g2', 'ffn_w_in', 'ffn_w_out', 'ple_proj', 'ple_gate')


def kernel(x_prompt, x_sample, cache_sb_k, cache_sb_v, state_rwkv_wkv, state_rwkv_shift, state_ffn_conv, p_prompt, p_sample, even_w_in, even_lnv_g, even_lnv_b, even_w_s, even_b_s, even_w_o, c_mu, c_w_r, c_w_k, c_w_v, c_w_o, c_w0, c_w1, c_w2, c_a0, c_a1, c_a2, c_v0, c_v1, c_v2, c_g1, c_g2, c_k_k, c_k_a, c_r_k, c_gn_g, c_gn_b, ffn_w_in, ffn_conv_w, ffn_conv_b, ffn_w_out, ln1_g, ln1_b, ln2_g, ln2_b, ple_proj, ple_gate):
    W = dict(even_w_in=even_w_in, even_lnv_g=even_lnv_g, even_lnv_b=even_lnv_b, even_w_s=even_w_s,
             even_b_s=even_b_s, even_w_o=even_w_o, c_mu=c_mu, c_w_r=c_w_r, c_w_k=c_w_k,
             c_w_v=c_w_v, c_w_o=c_w_o, c_w0=c_w0, c_w1=c_w1, c_w2=c_w2, c_a0=c_a0, c_a1=c_a1,
             c_a2=c_a2, c_v0=c_v0, c_v1=c_v1, c_v2=c_v2, c_g1=c_g1, c_g2=c_g2, c_k_k=c_k_k,
             c_k_a=c_k_a, c_r_k=c_r_k, c_gn_g=c_gn_g, c_gn_b=c_gn_b, ffn_w_in=ffn_w_in,
             ffn_conv_w=ffn_conv_w, ffn_conv_b=ffn_conv_b, ffn_w_out=ffn_w_out, ln1_g=ln1_g,
             ln1_b=ln1_b, ln2_g=ln2_g, ln2_b=ln2_b, ple_proj=ple_proj, ple_gate=ple_gate)
    for name in _MATRICES:
        W[name] = W[name].astype(BF)
    bp = x_prompt.shape[0]
    n_odd, n_even = state_rwkv_wkv.shape[0], cache_sb_k.shape[0]
    wkv_zero = jnp.zeros((n_odd, bp, C_HEADS, C_HEAD_DIM, C_HEAD_DIM), F32)
    shift_zero = jnp.zeros((n_odd, bp, D_MODEL), F32)
    conv_zero = jnp.zeros((DEPTH, bp, CONV_W - 1, D_FF), F32)
    y_p, k_p, v_p, _, wkv_p, shift_p, conv_p = _trunk(
        x_prompt, p_prompt, W, None, wkv_zero, shift_zero, conv_zero)
    y_s, k_s, v_s, va_s, wkv_s, shift_s, conv_s = _trunk(
        x_sample, p_sample, W, (cache_sb_k, cache_sb_v), state_rwkv_wkv, state_rwkv_shift, state_ffn_conv)
    return (y_p, y_s, k_p, v_p, wkv_p, shift_p, conv_p, k_s, v_s, va_s, wkv_s, shift_s, conv_s)
```

```python
import functools

import jax
import jax.numpy as jnp
from jax import lax
from jax.experimental import pallas as pl
from jax.experimental.pallas import tpu as pltpu

F32 = jnp.float32
BF = jnp.bfloat16

D_MODEL = 2048
DEPTH = 4
CHUNK = 64
A_WIDTH = D_MODEL // 2
A_HEADS = 8
A_DIM = A_WIDTH // A_HEADS
A_BLOCK = 128
B_HEAD_DIM = 64
B_WIDTH = D_MODEL // 2
B_HEADS = B_WIDTH // B_HEAD_DIM
C_HEAD_DIM = 64
C_HEADS = D_MODEL // C_HEAD_DIM
C_GN_EPS = 64e-5
D_FF = 5632
CONV_W = 3
LN_EPS = 1e-5
DN_ALPHA = (2 * DEPTH) ** 0.25

MXU_WIDTH = 256
LANES = 128
SUBLANES = 8
HEAD_PAIR = 2 * C_HEAD_DIM
N_PAIRS = D_MODEL // HEAD_PAIR
WKV_CHUNK = 64
WKV_PAIRS_PER_STEP = 16
SB_BLOCK = 256
SB_SUBTILE = 256
SB_CHAIN_LAG = 0
SB_CUMSUM_PASSES = 1
VMEM_CAP = 56 * 1024 * 1024

NT_DIMS = (((1,), (1,)), ((), ()))
TN_DIMS = (((0,), (0,)), ((), ()))


def _tile(m, pref):
    return pref if m % pref == 0 else m


def _params(sem, est_bytes):
    limit = int(min(max(est_bytes * 5 // 4 + (4 << 20), 16 << 20), VMEM_CAP))
    return pltpu.CompilerParams(dimension_semantics=sem, vmem_limit_bytes=limit)


def _nbytes(shape, dtype):
    n = 1
    for s in shape:
        n *= s
    return n * jnp.dtype(dtype).itemsize


def _layernorm(x, g, b, eps=LN_EPS):
    mu = jnp.mean(x, axis=-1, keepdims=True)
    d = x - mu
    var = jnp.mean(d * d, axis=-1, keepdims=True)
    return d * lax.rsqrt(var + eps) * g + b


def _bdot(a, b):
    return jnp.dot(a.astype(BF), b.astype(BF), preferred_element_type=F32)


def _run_in_lockstep(generators, lag=0):
    results = [None] * len(generators)
    started, live, tick = 0, [], 0
    while started < len(generators) or live:
        while started < len(generators) and tick >= started * lag:
            live.append(started)
            started += 1
        for idx in list(live):
            try:
                next(generators[idx])
            except StopIteration as done:
                results[idx] = done.value
                live.remove(idx)
        tick += 1
    return results


def _mm_body(*refs, nprod, ntile, nrow, nout, nk, epilogue, col_chunk):
    xs = refs[:nprod]
    ws = refs[nprod:2 * nprod]
    ts = refs[2 * nprod:2 * nprod + ntile]
    rs = refs[2 * nprod + ntile:2 * nprod + ntile + nrow]
    outs = refs[2 * nprod + ntile + nrow:2 * nprod + ntile + nrow + nout]
    accs = refs[2 * nprod + ntile + nrow + nout:]

    def finish(vals, cols=slice(None)):
        res = epilogue(vals, [t[:, cols] for t in ts], [r[:, cols] for r in rs])
        for o_ref, o in zip(outs, res):
            o_ref[:, cols] = o.astype(o_ref.dtype)

    tn = outs[0].shape[1]
    if nk == 1 and col_chunk is not None and tn > col_chunk and tn % col_chunk == 0:
        lhs = [x[...].astype(BF) for x in xs]
        pending = None
        for ch in range(tn // col_chunk):
            cols = slice(ch * col_chunk, (ch + 1) * col_chunk)
            cur = [jnp.dot(a, w[:, cols], preferred_element_type=F32) for a, w in zip(lhs, ws)]
            if pending is not None:
                finish(*pending)
            pending = (cur, cols)
        finish(*pending)
        return

    prods = [jnp.dot(x[...].astype(BF), w[...], preferred_element_type=F32) for x, w in zip(xs, ws)]
    if nk == 1:
        finish(prods)
        return
    k = pl.program_id(2)

    @pl.when(k == 0)
    def _():
        for a, p in zip(accs, prods):
            a[...] = p

    @pl.when(k > 0)
    def _():
        for a, p in zip(accs, prods):
            a[...] += p

    @pl.when(k == nk - 1)
    def _():
        finish([a[...] for a in accs])


def _mm(prods, epilogue, out_dtypes, *, n, tm, tn, tk=None, tiles=(), rows=(), name="mm",
        col_chunk=MXU_WIDTH):
    prods = [tuple(p) + (0,) * (4 - len(p)) for p in prods]
    m = prods[0][0].shape[0]
    tm = _tile(m, tm)
    ks = [p[0].shape[1] for p in prods]
    nk = 1 if tk is None else ks[0] // tk
    if nk > 1:
        assert all(k == ks[0] for k in ks) and ks[0] % tk == 0
    assert n % tn == 0 and m % tm == 0
    grid = (n // tn, m // tm, nk)
    in_specs, est = [], 0
    for k_dim in ks:
        bk = k_dim if nk == 1 else tk
        in_specs.append(pl.BlockSpec((tm, bk), lambda j, i, k: (i, k)))
    weights = []
    for (x, w, off, row_off), k_dim in zip(prods, ks):
        bk = k_dim if nk == 1 else tk
        assert off % tn == 0 and row_off % bk == 0
        spec, operand = _weight_spec(w, bk, tn, lambda k, rb=row_off // bk: k + rb,
                                     lambda j, ob=off // tn: j + ob)
        in_specs.append(spec)
        weights.append(operand)
        est += 2 * (_nbytes((tm, bk), x.dtype) + _nbytes((bk, tn), w.dtype))
    for t in tiles:
        in_specs.append(pl.BlockSpec((tm, tn), lambda j, i, k: (i, j)))
        est += 2 * _nbytes((tm, tn), t.dtype)
    for r in rows:
        in_specs.append(pl.BlockSpec((1, tn), lambda j, i, k: (0, j)))
    out_specs = [pl.BlockSpec((tm, tn), lambda j, i, k: (i, j)) for _ in out_dtypes]
    out_shape = [jax.ShapeDtypeStruct((m, n), dt) for dt in out_dtypes]
    est += sum(2 * _nbytes((tm, tn), dt) for dt in out_dtypes)
    est += (len(prods) + 2) * _nbytes((tm, tn), F32)
    scratch = [pltpu.VMEM((tm, tn), F32) for _ in prods] if nk > 1 else []
    body = functools.partial(_mm_body, nprod=len(prods), ntile=len(tiles), nrow=len(rows),
                             nout=len(out_dtypes), nk=nk, epilogue=epilogue, col_chunk=col_chunk)
    return pl.pallas_call(
        body, name=name, grid=grid, in_specs=in_specs, out_specs=out_specs, out_shape=out_shape,
        scratch_shapes=scratch,
        compiler_params=_params(("parallel", "parallel", "arbitrary"), est),
    )(*[p[0] for p in prods], *weights, *tiles, *rows)


def _row(v):
    return v.reshape(1, -1).astype(F32)


class _LayerOf:
    def __init__(self, stack, layer):
        self.stack, self.layer = stack, layer
        self.shape, self.dtype = stack.shape[1:], stack.dtype


class _Layers:
    def __init__(self, stack):
        self.stack = stack

    def __getitem__(self, layer):
        return _LayerOf(self.stack, layer)


def _weight_spec(w, rows, cols, row_block, col_block):
    if isinstance(w, _LayerOf):
        return (pl.BlockSpec((None, rows, cols), lambda j, i, k=0: (w.layer, row_block(k), col_block(j))),
                w.stack)
    return pl.BlockSpec((rows, cols), lambda j, i, k=0: (row_block(k), col_block(j))), w


def _gate_body(u_ref, va_ref, ws_ref, bs_ref, o_ref, *, blk, nblk):
    ti = lax.broadcasted_iota(jnp.int32, (blk, blk), 0) // CHUNK
    si = lax.broadcasted_iota(jnp.int32, (blk, blk), 1) // CHUNK
    allowed = ti >= si
    for h in range(A_HEADS):
        w = jnp.where(allowed, ws_ref[h], 0.0).astype(BF)
        bias = bs_ref[h]
        for nb in range(nblk):
            rs = slice(nb * blk, (nb + 1) * blk)
            cs = slice(h * A_DIM, (h + 1) * A_DIM)
            mixed = jnp.dot(w, va_ref[rs, cs], preferred_element_type=F32) + bias
            o_ref[rs, cs] = (u_ref[rs, cs].astype(F32) * mixed).astype(o_ref.dtype)


def _spatial_gate(u, va, w_s, b_s, blk):
    m = u.shape[0]
    tm = _tile(m, 512)
    nblk = tm // blk
    est = 2 * 3 * _nbytes((tm, A_WIDTH), BF) + 2 * _nbytes((A_HEADS, blk, blk), F32)
    return pl.pallas_call(
        functools.partial(_gate_body, blk=blk, nblk=nblk),
        name="spatial_gate",
        grid=(m // tm,),
        in_specs=[pl.BlockSpec((tm, A_WIDTH), lambda i: (i, 0)),
                  pl.BlockSpec((tm, A_WIDTH), lambda i: (i, 0)),
                  pl.BlockSpec((A_HEADS, blk, blk), lambda i: (0, 0, 0)),
                  pl.BlockSpec((A_HEADS, blk, 1), lambda i: (0, 0, 0))],
        out_specs=pl.BlockSpec((tm, A_WIDTH), lambda i: (i, 0)),
        out_shape=jax.ShapeDtypeStruct((m, A_WIDTH), BF),
        compiler_params=_params(("parallel",), est),
    )(u, va, w_s.astype(F32), b_s.astype(F32)[:, :, None])


def _sb_body(q_ref, kd_ref, vd_ref, kp_ref, vp_ref, o_ref, *, sub, nq, tk, blocks_per_iter, past_iters,
             unroll):
    head_a_q = lax.broadcasted_iota(jnp.int32, (sub, LANES), 1) < B_HEAD_DIM
    q_subs = []
    for i in range(nq):
        q = q_ref[i * sub:(i + 1) * sub, :]
        zero_q = jnp.zeros_like(q)
        q_subs.append((jnp.where(head_a_q, q, zero_q), jnp.where(head_a_q, zero_q, q)))

    def chain(qh, kb, valid, later, carry):
        z = lax.dot_general(qh, kb, NT_DIMS, preferred_element_type=F32)
        yield
        neg_z = -z
        log_fail = jnp.minimum(neg_z, 0.0) - jnp.log(1.0 + jnp.exp(jnp.minimum(z, neg_z)))
        log_hit = z + log_fail
        if valid is not None:
            log_fail = jnp.where(valid, log_fail, 0.0)
        part = log_fail.astype(BF)
        after = jnp.dot(part, later, preferred_element_type=F32)
        rest = log_fail
        for _ in range(SB_CUMSUM_PASSES - 1):
            rest = rest - part.astype(F32)
            part = rest.astype(BF)
            after = after + jnp.dot(part, later, preferred_element_type=F32)
        yield
        total = after[:, 0:1] + log_fail[:, 0:1]
        if carry is None:
            return log_hit + after, total
        w = jnp.exp(log_hit + after + carry)
        if valid is not None:
            w = jnp.where(valid, w, 0.0)
        return w.astype(BF), total

    def attend(k_blocks, v_blocks, masks, carries, accs):
        kbs = [k.astype(BF) for k in k_blocks]
        laters, v_stacks = {}, []
        for kb, v_blk in zip(kbs, v_blocks):
            nkeys = kb.shape[0]
            if nkeys not in laters:
                jj = lax.broadcasted_iota(jnp.int32, (nkeys, nkeys), 0)
                ss = lax.broadcasted_iota(jnp.int32, (nkeys, nkeys), 1)
                laters[nkeys] = jnp.where(jj > ss, 1.0, 0.0).astype(BF)
            vb = v_blk.astype(BF)
            head_a_k = lax.broadcasted_iota(jnp.int32, (nkeys, LANES), 1) < B_HEAD_DIM
            zero_v = jnp.zeros_like(vb)
            v_stacks.append(jnp.concatenate([jnp.where(head_a_k, vb, zero_v), jnp.where(head_a_k, zero_v, vb)],
                                            axis=0))
        early = len(kbs) == 1
        keys, gens = [], []
        for i in range(nq):
            for j, kb in enumerate(kbs):
                if isinstance(masks[i][j], str):
                    continue
                for h in range(2):
                    keys.append((i, j, h))
                    gens.append(chain(q_subs[i][h], kb, masks[i][j], laters[kb.shape[0]],
                                      carries[i][h] if early else None))
        scores = dict(zip(keys, _run_in_lockstep(gens, lag=SB_CHAIN_LAG)))
        new_carries, new_accs = [], []
        for i in range(nq):
            cs = list(carries[i])
            wts, vals = [], []
            for j in range(len(kbs)):
                if isinstance(masks[i][j], str):
                    continue
                for h in range(2):
                    w, total = scores[(i, j, h)]
                    if not early:
                        w = jnp.exp(w + cs[h])
                        if masks[i][j] is not None:
                            w = jnp.where(masks[i][j], w, 0.0)
                        w = w.astype(BF)
                    cs[h] = cs[h] + total
                    wts.append(w)
                vals.append(v_stacks[j])
            new_carries.append(cs)
            new_accs.append(accs[i] + jnp.dot(jnp.concatenate(wts, axis=1), jnp.concatenate(vals, axis=0),
                                              preferred_element_type=F32))
        return new_carries, new_accs

    k_diag = [kd_ref[j * sub:(j + 1) * sub, :] for j in reversed(range(nq))]
    v_diag = [vd_ref[j * sub:(j + 1) * sub, :] for j in reversed(range(nq))]
    tt = lax.broadcasted_iota(jnp.int32, (sub, sub), 0)
    sk = lax.broadcasted_iota(jnp.int32, (sub, sub), 1)
    causal = sk < tt
    masks = [[causal if j == i else (None if j < i else "skip") for j in reversed(range(nq))]
             for i in range(nq)]
    zeros_c = jnp.zeros((sub, 1), F32)
    carries, accs = attend(k_diag, v_diag, masks, [[zeros_c, zeros_c] for _ in range(nq)],
                           [jnp.zeros((sub, LANES), F32) for _ in range(nq)])
    n_iters = past_iters(pl.program_id(2))
    open_masks = [[None] * blocks_per_iter for _ in range(nq)]

    def step(it, state):
        carries = [list(state[2 * i:2 * i + 2]) for i in range(nq)]
        accs = list(state[2 * nq:])
        newest = n_iters - 1 - it
        offs = [pl.multiple_of((newest * blocks_per_iter + j) * tk, tk)
                for j in reversed(range(blocks_per_iter))]
        carries, accs = attend([kp_ref[pl.ds(o, tk), :] for o in offs], [vp_ref[pl.ds(o, tk), :] for o in offs],
                               open_masks, carries, accs)
        return tuple(c for pair in carries for c in pair) + tuple(accs)

    state = lax.fori_loop(0, n_iters, step, tuple(c for pair in carries for c in pair) + tuple(accs),
                          unroll=unroll)
    for i in range(nq):
        o_ref[i * sub:(i + 1) * sub, :] = state[2 * nq + i].astype(o_ref.dtype)


def _stick_breaking(q, k_new, v_new, k_past, v_past, *, tq, tk, causal_past, past_layer=None):
    bsz, t, _ = q.shape
    p_len = k_past.shape[-2]
    if past_layer is None:
        past_spec = pl.BlockSpec((None, p_len, LANES), lambda b, p, i: (b, 0, p))
    else:
        past_spec = pl.BlockSpec((None, None, p_len, LANES), lambda b, p, i: (past_layer, b, 0, p))
    sub = min(tq, SB_SUBTILE)
    nq = tq // sub
    if causal_past:
        blocks_per_iter, unroll = 1, None
        past_iters = lambda qi: qi * (tq // tk)
    else:
        n_blocks = p_len // tk
        blocks_per_iter, unroll = (2 if n_blocks % 2 == 0 else 1), True
        past_iters = lambda qi: n_blocks // blocks_per_iter
    est = (2 * 2 * _nbytes((p_len, LANES), k_past.dtype) + 2 * 4 * _nbytes((tq, LANES), F32)
           + 12 * 2 * nq * blocks_per_iter * _nbytes((sub, max(sub, tk)), F32))
    return pl.pallas_call(
        functools.partial(_sb_body, sub=sub, nq=nq, tk=tk, blocks_per_iter=blocks_per_iter,
                          past_iters=past_iters, unroll=unroll),
        name="stick_breaking_past" if causal_past else "stick_breaking_cache",
        grid=(bsz, B_WIDTH // LANES, t // tq),
        in_specs=[pl.BlockSpec((None, tq, LANES), lambda b, p, i: (b, i, p)),
                  pl.BlockSpec((None, tq, LANES), lambda b, p, i: (b, i, p)),
                  pl.BlockSpec((None, tq, LANES), lambda b, p, i: (b, i, p)),
                  past_spec, past_spec],
        out_specs=pl.BlockSpec((None, tq, LANES), lambda b, p, i: (b, i, p)),
        out_shape=jax.ShapeDtypeStruct((bsz, t, B_WIDTH), BF),
        compiler_params=_params(("parallel", "parallel", "arbitrary"), est),
    )(q, k_new, v_new, k_past, v_past)


def _shift_body(x_ref, last_ref, mu_ref, *rest, tiles_per_seq):
    outs, carry = rest[:6], rest[6]
    i = pl.program_id(1)

    @pl.when(i % tiles_per_seq == 0)
    def _():
        carry[...] = jnp.broadcast_to(last_ref[...], carry.shape)

    x = x_ref[...]
    rows = lax.broadcasted_iota(jnp.int32, x.shape, 0)
    x_prev = jnp.where(rows == 0, carry[SUBLANES - 1:SUBLANES, :], pltpu.roll(x, 1, 0))
    carry[...] = x[x.shape[0] - SUBLANES:, :]
    xx = x_prev - x
    for j, o_ref in enumerate(outs):
        o_ref[...] = (x + xx * mu_ref[j:j + 1, :]).astype(o_ref.dtype)


def _token_shift(x, x_last, mu):
    bsz, t, d = x.shape
    tm = _tile(t, 256)
    est = 2 * _nbytes((tm, d), F32) + 12 * _nbytes((tm, d), BF) + 4 * _nbytes((tm, d), F32)
    return pl.pallas_call(
        functools.partial(_shift_body, tiles_per_seq=t // tm),
        name="token_shift",
        grid=(bsz, t // tm),
        in_specs=[pl.BlockSpec((None, tm, d), lambda b, i: (b, i, 0)),
                  pl.BlockSpec((None, 1, d), lambda b, i: (b, 0, 0)),
                  pl.BlockSpec((6, d), lambda b, i: (0, 0))],
        out_specs=[pl.BlockSpec((None, tm, d), lambda b, i: (b, i, 0)) for _ in range(6)],
        out_shape=[jax.ShapeDtypeStruct((bsz, t, d), BF) for _ in range(6)],
        scratch_shapes=[pltpu.VMEM((SUBLANES, d), F32)],
        compiler_params=_params(("parallel", "arbitrary"), est),
    )(x, x_last[:, None, :].astype(F32), mu.astype(F32))


def _wkv_body(r_ref, k_ref, v_ref, lw_ref, a_ref, g_ref, kk_ref, ka_ref, rk_ref, gg_ref, gb_ref,
              s0_ref, y_ref, s_ref):
    c = WKV_CHUNK
    n = 2 * c

    @pl.when(pl.program_id(1) == 0)
    def _():
        s_ref[...] = s0_ref[...]

    head_a = lax.broadcasted_iota(jnp.int32, (c, LANES), 1) < C_HEAD_DIM
    ri = lax.broadcasted_iota(jnp.int32, (n, n), 0)
    ci = lax.broadcasted_iota(jnp.int32, (n, n), 1)
    same_head = (ri // c) == (ci // c)
    strict = same_head & (ci < ri)
    inclusive = same_head & (ci <= ri)
    diag16 = (ri // 16) == (ci // 16)
    diag32 = (ri // 32) == (ci // 32)
    eye = jnp.where(ri == ci, 1.0, 0.0).astype(F32)
    seg_ones = jnp.where(same_head, 1.0, 0.0).astype(BF)
    ti = lax.broadcasted_iota(jnp.int32, (c, c), 0)
    si = lax.broadcasted_iota(jnp.int32, (c, c), 1)
    prefix = jnp.where(si <= ti, 1.0, 0.0).astype(BF)

    def split2(x):
        hi = x.astype(BF)
        return hi, (x - hi.astype(F32)).astype(BF)

    def seg_sum(x):
        hi, lo = split2(x)
        return (jnp.dot(hi, seg_ones, preferred_element_type=F32)
                + jnp.dot(lo, seg_ones, preferred_element_type=F32))

    def cumsum_time(x):
        hi, rest = split2(x)
        rest_f = x - hi.astype(F32)
        mid = rest_f.astype(BF)
        lo = (rest_f - mid.astype(F32)).astype(BF)
        return (jnp.dot(prefix, hi, preferred_element_type=F32)
                + jnp.dot(prefix, mid, preferred_element_type=F32)
                + jnp.dot(prefix, lo, preferred_element_type=F32))

    def stack(x):
        zero = jnp.zeros_like(x)
        return jnp.concatenate([jnp.where(head_a, x, zero), jnp.where(head_a, zero, x)], axis=0)

    def pair(p, s):
        sl = pl.ds(pl.multiple_of(p * LANES, LANES), LANES)
        r = r_ref[:, sl]
        k = k_ref[:, sl]
        v = v_ref[:, sl]
        lw = lw_ref[:, sl]
        a = a_ref[:, sl]
        kkp = k * kk_ref[:, sl]
        norm = jnp.sqrt(seg_sum(kkp * kkp))
        yield
        kk = kkp / jnp.maximum(norm, 1e-12)
        k2 = k * (1.0 + (a - 1.0) * ka_ref[:, sl])
        bonus = seg_sum(r * k2 * rk_ref[:, sl]) * v
        yield
        cs = cumsum_time(lw)
        yield
        e_pos = jnp.exp(cs)
        e_neg = jnp.exp(-cs)
        a_t = -kk * e_pos * jnp.exp(-lw)
        b_t = kk * a * e_neg
        k_t = k2 * e_neg
        r_t = r * e_pos
        la, lr = stack(a_t), stack(r_t)
        rb, rk = stack(b_t), stack(k_t)
        vs = stack(v)
        q = lax.dot_general(jnp.concatenate([la, lr], axis=0).astype(BF),
                            jnp.concatenate([rb, rk], axis=0).astype(BF),
                            NT_DIMS, preferred_element_type=F32)
        yield
        p_ab = jnp.where(strict, q[:n, :n], 0.0)
        a_ak = jnp.where(strict, q[:n, n:], 0.0)
        a_rb = jnp.where(inclusive, q[n:, :n], 0.0)
        a_rk = jnp.where(inclusive, q[n:, n:], 0.0)
        akv = _bdot(a_ak, vs)
        yield
        pd = jnp.where(diag16, p_ab, 0.0)
        inv = eye + pd
        pw = pd
        for _ in range(3):
            pw = _bdot(pw, pw)
            yield
            inv = inv + _bdot(pw, inv)
            yield
        for off_diag in (jnp.where(diag32 & (~diag16), p_ab, 0.0), jnp.where(diag32, 0.0, p_ab)):
            right = _bdot(off_diag, inv)
            yield
            inv = inv + _bdot(inv, right)
            yield
        wu = _bdot(inv, jnp.concatenate([la, akv], axis=1))
        yield
        w_m, u_m = wu[:, :LANES], wu[:, LANES:]
        wr = lax.dot_general(jnp.concatenate([w_m, lr], axis=0).astype(BF), s.astype(BF),
                             NT_DIMS, preferred_element_type=F32)
        yield
        sa = wr[:n] + u_m
        sv = jnp.concatenate([sa, vs], axis=0)
        y_st = wr[n:] + _bdot(jnp.concatenate([a_rb, a_rk], axis=1), sv)
        g_end = e_pos[c - 1:c, :]
        rg = jnp.concatenate([rb, rk], axis=0) * g_end
        s_new = s * g_end + lax.dot_general(sv.astype(BF), rg.astype(BF), TN_DIMS,
                                            preferred_element_type=F32)
        yield
        y = y_st[:c] + y_st[c:]
        mean = seg_sum(y) * (1.0 / C_HEAD_DIM)
        yield
        d = y - mean
        var = seg_sum(d * d) * (1.0 / C_HEAD_DIM)
        yield
        yn = d * lax.rsqrt(var + C_GN_EPS) * gg_ref[:, sl] + gb_ref[:, sl]
        return s_new, ((yn + bonus) * g_ref[:, sl]).astype(y_ref.dtype)

    def group(it, carry):
        pairs = [it * WKV_PAIRS_PER_STEP + u for u in range(WKV_PAIRS_PER_STEP)]
        states = [s_ref[p] for p in pairs]
        results = _run_in_lockstep([pair(p, s) for p, s in zip(pairs, states)])
        for p, (s_new, y_out) in zip(pairs, results):
            s_ref[p] = s_new
            y_ref[:, pl.ds(pl.multiple_of(p * LANES, LANES), LANES)] = y_out
        return carry

    lax.fori_loop(0, N_PAIRS // WKV_PAIRS_PER_STEP, group, 0)


def _pair_state(s):
    bsz = s.shape[0]
    s = s.reshape(bsz, N_PAIRS, 2, C_HEAD_DIM, C_HEAD_DIM).astype(F32)
    z = jnp.zeros_like(s[:, :, 0])
    top = jnp.concatenate([s[:, :, 0], z], axis=-1)
    bot = jnp.concatenate([z, s[:, :, 1]], axis=-1)
    return jnp.concatenate([top, bot], axis=-2)


def _unpair_state(sp):
    bsz = sp.shape[0]
    a = sp[:, :, :C_HEAD_DIM, :C_HEAD_DIM]
    b = sp[:, :, C_HEAD_DIM:, C_HEAD_DIM:]
    return jnp.stack([a, b], axis=2).reshape(bsz, C_HEADS, C_HEAD_DIM, C_HEAD_DIM)


def _wkv(r, k, v, lw, a, g, s0, k_k, k_a, r_k, gn_g, gn_b):
    bsz, t, d = r.shape
    c = WKV_CHUNK
    seq = pl.BlockSpec((None, c, d), lambda b, i: (b, i, 0))
    vec = pl.BlockSpec((1, d), lambda b, i: (0, 0))
    st = pl.BlockSpec((None, N_PAIRS, HEAD_PAIR, HEAD_PAIR), lambda b, i: (b, 0, 0, 0))
    est = 2 * 6 * _nbytes((c, d), F32) + 4 * _nbytes((N_PAIRS, HEAD_PAIR, HEAD_PAIR), F32) + (8 << 20)
    y, s_new = pl.pallas_call(
        _wkv_body,
        name="wkv7_chunked",
        grid=(bsz, t // c),
        in_specs=[seq] * 6 + [vec] * 5 + [st],
        out_specs=[seq, st],
        out_shape=[jax.ShapeDtypeStruct((bsz, t, d), BF),
                   jax.ShapeDtypeStruct((bsz, N_PAIRS, HEAD_PAIR, HEAD_PAIR), F32)],
        compiler_params=_params(("parallel", "arbitrary"), est),
    )(r, k, v, lw, a, g, _row(k_k), _row(k_a), _row(r_k), _row(gn_g), _row(gn_b), _pair_state(s0))
    return y, _unpair_state(s_new)


def _ffn_in_body(x_ref, wg_ref, wu_ref, cw_ref, cb_ref, prev_ref, act_ref, st_ref, tail_ref, *,
                 seq_len, tm):
    x = x_ref[...]
    tn = act_ref.shape[1]
    cn = min(tn, MXU_WIDTH)
    last2, last1 = SUBLANES - 2, SUBLANES - 1
    carried = seq_len >= tm
    if carried:
        @pl.when(pl.program_id(1) % (seq_len // tm) == 0)
        def _():
            tail_ref[...] = prev_ref[0]

    def conv_gate(hg, hu, cols):
        rows = lax.broadcasted_iota(jnp.int32, hg.shape, 0)
        if carried:
            tail = tail_ref[:, cols]
            p0, p1 = tail[last2:last2 + 1, :], tail[last1:last1 + 1, :]
            pos = rows
            new_tail = hg[tm - SUBLANES:, :]
            tail_ref[:, cols] = new_tail
            st_ref[0, :, cols] = new_tail
        else:
            nseq = tm // seq_len
            prev = prev_ref[:, :, cols]
            spread = lambda rw: jnp.broadcast_to(rw, (nseq, seq_len, cn)).reshape(tm, cn)
            p0, p1 = spread(prev[:, last2:last2 + 1, :]), spread(prev[:, last1:last1 + 1, :])
            pos = rows % seq_len
            st_ref[:, :, cols] = hg.reshape(nseq, seq_len, cn)[:, seq_len - SUBLANES:, :]
        s1 = jnp.where(pos == 0, p1, pltpu.roll(hg, 1, 0))
        s2 = jnp.where(pos == 0, p0, jnp.where(pos == 1, p1, pltpu.roll(hg, 2, 0)))
        cw = cw_ref[:, cols]
        hc = cb_ref[:, cols] + s2 * cw[0:1, :] + s1 * cw[1:2, :] + hg * cw[2:3, :]
        act_ref[:, cols] = (jax.nn.gelu(hc) * hu).astype(act_ref.dtype)

    pending = None
    for ch in range(tn // cn):
        cols = slice(ch * cn, (ch + 1) * cn)
        hg = jnp.dot(x, wg_ref[:, cols], preferred_element_type=F32)
        hu = jnp.dot(x, wu_ref[:, cols], preferred_element_type=F32)
        if pending is not None:
            conv_gate(*pending)
        pending = (hg, hu, cols)
    conv_gate(*pending)


def _ffn_in(xb, w_in, conv_w, conv_b, conv_prev, seq_len):
    m, d = xb.shape
    nseq_total = m // seq_len
    tn = 512
    tm = _tile(m, 1024)
    nj = D_FF // tn
    prev8 = jnp.pad(conv_prev.astype(F32), ((0, 0), (SUBLANES - (CONV_W - 1), 0), (0, 0)))
    if seq_len >= tm:
        per = seq_len // tm
        nb = 1
        st_map = lambda j, i: (i // per, 0, j)
    else:
        nb = tm // seq_len
        st_map = lambda j, i: (i, 0, j)
    st_spec = pl.BlockSpec((nb, SUBLANES, tn), st_map)
    est = (2 * _nbytes((tm, d), BF) + 4 * _nbytes((d, tn), BF) + 2 * _nbytes((tm, tn), BF)
           + 8 * _nbytes((tm, tn), F32))
    gate_spec, w_operand = _weight_spec(w_in, d, tn, lambda k: 0, lambda j: j)
    up_spec, _ = _weight_spec(w_in, d, tn, lambda k: 0, lambda j: j + nj)
    act, tails = pl.pallas_call(
        functools.partial(_ffn_in_body, seq_len=seq_len, tm=tm),
        name="ffn_in_conv",
        grid=(nj, m // tm),
        in_specs=[pl.BlockSpec((tm, d), lambda j, i: (i, 0)),
                  gate_spec,
                  up_spec,
                  pl.BlockSpec((CONV_W, tn), lambda j, i: (0, j)),
                  pl.BlockSpec((1, tn), lambda j, i: (0, j)),
                  st_spec],
        out_specs=[pl.BlockSpec((tm, tn), lambda j, i: (i, j)), st_spec],
        out_shape=[jax.ShapeDtypeStruct((m, D_FF), BF),
                   jax.ShapeDtypeStruct((nseq_total, SUBLANES, D_FF), F32)],
        scratch_shapes=[pltpu.VMEM((SUBLANES, tn), F32)],
        compiler_params=_params(("parallel", "arbitrary"), est),
    )(xb, w_operand, w_operand, conv_w.astype(F32), _row(conv_b), prev8)
    return act, tails[:, SUBLANES - (CONV_W - 1):, :]


def _ln_residual_epilogue(accs, tiles, rows):
    mix = accs[0]
    for extra in accs[1:]:
        mix = mix + extra
    y = _layernorm(DN_ALPHA * tiles[0] + mix, rows[0], rows[1])
    return y, y


def _identity_epilogue(accs, tiles, rows):
    return (accs[0],)


def _even_mixer(x, xb, seq_len, w_in, lnv_g, lnv_b, w_s, b_s, w_o, ln_g, ln_b, cache):
    m = x.shape[0]
    bsz = m // seq_len
    proj = lambda name, off, ep, dts, rows=(), col_chunk=MXU_WIDTH: _mm(
        [(xb, w_in, off)], ep, dts, n=1024, tm=512, tn=1024, rows=rows, name=name, col_chunk=col_chunk)
    (u,) = proj("even_u", 0, lambda a, t, r: (jax.nn.gelu(a[0]),), [BF])

    def va_epilogue(a, t, r):
        va = _layernorm(jax.nn.gelu(a[0]), r[0], r[1])
        return va, va
    va, vab = proj("even_va", A_WIDTH, va_epilogue, [F32, BF], rows=(_row(lnv_g), _row(lnv_b)),
                   col_chunk=None)
    o = 2 * A_WIDTH
    (q,) = proj("even_q", o, lambda a, t, r: (a[0] * (B_HEAD_DIM ** -0.5),), [BF])
    (k,) = proj("even_k", o + B_WIDTH, _identity_epilogue, [F32])
    (v,) = proj("even_v", o + 2 * B_WIDTH, _identity_epilogue, [F32])

    blk = min(seq_len, A_BLOCK)
    a_out = _spatial_gate(u, vab, w_s[:, :blk, :blk], b_s[:, :blk], blk)

    q3 = q.reshape(bsz, seq_len, B_WIDTH)
    k3 = k.reshape(bsz, seq_len, B_WIDTH)
    v3 = v.reshape(bsz, seq_len, B_WIDTH)
    if cache is None:
        tq = _tile(seq_len, 2 * SB_BLOCK)
        b_out = _stick_breaking(q3, k3, v3, k3, v3, tq=tq, tk=min(tq, SB_BLOCK), causal_past=True)
    else:
        k_cache, v_cache, layer = cache
        b_out = _stick_breaking(q3, k3, v3, k_cache, v_cache, tq=seq_len,
                                tk=_tile(k_cache.shape[2], SB_BLOCK), causal_past=False, past_layer=layer)
    b_out = b_out.reshape(m, B_WIDTH)

    x1, x1b = _mm([(a_out, w_o, 0), (b_out, w_o, 0, A_WIDTH)], _ln_residual_epilogue, [F32, BF],
                  n=D_MODEL, tm=256, tn=D_MODEL, tiles=(x,), rows=(_row(ln_g), _row(ln_b)),
                  name="even_out_ln", col_chunk=None)
    return x1, x1b, k, v, va


def _rwkv_mixer(x, seq_len, x_last, s0, v_first, P, ln_g, ln_b):
    m = x.shape[0]
    bsz = m // seq_len
    mixes = _token_shift(x.reshape(bsz, seq_len, D_MODEL), x_last, P['mu'])
    xr, xw, xk, xv, xa, xg = [a.reshape(m, D_MODEL) for a in mixes]
    big = lambda xin, w: _mm([(xin, w, 0)], _identity_epilogue, [F32], n=D_MODEL, tm=512, tn=1024,
                             name="rwkv_proj")[0]
    r = big(xr, P['w_r'])
    k = big(xk, P['w_k'])

    def lora_in(xin, w, act):
        n = w.shape[1]
        return _mm([(xin, w, 0)], lambda a, t, r_: (act(a[0]),), [BF], n=n, tm=512, tn=n,
                   name="rwkv_lora_in")[0]

    def lora_out(h, w, bias, act):
        return _mm([(h, w, 0)], lambda a, t, r_: (act(r_[0] + a[0]),), [F32], n=D_MODEL, tm=512,
                   tn=1024, rows=(_row(bias),), name="rwkv_lora_out")[0]

    hw = lora_in(xw, P['w1'], jnp.tanh)
    lw = lora_out(hw, P['w2'], P['w0'],
                  lambda pre: -jnp.exp(-(jnp.maximum(-pre, 0.0) + jnp.log(1.0 + jnp.exp(-jnp.abs(pre)))) - 0.5))
    ha = lora_in(xa, P['a1'], lambda z: z)
    a = lora_out(ha, P['a2'], P['a0'], jax.nn.sigmoid)
    hg = lora_in(xg, P['g1'], jax.nn.sigmoid)
    g = _mm([(hg, P['g2'], 0)], _identity_epilogue, [F32], n=D_MODEL, tm=512, tn=1024, name="rwkv_gate")[0]
    if P['v1'] is None:
        v = big(xv, P['w_v'])
        v_first = v
    else:
        hv = lora_in(xv, P['v1'], lambda z: z)

        def v_epilogue(accs, tiles, rows):
            v_new = accs[0]
            return (v_new + (tiles[0] - v_new) * jax.nn.sigmoid(rows[0] + accs[1]),)
        v = _mm([(xv, P['w_v'], 0), (hv, P['v2'], 0)], v_epilogue, [F32], n=D_MODEL, tm=512, tn=1024,
                tiles=(v_first,), rows=(_row(P['v0']),), name="rwkv_v_residual")[0]
    to3 = lambda z: z.reshape(bsz, seq_len, D_MODEL)
    yg, s_new = _wkv(to3(r), to3(k), to3(v), to3(lw), to3(a), to3(g), s0,
                     P['k_k'], P['k_a'], P['r_k'], P['gn_g'], P['gn_b'])
    x1, x1b = _mm([(yg.reshape(m, D_MODEL), P['w_o'], 0)], _ln_residual_epilogue, [F32, BF],
                  n=D_MODEL, tm=256, tn=D_MODEL, tiles=(x,), rows=(_row(ln_g), _row(ln_b)),
                  name="rwkv_out_ln", col_chunk=None)
    return x1, x1b, s_new, v_first


def _ffn_ple(x1, x1b, seq_len, p, conv_prev, w_in, conv_w, conv_b, w_out, ln_g, ln_b, ple_gate, ple_proj):
    act, conv_rows = _ffn_in(x1b, w_in, conv_w, conv_b, conv_prev, seq_len)
    x2, x2b = _mm([(act, w_out, 0)], _ln_residual_epilogue, [F32, BF], n=D_MODEL, tm=512, tn=D_MODEL,
                  tk=D_FF // 4, tiles=(x1,), rows=(_row(ln_g), _row(ln_b)), name="ffn_out_ln", col_chunk=None)

    def ple_epilogue(accs, tiles, rows):
        y = tiles[0] + jax.nn.sigmoid(accs[0]) * accs[1]
        return y, y
    x3, x3b = _mm([(x2b, ple_gate, 0), (p, ple_proj, 0)], ple_epilogue, [F32, BF], n=D_MODEL, tm=512,
                  tn=1024, tiles=(x2,), name="ple")
    return x3, x3b, conv_rows


def _trunk(x, p, W, caches, wkv0, shift0, conv0):
    bsz, seq_len, _ = x.shape
    m = bsz * seq_len
    x = x.reshape(m, D_MODEL).astype(F32)
    xb = x.astype(BF)
    new_k, new_v, new_va, new_wkv, new_shift, new_conv = [], [], [], [], [], []
    v_first = None
    for i in range(DEPTH):
        j = i // 2
        if i % 2 == 0:
            cache = None if caches is None else (caches[0], caches[1], j)
            x1, x1b, k, v, va = _even_mixer(
                x, xb, seq_len, W['even_w_in'][j], W['even_lnv_g'][j], W['even_lnv_b'][j],
                W['even_w_s'][j], W['even_b_s'][j], W['even_w_o'][j], W['ln1_g'][i], W['ln1_b'][i], cache)
            new_k.append(k.reshape(bsz, seq_len, B_HEADS, B_HEAD_DIM))
            new_v.append(v.reshape(bsz, seq_len, B_HEADS, B_HEAD_DIM))
            new_va.append(va.reshape(bsz, seq_len, A_HEADS, A_DIM))
        else:
            P = dict(mu=W['c_mu'][j], w_r=W['c_w_r'][j], w_k=W['c_w_k'][j], w_v=W['c_w_v'][j],
                     w_o=W['c_w_o'][j], w0=W['c_w0'][j], w1=W['c_w1'][j], w2=W['c_w2'][j],
                     a0=W['c_a0'][j], a1=W['c_a1'][j], a2=W['c_a2'][j], g1=W['c_g1'][j], g2=W['c_g2'][j],
                     k_k=W['c_k_k'][j], k_a=W['c_k_a'][j], r_k=W['c_r_k'][j], gn_g=W['c_gn_g'][j],
                     gn_b=W['c_gn_b'][j],
                     v0=None if j == 0 else W['c_v0'][j - 1], v1=None if j == 0 else W['c_v1'][j - 1],
                     v2=None if j == 0 else W['c_v2'][j - 1])
            x1, x1b, s_new, v_first = _rwkv_mixer(x, seq_len, shift0[j], wkv0[j], v_first, P,
                                                  W['ln1_g'][i], W['ln1_b'][i])
            new_wkv.append(s_new)
            new_shift.append(x.reshape(bsz, seq_len, D_MODEL)[:, -1])
        x, xb, conv_rows = _ffn_ple(
            x1, x1b, seq_len, p[i].reshape(m, -1), conv0[i], W['ffn_w_in'][i], W['ffn_conv_w'][i],
            W['ffn_conv_b'][i], W['ffn_w_out'][i], W['ln2_g'][i], W['ln2_b'][i], W['ple_gate'][i],
            W['ple_proj'][i])
        new_conv.append(conv_rows)
    st = lambda l: jnp.stack(l) if l else None
    return (x.reshape(bsz, seq_len, D_MODEL), st(new_k), st(new_v), st(new_va), st(new_wkv),
            st(new_shift), st(new_conv))


_MATRICES = ('even_w_in', 'even_w_o', 'c_w_r', 'c_w_k', 'c_w_v', 'c_w_o', 'c_w1', 'c_w2', 'c_a1', 'c_a2',
             'c_v1', 'c_v2', 'c_g1', 'c_g2', 'ffn_w_in', 'ffn_w_out', 'ple_proj', 'ple_gate')


def kernel(x_prompt, x_sample, cache_sb_k, cache_sb_v, state_rwkv_wkv, state_rwkv_shift, state_ffn_conv, p_prompt, p_sample, even_w_in, even_lnv_g, even_lnv_b, even_w_s, even_b_s, even_w_o, c_mu, c_w_r, c_w_k, c_w_v, c_w_o, c_w0, c_w1, c_w2, c_a0, c_a1, c_a2, c_v0, c_v1, c_v2, c_g1, c_g2, c_k_k, c_k_a, c_r_k, c_gn_g, c_gn_b, ffn_w_in, ffn_conv_w, ffn_conv_b, ffn_w_out, ln1_g, ln1_b, ln2_g, ln2_b, ple_proj, ple_gate):
    W = dict(even_w_in=even_w_in, even_lnv_g=even_lnv_g, even_lnv_b=even_lnv_b, even_w_s=even_w_s,
             even_b_s=even_b_s, even_w_o=even_w_o, c_mu=c_mu, c_w_r=c_w_r, c_w_k=c_w_k,
             c_w_v=c_w_v, c_w_o=c_w_o, c_w0=c_w0, c_w1=c_w1, c_w2=c_w2, c_a0=c_a0, c_a1=c_a1,
             c_a2=c_a2, c_v0=c_v0, c_v1=c_v1, c_v2=c_v2, c_g1=c_g1, c_g2=c_g2, c_k_k=c_k_k,
             c_k_a=c_k_a, c_r_k=c_r_k, c_gn_g=c_gn_g, c_gn_b=c_gn_b, ffn_w_in=ffn_w_in,
             ffn_conv_w=ffn_conv_w, ffn_conv_b=ffn_conv_b, ffn_w_out=ffn_w_out, ln1_g=ln1_g,
             ln1_b=ln1_b, ln2_g=ln2_g, ln2_b=ln2_b, ple_proj=ple_proj, ple_gate=ple_gate)
    for name in _MATRICES:
        W[name] = _Layers(W[name].astype(BF))
    bp = x_prompt.shape[0]
    n_odd, n_even = state_rwkv_wkv.shape[0], cache_sb_k.shape[0]
    db, past = cache_sb_k.shape[1], cache_sb_k.shape[2]
    cache_sb_k = cache_sb_k.reshape(n_even, db, past, B_WIDTH)
    cache_sb_v = cache_sb_v.reshape(n_even, db, past, B_WIDTH)
    wkv_zero = jnp.zeros((n_odd, bp, C_HEADS, C_HEAD_DIM, C_HEAD_DIM), F32)
    shift_zero = jnp.zeros((n_odd, bp, D_MODEL), F32)
    conv_zero = jnp.zeros((DEPTH, bp, CONV_W - 1, D_FF), F32)
    y_p, k_p, v_p, _, wkv_p, shift_p, conv_p = _trunk(
        x_prompt, p_prompt, W, None, wkv_zero, shift_zero, conv_zero)
    y_s, k_s, v_s, va_s, wkv_s, shift_s, conv_s = _trunk(
        x_sample, p_sample, W, (cache_sb_k, cache_sb_v), state_rwkv_wkv, state_rwkv_shift, state_ffn_conv)
    return (y_p, y_s, k_p, v_p, wkv_p, shift_p, conv_p, k_s, v_s, va_s, wkv_s, shift_s, conv_s)
```

```python
import functools

import jax
import jax.numpy as jnp
from jax import lax
from jax.experimental import pallas as pl
from jax.experimental.pallas import tpu as pltpu

F32 = jnp.float32
BF = jnp.bfloat16

D_MODEL = 2048
DEPTH = 4
CHUNK = 64
A_WIDTH = D_MODEL // 2
A_HEADS = 8
A_DIM = A_WIDTH // A_HEADS
A_BLOCK = 128
B_HEAD_DIM = 64
B_WIDTH = D_MODEL // 2
B_HEADS = B_WIDTH // B_HEAD_DIM
C_HEAD_DIM = 64
C_HEADS = D_MODEL // C_HEAD_DIM
C_GN_EPS = 64e-5
D_FF = 5632
CONV_W = 3
LN_EPS = 1e-5
DN_ALPHA = (2 * DEPTH) ** 0.25

MXU_WIDTH = 256
LN_ROW_CHUNK = 256
LANES = 128
SUBLANES = 8
HEAD_PAIR = 2 * C_HEAD_DIM
N_PAIRS = D_MODEL // HEAD_PAIR
WKV_CHUNK = 64
WKV_PAIRS_PER_STEP = 16
SB_BLOCK = 256
SB_SUBTILE = 256
SB_CHAIN_LAG = 0
SB_CUMSUM_PASSES = 1
VMEM_CAP = 56 * 1024 * 1024

NT_DIMS = (((1,), (1,)), ((), ()))
TN_DIMS = (((0,), (0,)), ((), ()))


def _tile(m, pref):
    return pref if m % pref == 0 else m


def _params(sem, est_bytes):
    limit = int(min(max(est_bytes * 5 // 4 + (4 << 20), 16 << 20), VMEM_CAP))
    return pltpu.CompilerParams(dimension_semantics=sem, vmem_limit_bytes=limit)


def _nbytes(shape, dtype):
    n = 1
    for s in shape:
        n *= s
    return n * jnp.dtype(dtype).itemsize


def _layernorm(x, g, b, eps=LN_EPS):
    mu = jnp.mean(x, axis=-1, keepdims=True)
    d = x - mu
    var = jnp.mean(d * d, axis=-1, keepdims=True)
    return d * lax.rsqrt(var + eps) * g + b


def _bdot(a, b):
    return jnp.dot(a.astype(BF), b.astype(BF), preferred_element_type=F32)


def _run_in_lockstep(generators, lag=0):
    results = [None] * len(generators)
    started, live, tick = 0, [], 0
    while started < len(generators) or live:
        while started < len(generators) and tick >= started * lag:
            live.append(started)
            started += 1
        for idx in list(live):
            try:
                next(generators[idx])
            except StopIteration as done:
                results[idx] = done.value
                live.remove(idx)
        tick += 1
    return results


def _mm_body(*refs, nprod, ntile, nrow, nout, nk, epilogue, col_chunk, row_chunk):
    xs = refs[:nprod]
    ws = refs[nprod:2 * nprod]
    ts = refs[2 * nprod:2 * nprod + ntile]
    rs = refs[2 * nprod + ntile:2 * nprod + ntile + nrow]
    outs = refs[2 * nprod + ntile + nrow:2 * nprod + ntile + nrow + nout]
    accs = refs[2 * nprod + ntile + nrow + nout:]

    def finish(vals, cols=slice(None), rws=slice(None)):
        res = epilogue(vals, [t[rws, cols] for t in ts], [r[:, cols] for r in rs])
        for o_ref, o in zip(outs, res):
            o_ref[rws, cols] = o.astype(o_ref.dtype)

    tm, tn = outs[0].shape
    if nk == 1 and row_chunk is not None and tm > row_chunk and tm % row_chunk == 0:
        pending = None
        for ch in range(tm // row_chunk):
            rws = slice(ch * row_chunk, (ch + 1) * row_chunk)
            cur = [jnp.dot(x[rws, :].astype(BF), w[...], preferred_element_type=F32) for x, w in zip(xs, ws)]
            if pending is not None:
                finish(pending[0], rws=pending[1])
            pending = (cur, rws)
        finish(pending[0], rws=pending[1])
        return
    if nk == 1 and col_chunk is not None and tn > col_chunk and tn % col_chunk == 0:
        lhs = [x[...].astype(BF) for x in xs]
        pending = None
        for ch in range(tn // col_chunk):
            cols = slice(ch * col_chunk, (ch + 1) * col_chunk)
            cur = [jnp.dot(a, w[:, cols], preferred_element_type=F32) for a, w in zip(lhs, ws)]
            if pending is not None:
                finish(*pending)
            pending = (cur, cols)
        finish(*pending)
        return

    prods = [jnp.dot(x[...].astype(BF), w[...], preferred_element_type=F32) for x, w in zip(xs, ws)]
    if nk == 1:
        finish(prods)
        return
    k = pl.program_id(2)

    @pl.when(k == 0)
    def _():
        for a, p in zip(accs, prods):
            a[...] = p

    @pl.when(k > 0)
    def _():
        for a, p in zip(accs, prods):
            a[...] += p

    @pl.when(k == nk - 1)
    def _():
        finish([a[...] for a in accs])


def _mm(prods, epilogue, out_dtypes, *, n, tm, tn, tk=None, tiles=(), rows=(), name="mm",
        col_chunk=MXU_WIDTH, row_chunk=None, resident_weights=False):
    prods = [tuple(p) + (0,) * (4 - len(p)) for p in prods]
    m = prods[0][0].shape[0]
    tm = _tile(m, tm)
    ks = [p[0].shape[1] for p in prods]
    nk = 1 if tk is None else ks[0] // tk
    if nk > 1:
        assert all(k == ks[0] for k in ks) and ks[0] % tk == 0
    assert n % tn == 0 and m % tm == 0
    grid = (n // tn, m // tm, nk)
    in_specs, est = [], 0
    for k_dim in ks:
        bk = k_dim if nk == 1 else tk
        in_specs.append(pl.BlockSpec((tm, bk), lambda j, i, k: (i, k)))
    weights = []
    for (x, w, off, row_off), k_dim in zip(prods, ks):
        bk = k_dim if nk == 1 else tk
        assert off % tn == 0 and row_off % bk == 0
        if resident_weights:
            assert nk == 1 and n == tn
        spec, operand = _weight_spec(w, bk, tn, lambda k, rb=row_off // bk: k + rb,
                                     lambda j, ob=off // tn: j + ob, single_buffer=resident_weights)
        in_specs.append(spec)
        weights.append(operand)
        est += 2 * _nbytes((tm, bk), x.dtype) + (1 if resident_weights else 2) * _nbytes((bk, tn), w.dtype)
    for t in tiles:
        in_specs.append(pl.BlockSpec((tm, tn), lambda j, i, k: (i, j)))
        est += 2 * _nbytes((tm, tn), t.dtype)
    for r in rows:
        in_specs.append(pl.BlockSpec((1, tn), lambda j, i, k: (0, j)))
    out_specs = [pl.BlockSpec((tm, tn), lambda j, i, k: (i, j)) for _ in out_dtypes]
    out_shape = [jax.ShapeDtypeStruct((m, n), dt) for dt in out_dtypes]
    est += sum(2 * _nbytes((tm, tn), dt) for dt in out_dtypes)
    est += (len(prods) + 2) * _nbytes((tm, tn), F32)
    scratch = [pltpu.VMEM((tm, tn), F32) for _ in prods] if nk > 1 else []
    body = functools.partial(_mm_body, nprod=len(prods), ntile=len(tiles), nrow=len(rows),
                             nout=len(out_dtypes), nk=nk, epilogue=epilogue, col_chunk=col_chunk,
                             row_chunk=row_chunk)
    return pl.pallas_call(
        body, name=name, grid=grid, in_specs=in_specs, out_specs=out_specs, out_shape=out_shape,
        scratch_shapes=scratch,
        compiler_params=_params(("parallel", "parallel", "arbitrary"), est),
    )(*[p[0] for p in prods], *weights, *tiles, *rows)


def _row(v):
    return v.reshape(1, -1).astype(F32)


class _LayerOf:
    def __init__(self, stack, layer):
        self.stack, self.layer = stack, layer
        self.shape, self.dtype = stack.shape[1:], stack.dtype


class _Layers:
    def __init__(self, stack):
        self.stack = stack

    def __getitem__(self, layer):
        return _LayerOf(self.stack, layer)


def _weight_spec(w, rows, cols, row_block, col_block, single_buffer=False):
    mode = dict(pipeline_mode=pl.Buffered(1)) if single_buffer else {}
    if isinstance(w, _LayerOf):
        return (pl.BlockSpec((None, rows, cols), lambda j, i, k=0: (w.layer, row_block(k), col_block(j)),
                             **mode), w.stack)
    return pl.BlockSpec((rows, cols), lambda j, i, k=0: (row_block(k), col_block(j)), **mode), w


def _gate_body(u_ref, va_ref, ws_ref, bs_ref, o_ref, *, blk, nblk):
    ti = lax.broadcasted_iota(jnp.int32, (blk, blk), 0) // CHUNK
    si = lax.broadcasted_iota(jnp.int32, (blk, blk), 1) // CHUNK
    allowed = ti >= si
    for h in range(A_HEADS):
        w = jnp.where(allowed, ws_ref[h], 0.0).astype(BF)
        bias = bs_ref[h]
        for nb in range(nblk):
            rs = slice(nb * blk, (nb + 1) * blk)
            cs = slice(h * A_DIM, (h + 1) * A_DIM)
            mixed = jnp.dot(w, va_ref[rs, cs], preferred_element_type=F32) + bias
            o_ref[rs, cs] = (u_ref[rs, cs].astype(F32) * mixed).astype(o_ref.dtype)


def _spatial_gate(u, va, w_s, b_s, blk):
    m = u.shape[0]
    tm = _tile(m, 512)
    nblk = tm // blk
    est = 2 * 3 * _nbytes((tm, A_WIDTH), BF) + 2 * _nbytes((A_HEADS, blk, blk), F32)
    return pl.pallas_call(
        functools.partial(_gate_body, blk=blk, nblk=nblk),
        name="spatial_gate",
        grid=(m // tm,),
        in_specs=[pl.BlockSpec((tm, A_WIDTH), lambda i: (i, 0)),
                  pl.BlockSpec((tm, A_WIDTH), lambda i: (i, 0)),
                  pl.BlockSpec((A_HEADS, blk, blk), lambda i: (0, 0, 0)),
                  pl.BlockSpec((A_HEADS, blk, 1), lambda i: (0, 0, 0))],
        out_specs=pl.BlockSpec((tm, A_WIDTH), lambda i: (i, 0)),
        out_shape=jax.ShapeDtypeStruct((m, A_WIDTH), BF),
        compiler_params=_params(("parallel",), est),
    )(u, va, w_s.astype(F32), b_s.astype(F32)[:, :, None])


def _later_keys(nkeys):
    jj = lax.broadcasted_iota(jnp.int32, (nkeys, nkeys), 0)
    ss = lax.broadcasted_iota(jnp.int32, (nkeys, nkeys), 1)
    return jnp.where(jj > ss, 1.0, 0.0).astype(BF)


def _sb_chain(qh, kb, valid, later, carry):
    z = lax.dot_general(qh, kb, NT_DIMS, preferred_element_type=F32)
    yield
    neg_z = -z
    log_fail = jnp.minimum(neg_z, 0.0) - jnp.log(1.0 + jnp.exp(jnp.minimum(z, neg_z)))
    log_hit = z + log_fail
    if valid is not None:
        log_fail = jnp.where(valid, log_fail, 0.0)
    part = log_fail.astype(BF)
    after = jnp.dot(part, later, preferred_element_type=F32)
    rest = log_fail
    for _ in range(SB_CUMSUM_PASSES - 1):
        rest = rest - part.astype(F32)
        part = rest.astype(BF)
        after = after + jnp.dot(part, later, preferred_element_type=F32)
    yield
    total = after[:, 0:1] + log_fail[:, 0:1]
    if carry is None:
        return log_hit + after, total
    w = jnp.exp(log_hit + after + carry)
    if valid is not None:
        w = jnp.where(valid, w, 0.0)
    return w.astype(BF), total


def _sb_cache_body(q_ref, kn_ref, vn_ref, kc_ref, vc_ref, o_ref, c_ref, acc_ref, *, n_steps):
    step = pl.program_id(1)
    t = q_ref.shape[0]
    head_cols = lambda h: slice(h * B_HEAD_DIM, (h + 1) * B_HEAD_DIM)

    def attend(key_of, val_of, nkeys, valid):
        later = _later_keys(nkeys)
        gens = [_sb_chain(q_ref[:, head_cols(h)], key_of(h).astype(BF), valid, later, c_ref[h])
                for h in range(B_HEADS)]
        for h, (w, total) in enumerate(_run_in_lockstep(gens)):
            acc_ref[h] = acc_ref[h] + jnp.dot(w, val_of(h).astype(BF), preferred_element_type=F32)
            c_ref[h] = c_ref[h] + total

    @pl.when(step == 0)
    def _():
        c_ref[...] = jnp.zeros_like(c_ref)
        acc_ref[...] = jnp.zeros_like(acc_ref)
        tt = lax.broadcasted_iota(jnp.int32, (t, t), 0)
        sk = lax.broadcasted_iota(jnp.int32, (t, t), 1)
        attend(lambda h: kn_ref[:, head_cols(h)], lambda h: vn_ref[:, head_cols(h)], t, sk < tt)

    attend(lambda h: kc_ref[:, h, :], lambda h: vc_ref[:, h, :], kc_ref.shape[0], None)

    @pl.when(step == n_steps - 1)
    def _():
        for h in range(B_HEADS):
            o_ref[:, head_cols(h)] = acc_ref[h].astype(o_ref.dtype)


def _stick_breaking_cache(q, k_new, v_new, k_cache, v_cache, layer, tk):
    bsz, t, _ = q.shape
    p_len = k_cache.shape[2]
    n_steps = p_len // tk
    new_spec = pl.BlockSpec((None, t, B_WIDTH), lambda b, s: (b, 0, 0))
    cache_spec = pl.BlockSpec((None, None, tk, B_HEADS, B_HEAD_DIM),
                              lambda b, s: (layer, b, n_steps - 1 - s, 0, 0))
    est = (2 * 2 * _nbytes((tk, B_HEADS, LANES), F32) + 8 * _nbytes((t, B_WIDTH), F32)
           + 12 * B_HEADS * _nbytes((t, tk), F32))
    return pl.pallas_call(
        functools.partial(_sb_cache_body, n_steps=n_steps),
        name="stick_breaking_cache",
        grid=(bsz, n_steps),
        in_specs=[new_spec, new_spec, new_spec, cache_spec, cache_spec],
        out_specs=new_spec,
        out_shape=jax.ShapeDtypeStruct((bsz, t, B_WIDTH), BF),
        scratch_shapes=[pltpu.VMEM((B_HEADS, t, 1), F32), pltpu.VMEM((B_HEADS, t, B_HEAD_DIM), F32)],
        compiler_params=_params(("parallel", "arbitrary"), est),
    )(q, k_new, v_new, k_cache, v_cache)


def _sb_body(q_ref, kd_ref, vd_ref, kp_ref, vp_ref, o_ref, *, sub, nq, tk, blocks_per_iter, past_iters,
             unroll):
    head_a_q = lax.broadcasted_iota(jnp.int32, (sub, LANES), 1) < B_HEAD_DIM
    q_subs = []
    for i in range(nq):
        q = q_ref[i * sub:(i + 1) * sub, :]
        zero_q = jnp.zeros_like(q)
        q_subs.append((jnp.where(head_a_q, q, zero_q), jnp.where(head_a_q, zero_q, q)))

    def attend(k_blocks, v_blocks, masks, carries, accs):
        kbs = [k.astype(BF) for k in k_blocks]
        laters, v_stacks = {}, []
        for kb, v_blk in zip(kbs, v_blocks):
            nkeys = kb.shape[0]
            if nkeys not in laters:
                laters[nkeys] = _later_keys(nkeys)
            vb = v_blk.astype(BF)
            head_a_k = lax.broadcasted_iota(jnp.int32, (nkeys, LANES), 1) < B_HEAD_DIM
            zero_v = jnp.zeros_like(vb)
            v_stacks.append(jnp.concatenate([jnp.where(head_a_k, vb, zero_v), jnp.where(head_a_k, zero_v, vb)],
                                            axis=0))
        early = len(kbs) == 1
        keys, gens = [], []
        for i in range(nq):
            for j, kb in enumerate(kbs):
                if isinstance(masks[i][j], str):
                    continue
                for h in range(2):
                    keys.append((i, j, h))
                    gens.append(_sb_chain(q_subs[i][h], kb, masks[i][j], laters[kb.shape[0]],
                                          carries[i][h] if early else None))
        scores = dict(zip(keys, _run_in_lockstep(gens, lag=SB_CHAIN_LAG)))
        new_carries, new_accs = [], []
        for i in range(nq):
            cs = list(carries[i])
            wts, vals = [], []
            for j in range(len(kbs)):
                if isinstance(masks[i][j], str):
                    continue
                for h in range(2):
                    w, total = scores[(i, j, h)]
                    if not early:
                        w = jnp.exp(w + cs[h])
                        if masks[i][j] is not None:
                            w = jnp.where(masks[i][j], w, 0.0)
                        w = w.astype(BF)
                    cs[h] = cs[h] + total
                    wts.append(w)
                vals.append(v_stacks[j])
            new_carries.append(cs)
            new_accs.append(accs[i] + jnp.dot(jnp.concatenate(wts, axis=1), jnp.concatenate(vals, axis=0),
                                              preferred_element_type=F32))
        return new_carries, new_accs

    k_diag = [kd_ref[j * sub:(j + 1) * sub, :] for j in reversed(range(nq))]
    v_diag = [vd_ref[j * sub:(j + 1) * sub, :] for j in reversed(range(nq))]
    tt = lax.broadcasted_iota(jnp.int32, (sub, sub), 0)
    sk = lax.broadcasted_iota(jnp.int32, (sub, sub), 1)
    causal = sk < tt
    masks = [[causal if j == i else (None if j < i else "skip") for j in reversed(range(nq))]
             for i in range(nq)]
    zeros_c = jnp.zeros((sub, 1), F32)
    carries, accs = attend(k_diag, v_diag, masks, [[zeros_c, zeros_c] for _ in range(nq)],
                           [jnp.zeros((sub, LANES), F32) for _ in range(nq)])
    n_iters = past_iters(pl.program_id(2))
    open_masks = [[None] * blocks_per_iter for _ in range(nq)]

    def step(it, state):
        carries = [list(state[2 * i:2 * i + 2]) for i in range(nq)]
        accs = list(state[2 * nq:])
        newest = n_iters - 1 - it
        offs = [pl.multiple_of((newest * blocks_per_iter + j) * tk, tk)
                for j in reversed(range(blocks_per_iter))]
        carries, accs = attend([kp_ref[pl.ds(o, tk), :] for o in offs], [vp_ref[pl.ds(o, tk), :] for o in offs],
                               open_masks, carries, accs)
        return tuple(c for pair in carries for c in pair) + tuple(accs)

    state = lax.fori_loop(0, n_iters, step, tuple(c for pair in carries for c in pair) + tuple(accs),
                          unroll=unroll)
    for i in range(nq):
        o_ref[i * sub:(i + 1) * sub, :] = state[2 * nq + i].astype(o_ref.dtype)


def _stick_breaking(q, k_new, v_new, k_past, v_past, *, tq, tk, causal_past, past_layer=None):
    bsz, t, _ = q.shape
    p_len = k_past.shape[-2]
    if past_layer is None:
        past_spec = pl.BlockSpec((None, p_len, LANES), lambda b, p, i: (b, 0, p))
    else:
        past_spec = pl.BlockSpec((None, None, p_len, LANES), lambda b, p, i: (past_layer, b, 0, p))
    sub = min(tq, SB_SUBTILE)
    nq = tq // sub
    if causal_past:
        blocks_per_iter, unroll = 1, None
        past_iters = lambda qi: qi * (tq // tk)
    else:
        n_blocks = p_len // tk
        blocks_per_iter, unroll = (2 if n_blocks % 2 == 0 else 1), True
        past_iters = lambda qi: n_blocks // blocks_per_iter
    est = (2 * 2 * _nbytes((p_len, LANES), k_past.dtype) + 2 * 4 * _nbytes((tq, LANES), F32)
           + 12 * 2 * nq * blocks_per_iter * _nbytes((sub, max(sub, tk)), F32))
    return pl.pallas_call(
        functools.partial(_sb_body, sub=sub, nq=nq, tk=tk, blocks_per_iter=blocks_per_iter,
                          past_iters=past_iters, unroll=unroll),
        name="stick_breaking_past" if causal_past else "stick_breaking_cache",
        grid=(bsz, B_WIDTH // LANES, t // tq),
        in_specs=[pl.BlockSpec((None, tq, LANES), lambda b, p, i: (b, i, p)),
                  pl.BlockSpec((None, tq, LANES), lambda b, p, i: (b, i, p)),
                  pl.BlockSpec((None, tq, LANES), lambda b, p, i: (b, i, p)),
                  past_spec, past_spec],
        out_specs=pl.BlockSpec((None, tq, LANES), lambda b, p, i: (b, i, p)),
        out_shape=jax.ShapeDtypeStruct((bsz, t, B_WIDTH), BF),
        compiler_params=_params(("parallel", "parallel", "arbitrary"), est),
    )(q, k_new, v_new, k_past, v_past)


def _shift_body(x_ref, last_ref, mu_ref, *rest, tiles_per_seq):
    outs, carry = rest[:6], rest[6]
    i = pl.program_id(1)

    @pl.when(i % tiles_per_seq == 0)
    def _():
        carry[...] = jnp.broadcast_to(last_ref[...], carry.shape)

    x = x_ref[...]
    rows = lax.broadcasted_iota(jnp.int32, x.shape, 0)
    x_prev = jnp.where(rows == 0, carry[SUBLANES - 1:SUBLANES, :], pltpu.roll(x, 1, 0))
    carry[...] = x[x.shape[0] - SUBLANES:, :]
    xx = x_prev - x
    for j, o_ref in enumerate(outs):
        o_ref[...] = (x + xx * mu_ref[j:j + 1, :]).astype(o_ref.dtype)


def _token_shift(x, x_last, mu):
    bsz, t, d = x.shape
    tm = _tile(t, 256)
    est = 2 * _nbytes((tm, d), F32) + 12 * _nbytes((tm, d), BF) + 4 * _nbytes((tm, d), F32)
    return pl.pallas_call(
        functools.partial(_shift_body, tiles_per_seq=t // tm),
        name="token_shift",
        grid=(bsz, t // tm),
        in_specs=[pl.BlockSpec((None, tm, d), lambda b, i: (b, i, 0)),
                  pl.BlockSpec((None, 1, d), lambda b, i: (b, 0, 0)),
                  pl.BlockSpec((6, d), lambda b, i: (0, 0))],
        out_specs=[pl.BlockSpec((None, tm, d), lambda b, i: (b, i, 0)) for _ in range(6)],
        out_shape=[jax.ShapeDtypeStruct((bsz, t, d), BF) for _ in range(6)],
        scratch_shapes=[pltpu.VMEM((SUBLANES, d), F32)],
        compiler_params=_params(("parallel", "arbitrary"), est),
    )(x, x_last[:, None, :].astype(F32), mu.astype(F32))


def _wkv_body(r_ref, k_ref, v_ref, lw_ref, a_ref, g_ref, kk_ref, ka_ref, rk_ref, gg_ref, gb_ref,
              s0_ref, y_ref, s_ref):
    c = WKV_CHUNK
    n = 2 * c

    @pl.when(pl.program_id(1) == 0)
    def _():
        s_ref[...] = s0_ref[...]

    head_a = lax.broadcasted_iota(jnp.int32, (c, LANES), 1) < C_HEAD_DIM
    ri = lax.broadcasted_iota(jnp.int32, (n, n), 0)
    ci = lax.broadcasted_iota(jnp.int32, (n, n), 1)
    same_head = (ri // c) == (ci // c)
    strict = same_head & (ci < ri)
    inclusive = same_head & (ci <= ri)
    diag16 = (ri // 16) == (ci // 16)
    diag32 = (ri // 32) == (ci // 32)
    eye = jnp.where(ri == ci, 1.0, 0.0).astype(F32)
    seg_ones = jnp.where(same_head, 1.0, 0.0).astype(BF)
    ti = lax.broadcasted_iota(jnp.int32, (c, c), 0)
    si = lax.broadcasted_iota(jnp.int32, (c, c), 1)
    prefix = jnp.where(si <= ti, 1.0, 0.0).astype(BF)

    def split2(x):
        hi = x.astype(BF)
        return hi, (x - hi.astype(F32)).astype(BF)

    def seg_sum(x):
        hi, lo = split2(x)
        return (jnp.dot(hi, seg_ones, preferred_element_type=F32)
                + jnp.dot(lo, seg_ones, preferred_element_type=F32))

    def cumsum_time(x):
        hi, rest = split2(x)
        rest_f = x - hi.astype(F32)
        mid = rest_f.astype(BF)
        lo = (rest_f - mid.astype(F32)).astype(BF)
        return (jnp.dot(prefix, hi, preferred_element_type=F32)
                + jnp.dot(prefix, mid, preferred_element_type=F32)
                + jnp.dot(prefix, lo, preferred_element_type=F32))

    def stack(x):
        zero = jnp.zeros_like(x)
        return jnp.concatenate([jnp.where(head_a, x, zero), jnp.where(head_a, zero, x)], axis=0)

    def pair(p, s):
        sl = pl.ds(pl.multiple_of(p * LANES, LANES), LANES)
        r = r_ref[:, sl]
        k = k_ref[:, sl]
        v = v_ref[:, sl]
        lw = lw_ref[:, sl]
        a = a_ref[:, sl]
        kkp = k * kk_ref[:, sl]
        norm = jnp.sqrt(seg_sum(kkp * kkp))
        yield
        kk = kkp / jnp.maximum(norm, 1e-12)
        k2 = k * (1.0 + (a - 1.0) * ka_ref[:, sl])
        bonus = seg_sum(r * k2 * rk_ref[:, sl]) * v
        yield
        cs = cumsum_time(lw)
        yield
        e_pos = jnp.exp(cs)
        e_neg = jnp.exp(-cs)
        a_t = -kk * e_pos * jnp.exp(-lw)
        b_t = kk * a * e_neg
        k_t = k2 * e_neg
        r_t = r * e_pos
        la, lr = stack(a_t), stack(r_t)
        rb, rk = stack(b_t), stack(k_t)
        vs = stack(v)
        q = lax.dot_general(jnp.concatenate([la, lr], axis=0).astype(BF),
                            jnp.concatenate([rb, rk], axis=0).astype(BF),
                            NT_DIMS, preferred_element_type=F32)
        yield
        p_ab = jnp.where(strict, q[:n, :n], 0.0)
        a_ak = jnp.where(strict, q[:n, n:], 0.0)
        a_rb = jnp.where(inclusive, q[n:, :n], 0.0)
        a_rk = jnp.where(inclusive, q[n:, n:], 0.0)
        akv = _bdot(a_ak, vs)
        yield
        pd = jnp.where(diag16, p_ab, 0.0)
        inv = eye + pd
        pw = pd
        for _ in range(3):
            pw = _bdot(pw, pw)
            yield
            inv = inv + _bdot(pw, inv)
            yield
        for off_diag in (jnp.where(diag32 & (~diag16), p_ab, 0.0), jnp.where(diag32, 0.0, p_ab)):
            right = _bdot(off_diag, inv)
            yield
            inv = inv + _bdot(inv, right)
            yield
        wu = _bdot(inv, jnp.concatenate([la, akv], axis=1))
        yield
        w_m, u_m = wu[:, :LANES], wu[:, LANES:]
        wr = lax.dot_general(jnp.concatenate([w_m, lr], axis=0).astype(BF), s.astype(BF),
                             NT_DIMS, preferred_element_type=F32)
        yield
        sa = wr[:n] + u_m
        sv = jnp.concatenate([sa, vs], axis=0)
        y_st = wr[n:] + _bdot(jnp.concatenate([a_rb, a_rk], axis=1), sv)
        g_end = e_pos[c - 1:c, :]
        rg = jnp.concatenate([rb, rk], axis=0) * g_end
        s_new = s * g_end + lax.dot_general(sv.astype(BF), rg.astype(BF), TN_DIMS,
                                            preferred_element_type=F32)
        yield
        y = y_st[:c] + y_st[c:]
        mean = seg_sum(y) * (1.0 / C_HEAD_DIM)
        yield
        d = y - mean
        var = seg_sum(d * d) * (1.0 / C_HEAD_DIM)
        yield
        yn = d * lax.rsqrt(var + C_GN_EPS) * gg_ref[:, sl] + gb_ref[:, sl]
        return s_new, ((yn + bonus) * g_ref[:, sl]).astype(y_ref.dtype)

    def group(it, carry):
        pairs = [it * WKV_PAIRS_PER_STEP + u for u in range(WKV_PAIRS_PER_STEP)]
        states = [s_ref[p] for p in pairs]
        results = _run_in_lockstep([pair(p, s) for p, s in zip(pairs, states)])
        for p, (s_new, y_out) in zip(pairs, results):
            s_ref[p] = s_new
            y_ref[:, pl.ds(pl.multiple_of(p * LANES, LANES), LANES)] = y_out
        return carry

    lax.fori_loop(0, N_PAIRS // WKV_PAIRS_PER_STEP, group, 0)


def _pair_state(s):
    bsz = s.shape[0]
    s = s.reshape(bsz, N_PAIRS, 2, C_HEAD_DIM, C_HEAD_DIM).astype(F32)
    z = jnp.zeros_like(s[:, :, 0])
    top = jnp.concatenate([s[:, :, 0], z], axis=-1)
    bot = jnp.concatenate([z, s[:, :, 1]], axis=-1)
    return jnp.concatenate([top, bot], axis=-2)


def _unpair_state(sp):
    bsz = sp.shape[0]
    a = sp[:, :, :C_HEAD_DIM, :C_HEAD_DIM]
    b = sp[:, :, C_HEAD_DIM:, C_HEAD_DIM:]
    return jnp.stack([a, b], axis=2).reshape(bsz, C_HEADS, C_HEAD_DIM, C_HEAD_DIM)


def _wkv(r, k, v, lw, a, g, s0, k_k, k_a, r_k, gn_g, gn_b):
    bsz, t, d = r.shape
    c = WKV_CHUNK
    seq = pl.BlockSpec((None, c, d), lambda b, i: (b, i, 0))
    vec = pl.BlockSpec((1, d), lambda b, i: (0, 0))
    st = pl.BlockSpec((None, N_PAIRS, HEAD_PAIR, HEAD_PAIR), lambda b, i: (b, 0, 0, 0))
    est = 2 * 6 * _nbytes((c, d), F32) + 4 * _nbytes((N_PAIRS, HEAD_PAIR, HEAD_PAIR), F32) + (8 << 20)
    y, s_new = pl.pallas_call(
        _wkv_body,
        name="wkv7_chunked",
        grid=(bsz, t // c),
        in_specs=[seq] * 6 + [vec] * 5 + [st],
        out_specs=[seq, st],
        out_shape=[jax.ShapeDtypeStruct((bsz, t, d), BF),
                   jax.ShapeDtypeStruct((bsz, N_PAIRS, HEAD_PAIR, HEAD_PAIR), F32)],
        compiler_params=_params(("parallel", "arbitrary"), est),
    )(r, k, v, lw, a, g, _row(k_k), _row(k_a), _row(r_k), _row(gn_g), _row(gn_b), _pair_state(s0))
    return y, _unpair_state(s_new)


def _ffn_in_body(x_ref, wg_ref, wu_ref, cw_ref, cb_ref, prev_ref, act_ref, st_ref, tail_ref, *,
                 seq_len, tm):
    x = x_ref[...]
    tn = act_ref.shape[1]
    cn = min(tn, MXU_WIDTH)
    last2, last1 = SUBLANES - 2, SUBLANES - 1
    carried = seq_len >= tm
    if carried:
        @pl.when(pl.program_id(1) % (seq_len // tm) == 0)
        def _():
            tail_ref[...] = prev_ref[0]

    def conv_gate(hg, hu, cols):
        rows = lax.broadcasted_iota(jnp.int32, hg.shape, 0)
        if carried:
            tail = tail_ref[:, cols]
            p0, p1 = tail[last2:last2 + 1, :], tail[last1:last1 + 1, :]
            pos = rows
            new_tail = hg[tm - SUBLANES:, :]
            tail_ref[:, cols] = new_tail
            st_ref[0, :, cols] = new_tail
        else:
            nseq = tm // seq_len
            prev = prev_ref[:, :, cols]
            spread = lambda rw: jnp.broadcast_to(rw, (nseq, seq_len, cn)).reshape(tm, cn)
            p0, p1 = spread(prev[:, last2:last2 + 1, :]), spread(prev[:, last1:last1 + 1, :])
            pos = rows % seq_len
            st_ref[:, :, cols] = hg.reshape(nseq, seq_len, cn)[:, seq_len - SUBLANES:, :]
        s1 = jnp.where(pos == 0, p1, pltpu.roll(hg, 1, 0))
        s2 = jnp.where(pos == 0, p0, jnp.where(pos == 1, p1, pltpu.roll(hg, 2, 0)))
        cw = cw_ref[:, cols]
        hc = cb_ref[:, cols] + s2 * cw[0:1, :] + s1 * cw[1:2, :] + hg * cw[2:3, :]
        act_ref[:, cols] = (jax.nn.gelu(hc) * hu).astype(act_ref.dtype)

    pending = None
    for ch in range(tn // cn):
        cols = slice(ch * cn, (ch + 1) * cn)
        hg = jnp.dot(x, wg_ref[:, cols], preferred_element_type=F32)
        hu = jnp.dot(x, wu_ref[:, cols], preferred_element_type=F32)
        if pending is not None:
            conv_gate(*pending)
        pending = (hg, hu, cols)
    conv_gate(*pending)


def _ffn_in(xb, w_in, conv_w, conv_b, conv_prev, seq_len):
    m, d = xb.shape
    nseq_total = m // seq_len
    tn = 512
    tm = _tile(m, 1024)
    nj = D_FF // tn
    prev8 = jnp.pad(conv_prev.astype(F32), ((0, 0), (SUBLANES - (CONV_W - 1), 0), (0, 0)))
    if seq_len >= tm:
        per = seq_len // tm
        nb = 1
        st_map = lambda j, i: (i // per, 0, j)
    else:
        nb = tm // seq_len
        st_map = lambda j, i: (i, 0, j)
    st_spec = pl.BlockSpec((nb, SUBLANES, tn), st_map)
    est = (2 * _nbytes((tm, d), BF) + 4 * _nbytes((d, tn), BF) + 2 * _nbytes((tm, tn), BF)
           + 8 * _nbytes((tm, tn), F32))
    gate_spec, w_operand = _weight_spec(w_in, d, tn, lambda k: 0, lambda j: j)
    up_spec, _ = _weight_spec(w_in, d, tn, lambda k: 0, lambda j: j + nj)
    act, tails = pl.pallas_call(
        functools.partial(_ffn_in_body, seq_len=seq_len, tm=tm),
        name="ffn_in_conv",
        grid=(nj, m // tm),
        in_specs=[pl.BlockSpec((tm, d), lambda j, i: (i, 0)),
                  gate_spec,
                  up_spec,
                  pl.BlockSpec((CONV_W, tn), lambda j, i: (0, j)),
                  pl.BlockSpec((1, tn), lambda j, i: (0, j)),
                  st_spec],
        out_specs=[pl.BlockSpec((tm, tn), lambda j, i: (i, j)), st_spec],
        out_shape=[jax.ShapeDtypeStruct((m, D_FF), BF),
                   jax.ShapeDtypeStruct((nseq_total, SUBLANES, D_FF), F32)],
        scratch_shapes=[pltpu.VMEM((SUBLANES, tn), F32)],
        compiler_params=_params(("parallel", "arbitrary"), est),
    )(xb, w_operand, w_operand, conv_w.astype(F32), _row(conv_b), prev8)
    return act, tails[:, SUBLANES - (CONV_W - 1):, :]


def _ln_residual_epilogue(accs, tiles, rows):
    mix = accs[0]
    for extra in accs[1:]:
        mix = mix + extra
    y = _layernorm(DN_ALPHA * tiles[0] + mix, rows[0], rows[1])
    return y, y


def _identity_epilogue(accs, tiles, rows):
    return (accs[0],)


def _even_mixer(x, xb, seq_len, w_in, lnv_g, lnv_b, w_s, b_s, w_o, ln_g, ln_b, cache):
    m = x.shape[0]
    bsz = m // seq_len
    proj = lambda name, off, ep, dts, rows=(), col_chunk=MXU_WIDTH: _mm(
        [(xb, w_in, off)], ep, dts, n=1024, tm=512, tn=1024, rows=rows, name=name, col_chunk=col_chunk)
    (u,) = proj("even_u", 0, lambda a, t, r: (jax.nn.gelu(a[0]),), [BF])

    def va_epilogue(a, t, r):
        va = _layernorm(jax.nn.gelu(a[0]), r[0], r[1])
        return va, va
    va, vab = proj("even_va", A_WIDTH, va_epilogue, [F32, BF], rows=(_row(lnv_g), _row(lnv_b)),
                   col_chunk=None)
    o = 2 * A_WIDTH
    (q,) = proj("even_q", o, lambda a, t, r: (a[0] * (B_HEAD_DIM ** -0.5),), [BF])
    (k,) = proj("even_k", o + B_WIDTH, _identity_epilogue, [F32])
    (v,) = proj("even_v", o + 2 * B_WIDTH, _identity_epilogue, [F32])

    blk = min(seq_len, A_BLOCK)
    a_out = _spatial_gate(u, vab, w_s[:, :blk, :blk], b_s[:, :blk], blk)

    q3 = q.reshape(bsz, seq_len, B_WIDTH)
    k3 = k.reshape(bsz, seq_len, B_WIDTH)
    v3 = v.reshape(bsz, seq_len, B_WIDTH)
    if cache is None:
        tq = _tile(seq_len, 2 * SB_BLOCK)
        b_out = _stick_breaking(q3, k3, v3, k3, v3, tq=tq, tk=min(tq, SB_BLOCK), causal_past=True)
    else:
        k_cache, v_cache, layer = cache
        b_out = _stick_breaking_cache(q3, k3, v3, k_cache, v_cache, layer, _tile(k_cache.shape[2], SB_BLOCK))
    b_out = b_out.reshape(m, B_WIDTH)

    x1, x1b = _mm([(a_out, w_o, 0), (b_out, w_o, 0, A_WIDTH)], _ln_residual_epilogue, [F32, BF],
                  n=D_MODEL, tm=2 * LN_ROW_CHUNK, tn=D_MODEL, tiles=(x,), rows=(_row(ln_g), _row(ln_b)),
                  name="even_out_ln", col_chunk=None, row_chunk=LN_ROW_CHUNK, resident_weights=True)
    return x1, x1b, k, v, va


def _rwkv_mixer(x, seq_len, x_last, s0, v_first, P, ln_g, ln_b):
    m = x.shape[0]
    bsz = m // seq_len
    mixes = _token_shift(x.reshape(bsz, seq_len, D_MODEL), x_last, P['mu'])
    xr, xw, xk, xv, xa, xg = [a.reshape(m, D_MODEL) for a in mixes]
    big = lambda xin, w: _mm([(xin, w, 0)], _identity_epilogue, [F32], n=D_MODEL, tm=512, tn=1024,
                             name="rwkv_proj")[0]
    r = big(xr, P['w_r'])
    k = big(xk, P['w_k'])

    def lora_in(xin, w, act):
        n = w.shape[1]
        return _mm([(xin, w, 0)], lambda a, t, r_: (act(a[0]),), [BF], n=n, tm=512, tn=n,
                   name="rwkv_lora_in")[0]

    def lora_out(h, w, bias, act):
        return _mm([(h, w, 0)], lambda a, t, r_: (act(r_[0] + a[0]),), [F32], n=D_MODEL, tm=512,
                   tn=1024, rows=(_row(bias),), name="rwkv_lora_out")[0]

    hw = lora_in(xw, P['w1'], jnp.tanh)
    lw = lora_out(hw, P['w2'], P['w0'],
                  lambda pre: -jnp.exp(-(jnp.maximum(-pre, 0.0) + jnp.log(1.0 + jnp.exp(-jnp.abs(pre)))) - 0.5))
    ha = lora_in(xa, P['a1'], lambda z: z)
    a = lora_out(ha, P['a2'], P['a0'], jax.nn.sigmoid)
    hg = lora_in(xg, P['g1'], jax.nn.sigmoid)
    g = _mm([(hg, P['g2'], 0)], _identity_epilogue, [F32], n=D_MODEL, tm=512, tn=1024, name="rwkv_gate")[0]
    if P['v1'] is None:
        v = big(xv, P['w_v'])
        v_first = v
    else:
        hv = lora_in(xv, P['v1'], lambda z: z)

        def v_epilogue(accs, tiles, rows):
            v_new = accs[0]
            return (v_new + (tiles[0] - v_new) * jax.nn.sigmoid(rows[0] + accs[1]),)
        v = _mm([(xv, P['w_v'], 0), (hv, P['v2'], 0)], v_epilogue, [F32], n=D_MODEL, tm=512, tn=1024,
                tiles=(v_first,), rows=(_row(P['v0']),), name="rwkv_v_residual")[0]
    to3 = lambda z: z.reshape(bsz, seq_len, D_MODEL)
    yg, s_new = _wkv(to3(r), to3(k), to3(v), to3(lw), to3(a), to3(g), s0,
                     P['k_k'], P['k_a'], P['r_k'], P['gn_g'], P['gn_b'])
    x1, x1b = _mm([(yg.reshape(m, D_MODEL), P['w_o'], 0)], _ln_residual_epilogue, [F32, BF],
                  n=D_MODEL, tm=2 * LN_ROW_CHUNK, tn=D_MODEL, tiles=(x,), rows=(_row(ln_g), _row(ln_b)),
                  name="rwkv_out_ln", col_chunk=None, row_chunk=LN_ROW_CHUNK, resident_weights=True)
    return x1, x1b, s_new, v_first


def _ffn_ple(x1, x1b, seq_len, p, conv_prev, w_in, conv_w, conv_b, w_out, ln_g, ln_b, ple_gate, ple_proj):
    act, conv_rows = _ffn_in(x1b, w_in, conv_w, conv_b, conv_prev, seq_len)
    x2, x2b = _mm([(act, w_out, 0)], _ln_residual_epilogue, [F32, BF], n=D_MODEL, tm=512, tn=D_MODEL,
                  tk=D_FF // 4, tiles=(x1,), rows=(_row(ln_g), _row(ln_b)), name="ffn_out_ln", col_chunk=None)

    def ple_epilogue(accs, tiles, rows):
        y = tiles[0] + jax.nn.sigmoid(accs[0]) * accs[1]
        return y, y
    x3, x3b = _mm([(x2b, ple_gate, 0), (p, ple_proj, 0)], ple_epilogue, [F32, BF], n=D_MODEL, tm=512,
                  tn=1024, tiles=(x2,), name="ple")
    return x3, x3b, conv_rows


def _trunk(x, p, W, caches, wkv0, shift0, conv0):
    bsz, seq_len, _ = x.shape
    m = bsz * seq_len
    x = x.reshape(m, D_MODEL).astype(F32)
    xb = x.astype(BF)
    new_k, new_v, new_va, new_wkv, new_shift, new_conv = [], [], [], [], [], []
    v_first = None
    for i in range(DEPTH):
        j = i // 2
        if i % 2 == 0:
            cache = None if caches is None else (caches[0], caches[1], j)
            x1, x1b, k, v, va = _even_mixer(
                x, xb, seq_len, W['even_w_in'][j], W['even_lnv_g'][j], W['even_lnv_b'][j],
                W['even_w_s'][j], W['even_b_s'][j], W['even_w_o'][j], W['ln1_g'][i], W['ln1_b'][i], cache)
            new_k.append(k.reshape(bsz, seq_len, B_HEADS, B_HEAD_DIM))
            new_v.append(v.reshape(bsz, seq_len, B_HEADS, B_HEAD_DIM))
            new_va.append(va.reshape(bsz, seq_len, A_HEADS, A_DIM))
        else:
            P = dict(mu=W['c_mu'][j], w_r=W['c_w_r'][j], w_k=W['c_w_k'][j], w_v=W['c_w_v'][j],
                     w_o=W['c_w_o'][j], w0=W['c_w0'][j], w1=W['c_w1'][j], w2=W['c_w2'][j],
                     a0=W['c_a0'][j], a1=W['c_a1'][j], a2=W['c_a2'][j], g1=W['c_g1'][j], g2=W['c_g2'][j],
                     k_k=W['c_k_k'][j], k_a=W['c_k_a'][j], r_k=W['c_r_k'][j], gn_g=W['c_gn_g'][j],
                     gn_b=W['c_gn_b'][j],
                     v0=None if j == 0 else W['c_v0'][j - 1], v1=None if j == 0 else W['c_v1'][j - 1],
                     v2=None if j == 0 else W['c_v2'][j - 1])
            x1, x1b, s_new, v_first = _rwkv_mixer(x, seq_len, shift0[j], wkv0[j], v_first, P,
                                                  W['ln1_g'][i], W['ln1_b'][i])
            new_wkv.append(s_new)
            new_shift.append(x.reshape(bsz, seq_len, D_MODEL)[:, -1])
        x, xb, conv_rows = _ffn_ple(
            x1, x1b, seq_len, p[i].reshape(m, -1), conv0[i], W['ffn_w_in'][i], W['ffn_conv_w'][i],
            W['ffn_conv_b'][i], W['ffn_w_out'][i], W['ln2_g'][i], W['ln2_b'][i], W['ple_gate'][i],
            W['ple_proj'][i])
        new_conv.append(conv_rows)
    st = lambda l: jnp.stack(l) if l else None
    return (x.reshape(bsz, seq_len, D_MODEL), st(new_k), st(new_v), st(new_va), st(new_wkv),
            st(new_shift), st(new_conv))


_MATRICES = ('even_w_in', 'even_w_o', 'c_w_r', 'c_w_k', 'c_w_v', 'c_w_o', 'c_w1', 'c_w2', 'c_a1', 'c_a2',
             'c_v1', 'c_v2', 'c_g1', 'c_g2', 'ffn_w_in', 'ffn_w_out', 'ple_proj', 'ple_gate')


def kernel(x_prompt, x_sample, cache_sb_k, cache_sb_v, state_rwkv_wkv, state_rwkv_shift, state_ffn_conv, p_prompt, p_sample, even_w_in, even_lnv_g, even_lnv_b, even_w_s, even_b_s, even_w_o, c_mu, c_w_r, c_w_k, c_w_v, c_w_o, c_w0, c_w1, c_w2, c_a0, c_a1, c_a2, c_v0, c_v1, c_v2, c_g1, c_g2, c_k_k, c_k_a, c_r_k, c_gn_g, c_gn_b, ffn_w_in, ffn_conv_w, ffn_conv_b, ffn_w_out, ln1_g, ln1_b, ln2_g, ln2_b, ple_proj, ple_gate):
    W = dict(even_w_in=even_w_in, even_lnv_g=even_lnv_g, even_lnv_b=even_lnv_b, even_w_s=even_w_s,
             even_b_s=even_b_s, even_w_o=even_w_o, c_mu=c_mu, c_w_r=c_w_r, c_w_k=c_w_k,
             c_w_v=c_w_v, c_w_o=c_w_o, c_w0=c_w0, c_w1=c_w1, c_w2=c_w2, c_a0=c_a0, c_a1=c_a1,
             c_a2=c_a2, c_v0=c_v0, c_v1=c_v1, c_v2=c_v2, c_g1=c_g1, c_g2=c_g2, c_k_k=c_k_k,
             c_k_a=c_k_a, c_r_k=c_r_k, c_gn_g=c_gn_g, c_gn_b=c_gn_b, ffn_w_in=ffn_w_in,
             ffn_conv_w=ffn_conv_w, ffn_conv_b=ffn_conv_b, ffn_w_out=ffn_w_out, ln1_g=ln1_g,
             ln1_b=ln1_b, ln2_g=ln2_g, ln2_b=ln2_b, ple_proj=ple_proj, ple_gate=ple_gate)
    for name in _MATRICES:
        W[name] = _Layers(W[name].astype(BF))
    bp = x_prompt.shape[0]
    n_odd, n_even = state_rwkv_wkv.shape[0], cache_sb_k.shape[0]
    wkv_zero = jnp.zeros((n_odd, bp, C_HEADS, C_HEAD_DIM, C_HEAD_DIM), F32)
    shift_zero = jnp.zeros((n_odd, bp, D_MODEL), F32)
    conv_zero = jnp.zeros((DEPTH, bp, CONV_W - 1, D_FF), F32)
    y_p, k_p, v_p, _, wkv_p, shift_p, conv_p = _trunk(
        x_prompt, p_prompt, W, None, wkv_zero, shift_zero, conv_zero)
    y_s, k_s, v_s, va_s, wkv_s, shift_s, conv_s = _trunk(
        x_sample, p_sample, W, (cache_sb_k, cache_sb_v), state_rwkv_wkv, state_rwkv_shift, state_ffn_conv)
    return (y_p, y_s, k_p, v_p, wkv_p, shift_p, conv_p, k_s, v_s, va_s, wkv_s, shift_s, conv_s)
```

```python
import functools

import jax
import jax.numpy as jnp
from jax import lax
from jax.experimental import pallas as pl
from jax.experimental.pallas import tpu as pltpu

F32 = jnp.float32
BF = jnp.bfloat16

D_MODEL = 2048
DEPTH = 4
CHUNK = 64
A_WIDTH = D_MODEL // 2
A_HEADS = 8
A_DIM = A_WIDTH // A_HEADS
A_BLOCK = 128
B_HEAD_DIM = 64
B_WIDTH = D_MODEL // 2
B_HEADS = B_WIDTH // B_HEAD_DIM
C_HEAD_DIM = 64
C_HEADS = D_MODEL // C_HEAD_DIM
C_GN_EPS = 64e-5
D_FF = 5632
CONV_W = 3
LN_EPS = 1e-5
DN_ALPHA = (2 * DEPTH) ** 0.25

MXU_WIDTH = 256
LN_ROW_CHUNK = 256
LANES = 128
SUBLANES = 8
HEAD_PAIR = 2 * C_HEAD_DIM
N_PAIRS = D_MODEL // HEAD_PAIR
WKV_CHUNK = 64
WKV_PAIRS_PER_STEP = 16
SB_BLOCK = 256
SB_SUBTILE = 256
SB_CHAIN_LAG = 0
SB_CUMSUM_PASSES = 1
VMEM_CAP = 56 * 1024 * 1024

NT_DIMS = (((1,), (1,)), ((), ()))
TN_DIMS = (((0,), (0,)), ((), ()))


def _tile(m, pref):
    return pref if m % pref == 0 else m


def _params(sem, est_bytes):
    limit = int(min(max(est_bytes * 5 // 4 + (4 << 20), 16 << 20), VMEM_CAP))
    return pltpu.CompilerParams(dimension_semantics=sem, vmem_limit_bytes=limit)


def _nbytes(shape, dtype):
    n = 1
    for s in shape:
        n *= s
    return n * jnp.dtype(dtype).itemsize


def _layernorm(x, g, b, eps=LN_EPS):
    mu = jnp.mean(x, axis=-1, keepdims=True)
    d = x - mu
    var = jnp.mean(d * d, axis=-1, keepdims=True)
    return d * lax.rsqrt(var + eps) * g + b


def _bdot(a, b):
    return jnp.dot(a.astype(BF), b.astype(BF), preferred_element_type=F32)


def _run_in_lockstep(generators, lag=0):
    results = [None] * len(generators)
    started, live, tick = 0, [], 0
    while started < len(generators) or live:
        while started < len(generators) and tick >= started * lag:
            live.append(started)
            started += 1
        for idx in list(live):
            try:
                next(generators[idx])
            except StopIteration as done:
                results[idx] = done.value
                live.remove(idx)
        tick += 1
    return results


def _mm_body(*refs, nprod, ntile, nrow, nout, nk, epilogue, col_chunk, row_chunk):
    xs = refs[:nprod]
    ws = refs[nprod:2 * nprod]
    ts = refs[2 * nprod:2 * nprod + ntile]
    rs = refs[2 * nprod + ntile:2 * nprod + ntile + nrow]
    outs = refs[2 * nprod + ntile + nrow:2 * nprod + ntile + nrow + nout]
    accs = refs[2 * nprod + ntile + nrow + nout:]

    def finish(vals, cols=slice(None), rws=slice(None)):
        res = epilogue(vals, [t[rws, cols] for t in ts], [r[:, cols] for r in rs])
        for o_ref, o in zip(outs, res):
            o_ref[rws, cols] = o.astype(o_ref.dtype)

    tm, tn = outs[0].shape
    if nk == 1 and row_chunk is not None and tm > row_chunk and tm % row_chunk == 0:
        pending = None
        for ch in range(tm // row_chunk):
            rws = slice(ch * row_chunk, (ch + 1) * row_chunk)
            cur = [jnp.dot(x[rws, :].astype(BF), w[...], preferred_element_type=F32) for x, w in zip(xs, ws)]
            if pending is not None:
                finish(pending[0], rws=pending[1])
            pending = (cur, rws)
        finish(pending[0], rws=pending[1])
        return
    if nk == 1 and col_chunk is not None and tn > col_chunk and tn % col_chunk == 0:
        lhs = [x[...].astype(BF) for x in xs]
        pending = None
        for ch in range(tn // col_chunk):
            cols = slice(ch * col_chunk, (ch + 1) * col_chunk)
            cur = [jnp.dot(a, w[:, cols], preferred_element_type=F32) for a, w in zip(lhs, ws)]
            if pending is not None:
                finish(*pending)
            pending = (cur, cols)
        finish(*pending)
        return

    prods = [jnp.dot(x[...].astype(BF), w[...], preferred_element_type=F32) for x, w in zip(xs, ws)]
    if nk == 1:
        finish(prods)
        return
    k = pl.program_id(2)

    @pl.when(k == 0)
    def _():
        for a, p in zip(accs, prods):
            a[...] = p

    @pl.when(k > 0)
    def _():
        for a, p in zip(accs, prods):
            a[...] += p

    @pl.when(k == nk - 1)
    def _():
        finish([a[...] for a in accs])


def _mm(prods, epilogue, out_dtypes, *, n, tm, tn, tk=None, tiles=(), rows=(), name="mm",
        col_chunk=MXU_WIDTH, row_chunk=None, resident_weights=False):
    prods = [tuple(p) + (0,) * (4 - len(p)) for p in prods]
    m = prods[0][0].shape[0]
    tm = _tile(m, tm)
    ks = [p[0].shape[1] for p in prods]
    nk = 1 if tk is None else ks[0] // tk
    if nk > 1:
        assert all(k == ks[0] for k in ks) and ks[0] % tk == 0
    assert n % tn == 0 and m % tm == 0
    grid = (n // tn, m // tm, nk)
    in_specs, est = [], 0
    for k_dim in ks:
        bk = k_dim if nk == 1 else tk
        in_specs.append(pl.BlockSpec((tm, bk), lambda j, i, k: (i, k)))
    weights = []
    for (x, w, off, row_off), k_dim in zip(prods, ks):
        bk = k_dim if nk == 1 else tk
        assert off % tn == 0 and row_off % bk == 0
        if resident_weights:
            assert nk == 1 and n == tn
        spec, operand = _weight_spec(w, bk, tn, lambda k, rb=row_off // bk: k + rb,
                                     lambda j, ob=off // tn: j + ob, single_buffer=resident_weights)
        in_specs.append(spec)
        weights.append(operand)
        est += 2 * _nbytes((tm, bk), x.dtype) + (1 if resident_weights else 2) * _nbytes((bk, tn), w.dtype)
    for t in tiles:
        in_specs.append(pl.BlockSpec((tm, tn), lambda j, i, k: (i, j)))
        est += 2 * _nbytes((tm, tn), t.dtype)
    for r in rows:
        in_specs.append(pl.BlockSpec((1, tn), lambda j, i, k: (0, j)))
    out_specs = [pl.BlockSpec((tm, tn), lambda j, i, k: (i, j)) for _ in out_dtypes]
    out_shape = [jax.ShapeDtypeStruct((m, n), dt) for dt in out_dtypes]
    est += sum(2 * _nbytes((tm, tn), dt) for dt in out_dtypes)
    est += (len(prods) + 2) * _nbytes((tm, tn), F32)
    scratch = [pltpu.VMEM((tm, tn), F32) for _ in prods] if nk > 1 else []
    body = functools.partial(_mm_body, nprod=len(prods), ntile=len(tiles), nrow=len(rows),
                             nout=len(out_dtypes), nk=nk, epilogue=epilogue, col_chunk=col_chunk,
                             row_chunk=row_chunk)
    return pl.pallas_call(
        body, name=name, grid=grid, in_specs=in_specs, out_specs=out_specs, out_shape=out_shape,
        scratch_shapes=scratch,
        compiler_params=_params(("parallel", "parallel", "arbitrary"), est),
    )(*[p[0] for p in prods], *weights, *tiles, *rows)


def _row(v):
    return v.reshape(1, -1).astype(F32)


class _LayerOf:
    def __init__(self, stack, layer):
        self.stack, self.layer = stack, layer
        self.shape, self.dtype = stack.shape[1:], stack.dtype


class _Layers:
    def __init__(self, stack):
        self.stack = stack

    def __getitem__(self, layer):
        return _LayerOf(self.stack, layer)


def _weight_spec(w, rows, cols, row_block, col_block, single_buffer=False):
    mode = dict(pipeline_mode=pl.Buffered(1)) if single_buffer else {}
    if isinstance(w, _LayerOf):
        return (pl.BlockSpec((None, rows, cols), lambda j, i, k=0: (w.layer, row_block(k), col_block(j)),
                             **mode), w.stack)
    return pl.BlockSpec((rows, cols), lambda j, i, k=0: (row_block(k), col_block(j)), **mode), w


def _gate_body(u_ref, va_ref, ws_ref, bs_ref, o_ref, *, blk, nblk):
    ti = lax.broadcasted_iota(jnp.int32, (blk, blk), 0) // CHUNK
    si = lax.broadcasted_iota(jnp.int32, (blk, blk), 1) // CHUNK
    allowed = ti >= si
    for h in range(A_HEADS):
        w = jnp.where(allowed, ws_ref[h], 0.0).astype(BF)
        bias = bs_ref[h]
        for nb in range(nblk):
            rs = slice(nb * blk, (nb + 1) * blk)
            cs = slice(h * A_DIM, (h + 1) * A_DIM)
            mixed = jnp.dot(w, va_ref[rs, cs], preferred_element_type=F32) + bias
            o_ref[rs, cs] = (u_ref[rs, cs].astype(F32) * mixed).astype(o_ref.dtype)


def _spatial_gate(u, va, w_s, b_s, blk):
    m = u.shape[0]
    tm = _tile(m, 512)
    nblk = tm // blk
    est = 2 * 3 * _nbytes((tm, A_WIDTH), BF) + 2 * _nbytes((A_HEADS, blk, blk), F32)
    return pl.pallas_call(
        functools.partial(_gate_body, blk=blk, nblk=nblk),
        name="spatial_gate",
        grid=(m // tm,),
        in_specs=[pl.BlockSpec((tm, A_WIDTH), lambda i: (i, 0)),
                  pl.BlockSpec((tm, A_WIDTH), lambda i: (i, 0)),
                  pl.BlockSpec((A_HEADS, blk, blk), lambda i: (0, 0, 0)),
                  pl.BlockSpec((A_HEADS, blk, 1), lambda i: (0, 0, 0))],
        out_specs=pl.BlockSpec((tm, A_WIDTH), lambda i: (i, 0)),
        out_shape=jax.ShapeDtypeStruct((m, A_WIDTH), BF),
        compiler_params=_params(("parallel",), est),
    )(u, va, w_s.astype(F32), b_s.astype(F32)[:, :, None])


def _later_keys(nkeys):
    jj = lax.broadcasted_iota(jnp.int32, (nkeys, nkeys), 0)
    ss = lax.broadcasted_iota(jnp.int32, (nkeys, nkeys), 1)
    return jnp.where(jj > ss, 1.0, 0.0).astype(BF)


def _sb_chain(qh, kb, valid, later, carry, keys_on_lanes=False):
    if keys_on_lanes:
        z = jnp.dot(qh, kb, preferred_element_type=F32)
    else:
        z = lax.dot_general(qh, kb, NT_DIMS, preferred_element_type=F32)
    yield
    neg_z = -z
    log_fail = jnp.minimum(neg_z, 0.0) - jnp.log(1.0 + jnp.exp(jnp.minimum(z, neg_z)))
    log_hit = z + log_fail
    if valid is not None:
        log_fail = jnp.where(valid, log_fail, 0.0)
    part = log_fail.astype(BF)
    after = jnp.dot(part, later, preferred_element_type=F32)
    rest = log_fail
    for _ in range(SB_CUMSUM_PASSES - 1):
        rest = rest - part.astype(F32)
        part = rest.astype(BF)
        after = after + jnp.dot(part, later, preferred_element_type=F32)
    yield
    total = after[:, 0:1] + log_fail[:, 0:1]
    if carry is None:
        return log_hit + after, total
    w = jnp.exp(log_hit + after + carry)
    if valid is not None:
        w = jnp.where(valid, w, 0.0)
    return w.astype(BF), total


def _sb_cache_body(q_ref, kn_ref, vn_ref, kc_ref, vc_ref, o_ref, c_ref, acc_ref, *, n_steps):
    step = pl.program_id(1)
    t = q_ref.shape[0]
    head_cols = lambda h: slice(h * B_HEAD_DIM, (h + 1) * B_HEAD_DIM)

    def attend(key_of, val_of, nkeys, valid, keys_on_lanes):
        later = _later_keys(nkeys)
        gens = [_sb_chain(q_ref[:, head_cols(h)], key_of(h).astype(BF), valid, later, c_ref[h], keys_on_lanes)
                for h in range(B_HEADS)]
        for h, (w, total) in enumerate(_run_in_lockstep(gens)):
            vals = val_of(h).astype(BF)
            if keys_on_lanes:
                out = lax.dot_general(w, vals, NT_DIMS, preferred_element_type=F32)
            else:
                out = jnp.dot(w, vals, preferred_element_type=F32)
            acc_ref[h] = acc_ref[h] + out
            c_ref[h] = c_ref[h] + total

    @pl.when(step == 0)
    def _():
        c_ref[...] = jnp.zeros_like(c_ref)
        acc_ref[...] = jnp.zeros_like(acc_ref)
        tt = lax.broadcasted_iota(jnp.int32, (t, t), 0)
        sk = lax.broadcasted_iota(jnp.int32, (t, t), 1)
        attend(lambda h: kn_ref[:, head_cols(h)], lambda h: vn_ref[:, head_cols(h)], t, sk < tt, False)

    attend(lambda h: kc_ref[h], lambda h: vc_ref[h], kc_ref.shape[2], None, True)

    @pl.when(step == n_steps - 1)
    def _():
        for h in range(B_HEADS):
            o_ref[:, head_cols(h)] = acc_ref[h].astype(o_ref.dtype)


def _stick_breaking_cache(q, k_new, v_new, k_cache, v_cache, layer, tk):
    bsz, t, _ = q.shape
    p_len = k_cache.shape[4]
    n_steps = p_len // tk
    new_spec = pl.BlockSpec((None, t, B_WIDTH), lambda b, s: (b, 0, 0))
    cache_spec = pl.BlockSpec((None, None, B_HEADS, B_HEAD_DIM, tk),
                              lambda b, s: (layer, b, 0, 0, n_steps - 1 - s))
    est = (2 * 2 * _nbytes((tk, B_HEADS, LANES), F32) + 8 * _nbytes((t, B_WIDTH), F32)
           + 12 * B_HEADS * _nbytes((t, tk), F32))
    return pl.pallas_call(
        functools.partial(_sb_cache_body, n_steps=n_steps),
        name="stick_breaking_cache",
        grid=(bsz, n_steps),
        in_specs=[new_spec, new_spec, new_spec, cache_spec, cache_spec],
        out_specs=new_spec,
        out_shape=jax.ShapeDtypeStruct((bsz, t, B_WIDTH), BF),
        scratch_shapes=[pltpu.VMEM((B_HEADS, t, 1), F32), pltpu.VMEM((B_HEADS, t, B_HEAD_DIM), F32)],
        compiler_params=_params(("parallel", "arbitrary"), est),
    )(q, k_new, v_new, k_cache, v_cache)


def _sb_body(q_ref, kd_ref, vd_ref, kp_ref, vp_ref, o_ref, *, sub, nq, tk, blocks_per_iter, past_iters,
             unroll):
    head_a_q = lax.broadcasted_iota(jnp.int32, (sub, LANES), 1) < B_HEAD_DIM
    q_subs = []
    for i in range(nq):
        q = q_ref[i * sub:(i + 1) * sub, :]
        zero_q = jnp.zeros_like(q)
        q_subs.append((jnp.where(head_a_q, q, zero_q), jnp.where(head_a_q, zero_q, q)))

    def attend(k_blocks, v_blocks, masks, carries, accs):
        kbs = [k.astype(BF) for k in k_blocks]
        laters, v_stacks = {}, []
        for kb, v_blk in zip(kbs, v_blocks):
            nkeys = kb.shape[0]
            if nkeys not in laters:
                laters[nkeys] = _later_keys(nkeys)
            vb = v_blk.astype(BF)
            head_a_k = lax.broadcasted_iota(jnp.int32, (nkeys, LANES), 1) < B_HEAD_DIM
            zero_v = jnp.zeros_like(vb)
            v_stacks.append(jnp.concatenate([jnp.where(head_a_k, vb, zero_v), jnp.where(head_a_k, zero_v, vb)],
                                            axis=0))
        early = len(kbs) == 1
        keys, gens = [], []
        for i in range(nq):
            for j, kb in enumerate(kbs):
                if isinstance(masks[i][j], str):
                    continue
                for h in range(2):
                    keys.append((i, j, h))
                    gens.append(_sb_chain(q_subs[i][h], kb, masks[i][j], laters[kb.shape[0]],
                                          carries[i][h] if early else None))
        scores = dict(zip(keys, _run_in_lockstep(gens, lag=SB_CHAIN_LAG)))
        new_carries, new_accs = [], []
        for i in range(nq):
            cs = list(carries[i])
            wts, vals = [], []
            for j in range(len(kbs)):
                if isinstance(masks[i][j], str):
                    continue
                for h in range(2):
                    w, total = scores[(i, j, h)]
                    if not early:
                        w = jnp.exp(w + cs[h])
                        if masks[i][j] is not None:
                            w = jnp.where(masks[i][j], w, 0.0)
                        w = w.astype(BF)
                    cs[h] = cs[h] + total
                    wts.append(w)
                vals.append(v_stacks[j])
            new_carries.append(cs)
            new_accs.append(accs[i] + jnp.dot(jnp.concatenate(wts, axis=1), jnp.concatenate(vals, axis=0),
                                              preferred_element_type=F32))
        return new_carries, new_accs

    k_diag = [kd_ref[j * sub:(j + 1) * sub, :] for j in reversed(range(nq))]
    v_diag = [vd_ref[j * sub:(j + 1) * sub, :] for j in reversed(range(nq))]
    tt = lax.broadcasted_iota(jnp.int32, (sub, sub), 0)
    sk = lax.broadcasted_iota(jnp.int32, (sub, sub), 1)
    causal = sk < tt
    masks = [[causal if j == i else (None if j < i else "skip") for j in reversed(range(nq))]
             for i in range(nq)]
    zeros_c = jnp.zeros((sub, 1), F32)
    carries, accs = attend(k_diag, v_diag, masks, [[zeros_c, zeros_c] for _ in range(nq)],
                           [jnp.zeros((sub, LANES), F32) for _ in range(nq)])
    n_iters = past_iters(pl.program_id(2))
    open_masks = [[None] * blocks_per_iter for _ in range(nq)]

    def step(it, state):
        carries = [list(state[2 * i:2 * i + 2]) for i in range(nq)]
        accs = list(state[2 * nq:])
        newest = n_iters - 1 - it
        offs = [pl.multiple_of((newest * blocks_per_iter + j) * tk, tk)
                for j in reversed(range(blocks_per_iter))]
        carries, accs = attend([kp_ref[pl.ds(o, tk), :] for o in offs], [vp_ref[pl.ds(o, tk), :] for o in offs],
                               open_masks, carries, accs)
        return tuple(c for pair in carries for c in pair) + tuple(accs)

    state = lax.fori_loop(0, n_iters, step, tuple(c for pair in carries for c in pair) + tuple(accs),
                          unroll=unroll)
    for i in range(nq):
        o_ref[i * sub:(i + 1) * sub, :] = state[2 * nq + i].astype(o_ref.dtype)


def _stick_breaking(q, k_new, v_new, k_past, v_past, *, tq, tk, causal_past, past_layer=None):
    bsz, t, _ = q.shape
    p_len = k_past.shape[-2]
    if past_layer is None:
        past_spec = pl.BlockSpec((None, p_len, LANES), lambda b, p, i: (b, 0, p))
    else:
        past_spec = pl.BlockSpec((None, None, p_len, LANES), lambda b, p, i: (past_layer, b, 0, p))
    sub = min(tq, SB_SUBTILE)
    nq = tq // sub
    if causal_past:
        blocks_per_iter, unroll = 1, None
        past_iters = lambda qi: qi * (tq // tk)
    else:
        n_blocks = p_len // tk
        blocks_per_iter, unroll = (2 if n_blocks % 2 == 0 else 1), True
        past_iters = lambda qi: n_blocks // blocks_per_iter
    est = (2 * 2 * _nbytes((p_len, LANES), k_past.dtype) + 2 * 4 * _nbytes((tq, LANES), F32)
           + 12 * 2 * nq * blocks_per_iter * _nbytes((sub, max(sub, tk)), F32))
    return pl.pallas_call(
        functools.partial(_sb_body, sub=sub, nq=nq, tk=tk, blocks_per_iter=blocks_per_iter,
                          past_iters=past_iters, unroll=unroll),
        name="stick_breaking_past" if causal_past else "stick_breaking_cache",
        grid=(bsz, B_WIDTH // LANES, t // tq),
        in_specs=[pl.BlockSpec((None, tq, LANES), lambda b, p, i: (b, i, p)),
                  pl.BlockSpec((None, tq, LANES), lambda b, p, i: (b, i, p)),
                  pl.BlockSpec((None, tq, LANES), lambda b, p, i: (b, i, p)),
                  past_spec, past_spec],
        out_specs=pl.BlockSpec((None, tq, LANES), lambda b, p, i: (b, i, p)),
        out_shape=jax.ShapeDtypeStruct((bsz, t, B_WIDTH), BF),
        compiler_params=_params(("parallel", "parallel", "arbitrary"), est),
    )(q, k_new, v_new, k_past, v_past)


def _shift_body(x_ref, last_ref, mu_ref, *rest, tiles_per_seq):
    outs, carry = rest[:6], rest[6]
    i = pl.program_id(1)

    @pl.when(i % tiles_per_seq == 0)
    def _():
        carry[...] = jnp.broadcast_to(last_ref[...], carry.shape)

    x = x_ref[...]
    rows = lax.broadcasted_iota(jnp.int32, x.shape, 0)
    x_prev = jnp.where(rows == 0, carry[SUBLANES - 1:SUBLANES, :], pltpu.roll(x, 1, 0))
    carry[...] = x[x.shape[0] - SUBLANES:, :]
    xx = x_prev - x
    for j, o_ref in enumerate(outs):
        o_ref[...] = (x + xx * mu_ref[j:j + 1, :]).astype(o_ref.dtype)


def _token_shift(x, x_last, mu):
    bsz, t, d = x.shape
    tm = _tile(t, 256)
    est = 2 * _nbytes((tm, d), F32) + 12 * _nbytes((tm, d), BF) + 4 * _nbytes((tm, d), F32)
    return pl.pallas_call(
        functools.partial(_shift_body, tiles_per_seq=t // tm),
        name="token_shift",
        grid=(bsz, t // tm),
        in_specs=[pl.BlockSpec((None, tm, d), lambda b, i: (b, i, 0)),
                  pl.BlockSpec((None, 1, d), lambda b, i: (b, 0, 0)),
                  pl.BlockSpec((6, d), lambda b, i: (0, 0))],
        out_specs=[pl.BlockSpec((None, tm, d), lambda b, i: (b, i, 0)) for _ in range(6)],
        out_shape=[jax.ShapeDtypeStruct((bsz, t, d), BF) for _ in range(6)],
        scratch_shapes=[pltpu.VMEM((SUBLANES, d), F32)],
        compiler_params=_params(("parallel", "arbitrary"), est),
    )(x, x_last[:, None, :].astype(F32), mu.astype(F32))


def _wkv_body(r_ref, k_ref, v_ref, lw_ref, a_ref, g_ref, kk_ref, ka_ref, rk_ref, gg_ref, gb_ref,
              s0_ref, y_ref, s_ref):
    c = WKV_CHUNK
    n = 2 * c

    @pl.when(pl.program_id(1) == 0)
    def _():
        s_ref[...] = s0_ref[...]

    head_a = lax.broadcasted_iota(jnp.int32, (c, LANES), 1) < C_HEAD_DIM
    ri = lax.broadcasted_iota(jnp.int32, (n, n), 0)
    ci = lax.broadcasted_iota(jnp.int32, (n, n), 1)
    same_head = (ri // c) == (ci // c)
    strict = same_head & (ci < ri)
    inclusive = same_head & (ci <= ri)
    diag16 = (ri // 16) == (ci // 16)
    diag32 = (ri // 32) == (ci // 32)
    eye = jnp.where(ri == ci, 1.0, 0.0).astype(F32)
    seg_ones = jnp.where(same_head, 1.0, 0.0).astype(BF)
    ti = lax.broadcasted_iota(jnp.int32, (c, c), 0)
    si = lax.broadcasted_iota(jnp.int32, (c, c), 1)
    prefix = jnp.where(si <= ti, 1.0, 0.0).astype(BF)

    def split2(x):
        hi = x.astype(BF)
        return hi, (x - hi.astype(F32)).astype(BF)

    def seg_sum(x):
        hi, lo = split2(x)
        return (jnp.dot(hi, seg_ones, preferred_element_type=F32)
                + jnp.dot(lo, seg_ones, preferred_element_type=F32))

    def cumsum_time(x):
        hi, rest = split2(x)
        rest_f = x - hi.astype(F32)
        mid = rest_f.astype(BF)
        lo = (rest_f - mid.astype(F32)).astype(BF)
        return (jnp.dot(prefix, hi, preferred_element_type=F32)
                + jnp.dot(prefix, mid, preferred_element_type=F32)
                + jnp.dot(prefix, lo, preferred_element_type=F32))

    def stack(x):
        zero = jnp.zeros_like(x)
        return jnp.concatenate([jnp.where(head_a, x, zero), jnp.where(head_a, zero, x)], axis=0)

    def pair(p, s):
        sl = pl.ds(pl.multiple_of(p * LANES, LANES), LANES)
        r = r_ref[:, sl]
        k = k_ref[:, sl]
        v = v_ref[:, sl]
        lw = lw_ref[:, sl]
        a = a_ref[:, sl]
        kkp = k * kk_ref[:, sl]
        norm = jnp.sqrt(seg_sum(kkp * kkp))
        yield
        kk = kkp / jnp.maximum(norm, 1e-12)
        k2 = k * (1.0 + (a - 1.0) * ka_ref[:, sl])
        bonus = seg_sum(r * k2 * rk_ref[:, sl]) * v
        yield
        cs = cumsum_time(lw)
        yield
        e_pos = jnp.exp(cs)
        e_neg = jnp.exp(-cs)
        a_t = -kk * e_pos * jnp.exp(-lw)
        b_t = kk * a * e_neg
        k_t = k2 * e_neg
        r_t = r * e_pos
        la, lr = stack(a_t), stack(r_t)
        rb, rk = stack(b_t), stack(k_t)
        vs = stack(v)
        q = lax.dot_general(jnp.concatenate([la, lr], axis=0).astype(BF),
                            jnp.concatenate([rb, rk], axis=0).astype(BF),
                            NT_DIMS, preferred_element_type=F32)
        yield
        p_ab = jnp.where(strict, q[:n, :n], 0.0)
        a_ak = jnp.where(strict, q[:n, n:], 0.0)
        a_rb = jnp.where(inclusive, q[n:, :n], 0.0)
        a_rk = jnp.where(inclusive, q[n:, n:], 0.0)
        akv = _bdot(a_ak, vs)
        yield
        pd = jnp.where(diag16, p_ab, 0.0)
        inv = eye + pd
        pw = pd
        for _ in range(3):
            pw = _bdot(pw, pw)
            yield
            inv = inv + _bdot(pw, inv)
            yield
        for off_diag in (jnp.where(diag32 & (~diag16), p_ab, 0.0), jnp.where(diag32, 0.0, p_ab)):
            right = _bdot(off_diag, inv)
            yield
            inv = inv + _bdot(inv, right)
            yield
        wu = _bdot(inv, jnp.concatenate([la, akv], axis=1))
        yield
        w_m, u_m = wu[:, :LANES], wu[:, LANES:]
        wr = lax.dot_general(jnp.concatenate([w_m, lr], axis=0).astype(BF), s.astype(BF),
                             NT_DIMS, preferred_element_type=F32)
        yield
        sa = wr[:n] + u_m
        sv = jnp.concatenate([sa, vs], axis=0)
        y_st = wr[n:] + _bdot(jnp.concatenate([a_rb, a_rk], axis=1), sv)
        g_end = e_pos[c - 1:c, :]
        rg = jnp.concatenate([rb, rk], axis=0) * g_end
        s_new = s * g_end + lax.dot_general(sv.astype(BF), rg.astype(BF), TN_DIMS,
                                            preferred_element_type=F32)
        yield
        y = y_st[:c] + y_st[c:]
        mean = seg_sum(y) * (1.0 / C_HEAD_DIM)
        yield
        d = y - mean
        var = seg_sum(d * d) * (1.0 / C_HEAD_DIM)
        yield
        yn = d * lax.rsqrt(var + C_GN_EPS) * gg_ref[:, sl] + gb_ref[:, sl]
        return s_new, ((yn + bonus) * g_ref[:, sl]).astype(y_ref.dtype)

    def group(it, carry):
        pairs = [it * WKV_PAIRS_PER_STEP + u for u in range(WKV_PAIRS_PER_STEP)]
        states = [s_ref[p] for p in pairs]
        results = _run_in_lockstep([pair(p, s) for p, s in zip(pairs, states)])
        for p, (s_new, y_out) in zip(pairs, results):
            s_ref[p] = s_new
            y_ref[:, pl.ds(pl.multiple_of(p * LANES, LANES), LANES)] = y_out
        return carry

    lax.fori_loop(0, N_PAIRS // WKV_PAIRS_PER_STEP, group, 0)


def _pair_state(s):
    bsz = s.shape[0]
    s = s.reshape(bsz, N_PAIRS, 2, C_HEAD_DIM, C_HEAD_DIM).astype(F32)
    z = jnp.zeros_like(s[:, :, 0])
    top = jnp.concatenate([s[:, :, 0], z], axis=-1)
    bot = jnp.concatenate([z, s[:, :, 1]], axis=-1)
    return jnp.concatenate([top, bot], axis=-2)


def _unpair_state(sp):
    bsz = sp.shape[0]
    a = sp[:, :, :C_HEAD_DIM, :C_HEAD_DIM]
    b = sp[:, :, C_HEAD_DIM:, C_HEAD_DIM:]
    return jnp.stack([a, b], axis=2).reshape(bsz, C_HEADS, C_HEAD_DIM, C_HEAD_DIM)


def _wkv(r, k, v, lw, a, g, s0, k_k, k_a, r_k, gn_g, gn_b):
    bsz, t, d = r.shape
    c = WKV_CHUNK
    seq = pl.BlockSpec((None, c, d), lambda b, i: (b, i, 0))
    vec = pl.BlockSpec((1, d), lambda b, i: (0, 0))
    st = pl.BlockSpec((None, N_PAIRS, HEAD_PAIR, HEAD_PAIR), lambda b, i: (b, 0, 0, 0))
    est = 2 * 6 * _nbytes((c, d), F32) + 4 * _nbytes((N_PAIRS, HEAD_PAIR, HEAD_PAIR), F32) + (8 << 20)
    y, s_new = pl.pallas_call(
        _wkv_body,
        name="wkv7_chunked",
        grid=(bsz, t // c),
        in_specs=[seq] * 6 + [vec] * 5 + [st],
        out_specs=[seq, st],
        out_shape=[jax.ShapeDtypeStruct((bsz, t, d), BF),
                   jax.ShapeDtypeStruct((bsz, N_PAIRS, HEAD_PAIR, HEAD_PAIR), F32)],
        compiler_params=_params(("parallel", "arbitrary"), est),
    )(r, k, v, lw, a, g, _row(k_k), _row(k_a), _row(r_k), _row(gn_g), _row(gn_b), _pair_state(s0))
    return y, _unpair_state(s_new)


def _ffn_in_body(x_ref, wg_ref, wu_ref, cw_ref, cb_ref, prev_ref, act_ref, st_ref, tail_ref, *,
                 seq_len, tm):
    x = x_ref[...]
    tn = act_ref.shape[1]
    cn = min(tn, MXU_WIDTH)
    last2, last1 = SUBLANES - 2, SUBLANES - 1
    carried = seq_len >= tm
    if carried:
        @pl.when(pl.program_id(1) % (seq_len // tm) == 0)
        def _():
            tail_ref[...] = prev_ref[0]

    def conv_gate(hg, hu, cols):
        rows = lax.broadcasted_iota(jnp.int32, hg.shape, 0)
        if carried:
            tail = tail_ref[:, cols]
            p0, p1 = tail[last2:last2 + 1, :], tail[last1:last1 + 1, :]
            pos = rows
            new_tail = hg[tm - SUBLANES:, :]
            tail_ref[:, cols] = new_tail
            st_ref[0, :, cols] = new_tail
        else:
            nseq = tm // seq_len
            prev = prev_ref[:, :, cols]
            spread = lambda rw: jnp.broadcast_to(rw, (nseq, seq_len, cn)).reshape(tm, cn)
            p0, p1 = spread(prev[:, last2:last2 + 1, :]), spread(prev[:, last1:last1 + 1, :])
            pos = rows % seq_len
            st_ref[:, :, cols] = hg.reshape(nseq, seq_len, cn)[:, seq_len - SUBLANES:, :]
        s1 = jnp.where(pos == 0, p1, pltpu.roll(hg, 1, 0))
        s2 = jnp.where(pos == 0, p0, jnp.where(pos == 1, p1, pltpu.roll(hg, 2, 0)))
        cw = cw_ref[:, cols]
        hc = cb_ref[:, cols] + s2 * cw[0:1, :] + s1 * cw[1:2, :] + hg * cw[2:3, :]
        act_ref[:, cols] = (jax.nn.gelu(hc) * hu).astype(act_ref.dtype)

    pending = None
    for ch in range(tn // cn):
        cols = slice(ch * cn, (ch + 1) * cn)
        hg = jnp.dot(x, wg_ref[:, cols], preferred_element_type=F32)
        hu = jnp.dot(x, wu_ref[:, cols], preferred_element_type=F32)
        if pending is not None:
            conv_gate(*pending)
        pending = (hg, hu, cols)
    conv_gate(*pending)


def _ffn_in(xb, w_in, conv_w, conv_b, conv_prev, seq_len):
    m, d = xb.shape
    nseq_total = m // seq_len
    tn = 512
    tm = _tile(m, 1024)
    nj = D_FF // tn
    prev8 = jnp.pad(conv_prev.astype(F32), ((0, 0), (SUBLANES - (CONV_W - 1), 0), (0, 0)))
    if seq_len >= tm:
        per = seq_len // tm
        nb = 1
        st_map = lambda j, i: (i // per, 0, j)
    else:
        nb = tm // seq_len
        st_map = lambda j, i: (i, 0, j)
    st_spec = pl.BlockSpec((nb, SUBLANES, tn), st_map)
    est = (2 * _nbytes((tm, d), BF) + 4 * _nbytes((d, tn), BF) + 2 * _nbytes((tm, tn), BF)
           + 8 * _nbytes((tm, tn), F32))
    gate_spec, w_operand = _weight_spec(w_in, d, tn, lambda k: 0, lambda j: j)
    up_spec, _ = _weight_spec(w_in, d, tn, lambda k: 0, lambda j: j + nj)
    act, tails = pl.pallas_call(
        functools.partial(_ffn_in_body, seq_len=seq_len, tm=tm),
        name="ffn_in_conv",
        grid=(nj, m // tm),
        in_specs=[pl.BlockSpec((tm, d), lambda j, i: (i, 0)),
                  gate_spec,
                  up_spec,
                  pl.BlockSpec((CONV_W, tn), lambda j, i: (0, j)),
                  pl.BlockSpec((1, tn), lambda j, i: (0, j)),
                  st_spec],
        out_specs=[pl.BlockSpec((tm, tn), lambda j, i: (i, j)), st_spec],
        out_shape=[jax.ShapeDtypeStruct((m, D_FF), BF),
                   jax.ShapeDtypeStruct((nseq_total, SUBLANES, D_FF), F32)],
        scratch_shapes=[pltpu.VMEM((SUBLANES, tn), F32)],
        compiler_params=_params(("parallel", "arbitrary"), est),
    )(xb, w_operand, w_operand, conv_w.astype(F32), _row(conv_b), prev8)
    return act, tails[:, SUBLANES - (CONV_W - 1):, :]


def _ln_residual_epilogue(accs, tiles, rows):
    mix = accs[0]
    for extra in accs[1:]:
        mix = mix + extra
    y = _layernorm(DN_ALPHA * tiles[0] + mix, rows[0], rows[1])
    return y, y


def _identity_epilogue(accs, tiles, rows):
    return (accs[0],)


def _even_mixer(x, xb, seq_len, w_in, lnv_g, lnv_b, w_s, b_s, w_o, ln_g, ln_b, cache):
    m = x.shape[0]
    bsz = m // seq_len
    proj = lambda name, off, ep, dts, rows=(), col_chunk=MXU_WIDTH: _mm(
        [(xb, w_in, off)], ep, dts, n=1024, tm=512, tn=1024, rows=rows, name=name, col_chunk=col_chunk)
    (u,) = proj("even_u", 0, lambda a, t, r: (jax.nn.gelu(a[0]),), [BF])

    def va_epilogue(a, t, r):
        va = _layernorm(jax.nn.gelu(a[0]), r[0], r[1])
        return va, va
    va, vab = proj("even_va", A_WIDTH, va_epilogue, [F32, BF], rows=(_row(lnv_g), _row(lnv_b)),
                   col_chunk=None)
    o = 2 * A_WIDTH
    (q,) = proj("even_q", o, lambda a, t, r: (a[0] * (B_HEAD_DIM ** -0.5),), [BF])
    (k,) = proj("even_k", o + B_WIDTH, _identity_epilogue, [F32])
    (v,) = proj("even_v", o + 2 * B_WIDTH, _identity_epilogue, [F32])

    blk = min(seq_len, A_BLOCK)
    a_out = _spatial_gate(u, vab, w_s[:, :blk, :blk], b_s[:, :blk], blk)

    q3 = q.reshape(bsz, seq_len, B_WIDTH)
    k3 = k.reshape(bsz, seq_len, B_WIDTH)
    v3 = v.reshape(bsz, seq_len, B_WIDTH)
    if cache is None:
        tq = _tile(seq_len, 2 * SB_BLOCK)
        b_out = _stick_breaking(q3, k3, v3, k3, v3, tq=tq, tk=min(tq, SB_BLOCK), causal_past=True)
    else:
        k_cache, v_cache, layer = cache
        b_out = _stick_breaking_cache(q3, k3, v3, k_cache, v_cache, layer, _tile(k_cache.shape[4], SB_BLOCK))
    b_out = b_out.reshape(m, B_WIDTH)

    x1, x1b = _mm([(a_out, w_o, 0), (b_out, w_o, 0, A_WIDTH)], _ln_residual_epilogue, [F32, BF],
                  n=D_MODEL, tm=2 * LN_ROW_CHUNK, tn=D_MODEL, tiles=(x,), rows=(_row(ln_g), _row(ln_b)),
                  name="even_out_ln", col_chunk=None, row_chunk=LN_ROW_CHUNK, resident_weights=True)
    return x1, x1b, k, v, va


def _rwkv_mixer(x, seq_len, x_last, s0, v_first, P, ln_g, ln_b):
    m = x.shape[0]
    bsz = m // seq_len
    mixes = _token_shift(x.reshape(bsz, seq_len, D_MODEL), x_last, P['mu'])
    xr, xw, xk, xv, xa, xg = [a.reshape(m, D_MODEL) for a in mixes]
    big = lambda xin, w: _mm([(xin, w, 0)], _identity_epilogue, [F32], n=D_MODEL, tm=512, tn=1024,
                             name="rwkv_proj")[0]
    r = big(xr, P['w_r'])
    k = big(xk, P['w_k'])

    def lora_in(xin, w, act):
        n = w.shape[1]
        return _mm([(xin, w, 0)], lambda a, t, r_: (act(a[0]),), [BF], n=n, tm=512, tn=n,
                   name="rwkv_lora_in")[0]

    def lora_out(h, w, bias, act):
        return _mm([(h, w, 0)], lambda a, t, r_: (act(r_[0] + a[0]),), [F32], n=D_MODEL, tm=512,
                   tn=1024, rows=(_row(bias),), name="rwkv_lora_out")[0]

    hw = lora_in(xw, P['w1'], jnp.tanh)
    lw = lora_out(hw, P['w2'], P['w0'],
                  lambda pre: -jnp.exp(-(jnp.maximum(-pre, 0.0) + jnp.log(1.0 + jnp.exp(-jnp.abs(pre)))) - 0.5))
    ha = lora_in(xa, P['a1'], lambda z: z)
    a = lora_out(ha, P['a2'], P['a0'], jax.nn.sigmoid)
    hg = lora_in(xg, P['g1'], jax.nn.sigmoid)
    g = _mm([(hg, P['g2'], 0)], _identity_epilogue, [F32], n=D_MODEL, tm=512, tn=1024, name="rwkv_gate")[0]
    if P['v1'] is None:
        v = big(xv, P['w_v'])
        v_first = v
    else:
        hv = lora_in(xv, P['v1'], lambda z: z)

        def v_epilogue(accs, tiles, rows):
            v_new = accs[0]
            return (v_new + (tiles[0] - v_new) * jax.nn.sigmoid(rows[0] + accs[1]),)
        v = _mm([(xv, P['w_v'], 0), (hv, P['v2'], 0)], v_epilogue, [F32], n=D_MODEL, tm=512, tn=1024,
                tiles=(v_first,), rows=(_row(P['v0']),), name="rwkv_v_residual")[0]
    to3 = lambda z: z.reshape(bsz, seq_len, D_MODEL)
    yg, s_new = _wkv(to3(r), to3(k), to3(v), to3(lw), to3(a), to3(g), s0,
                     P['k_k'], P['k_a'], P['r_k'], P['gn_g'], P['gn_b'])
    x1, x1b = _mm([(yg.reshape(m, D_MODEL), P['w_o'], 0)], _ln_residual_epilogue, [F32, BF],
                  n=D_MODEL, tm=2 * LN_ROW_CHUNK, tn=D_MODEL, tiles=(x,), rows=(_row(ln_g), _row(ln_b)),
                  name="rwkv_out_ln", col_chunk=None, row_chunk=LN_ROW_CHUNK, resident_weights=True)
    return x1, x1b, s_new, v_first


def _ffn_ple(x1, x1b, seq_len, p, conv_prev, w_in, conv_w, conv_b, w_out, ln_g, ln_b, ple_gate, ple_proj):
    act, conv_rows = _ffn_in(x1b, w_in, conv_w, conv_b, conv_prev, seq_len)
    x2, x2b = _mm([(act, w_out, 0)], _ln_residual_epilogue, [F32, BF], n=D_MODEL, tm=512, tn=D_MODEL,
                  tk=D_FF // 4, tiles=(x1,), rows=(_row(ln_g), _row(ln_b)), name="ffn_out_ln", col_chunk=None)

    def ple_epilogue(accs, tiles, rows):
        y = tiles[0] + jax.nn.sigmoid(accs[0]) * accs[1]
        return y, y
    x3, x3b = _mm([(x2b, ple_gate, 0), (p, ple_proj, 0)], ple_epilogue, [F32, BF], n=D_MODEL, tm=512,
                  tn=1024, tiles=(x2,), name="ple")
    return x3, x3b, conv_rows


def _trunk(x, p, W, caches, wkv0, shift0, conv0):
    bsz, seq_len, _ = x.shape
    m = bsz * seq_len
    x = x.reshape(m, D_MODEL).astype(F32)
    xb = x.astype(BF)
    new_k, new_v, new_va, new_wkv, new_shift, new_conv = [], [], [], [], [], []
    v_first = None
    for i in range(DEPTH):
        j = i // 2
        if i % 2 == 0:
            cache = None if caches is None else (caches[0], caches[1], j)
            x1, x1b, k, v, va = _even_mixer(
                x, xb, seq_len, W['even_w_in'][j], W['even_lnv_g'][j], W['even_lnv_b'][j],
                W['even_w_s'][j], W['even_b_s'][j], W['even_w_o'][j], W['ln1_g'][i], W['ln1_b'][i], cache)
            new_k.append(k.reshape(bsz, seq_len, B_HEADS, B_HEAD_DIM))
            new_v.append(v.reshape(bsz, seq_len, B_HEADS, B_HEAD_DIM))
            new_va.append(va.reshape(bsz, seq_len, A_HEADS, A_DIM))
        else:
            P = dict(mu=W['c_mu'][j], w_r=W['c_w_r'][j], w_k=W['c_w_k'][j], w_v=W['c_w_v'][j],
                     w_o=W['c_w_o'][j], w0=W['c_w0'][j], w1=W['c_w1'][j], w2=W['c_w2'][j],
                     a0=W['c_a0'][j], a1=W['c_a1'][j], a2=W['c_a2'][j], g1=W['c_g1'][j], g2=W['c_g2'][j],
                     k_k=W['c_k_k'][j], k_a=W['c_k_a'][j], r_k=W['c_r_k'][j], gn_g=W['c_gn_g'][j],
                     gn_b=W['c_gn_b'][j],
                     v0=None if j == 0 else W['c_v0'][j - 1], v1=None if j == 0 else W['c_v1'][j - 1],
                     v2=None if j == 0 else W['c_v2'][j - 1])
            x1, x1b, s_new, v_first = _rwkv_mixer(x, seq_len, shift0[j], wkv0[j], v_first, P,
                                                  W['ln1_g'][i], W['ln1_b'][i])
            new_wkv.append(s_new)
            new_shift.append(x.reshape(bsz, seq_len, D_MODEL)[:, -1])
        x, xb, conv_rows = _ffn_ple(
            x1, x1b, seq_len, p[i].reshape(m, -1), conv0[i], W['ffn_w_in'][i], W['ffn_conv_w'][i],
            W['ffn_conv_b'][i], W['ffn_w_out'][i], W['ln2_g'][i], W['ln2_b'][i], W['ple_gate'][i],
            W['ple_proj'][i])
        new_conv.append(conv_rows)
    st = lambda l: jnp.stack(l) if l else None
    return (x.reshape(bsz, seq_len, D_MODEL), st(new_k), st(new_v), st(new_va), st(new_wkv),
            st(new_shift), st(new_conv))


_MATRICES = ('even_w_in', 'even_w_o', 'c_w_r', 'c_w_k', 'c_w_v', 'c_w_o', 'c_w1', 'c_w2', 'c_a1', 'c_a2',
             'c_v1', 'c_v2', 'c_g1', 'c_g2', 'ffn_w_in', 'ffn_w_out', 'ple_proj', 'ple_gate')


def kernel(x_prompt, x_sample, cache_sb_k, cache_sb_v, state_rwkv_wkv, state_rwkv_shift, state_ffn_conv, p_prompt, p_sample, even_w_in, even_lnv_g, even_lnv_b, even_w_s, even_b_s, even_w_o, c_mu, c_w_r, c_w_k, c_w_v, c_w_o, c_w0, c_w1, c_w2, c_a0, c_a1, c_a2, c_v0, c_v1, c_v2, c_g1, c_g2, c_k_k, c_k_a, c_r_k, c_gn_g, c_gn_b, ffn_w_in, ffn_conv_w, ffn_conv_b, ffn_w_out, ln1_g, ln1_b, ln2_g, ln2_b, ple_proj, ple_gate):
    W = dict(even_w_in=even_w_in, even_lnv_g=even_lnv_g, even_lnv_b=even_lnv_b, even_w_s=even_w_s,
             even_b_s=even_b_s, even_w_o=even_w_o, c_mu=c_mu, c_w_r=c_w_r, c_w_k=c_w_k,
             c_w_v=c_w_v, c_w_o=c_w_o, c_w0=c_w0, c_w1=c_w1, c_w2=c_w2, c_a0=c_a0, c_a1=c_a1,
             c_a2=c_a2, c_v0=c_v0, c_v1=c_v1, c_v2=c_v2, c_g1=c_g1, c_g2=c_g2, c_k_k=c_k_k,
             c_k_a=c_k_a, c_r_k=c_r_k, c_gn_g=c_gn_g, c_gn_b=c_gn_b, ffn_w_in=ffn_w_in,
             ffn_conv_w=ffn_conv_w, ffn_conv_b=ffn_conv_b, ffn_w_out=ffn_w_out, ln1_g=ln1_g,
             ln1_b=ln1_b, ln2_g=ln2_g, ln2_b=ln2_b, ple_proj=ple_proj, ple_gate=ple_gate)
    for name in _MATRICES:
        W[name] = _Layers(W[name].astype(BF))
    bp = x_prompt.shape[0]
    n_odd, n_even = state_rwkv_wkv.shape[0], cache_sb_k.shape[0]
    cache_sb_k = jnp.transpose(cache_sb_k, (0, 1, 3, 4, 2))
    cache_sb_v = jnp.transpose(cache_sb_v, (0, 1, 3, 4, 2))
    wkv_zero = jnp.zeros((n_odd, bp, C_HEADS, C_HEAD_DIM, C_HEAD_DIM), F32)
    shift_zero = jnp.zeros((n_odd, bp, D_MODEL), F32)
    conv_zero = jnp.zeros((DEPTH, bp, CONV_W - 1, D_FF), F32)
    y_p, k_p, v_p, _, wkv_p, shift_p, conv_p = _trunk(
        x_prompt, p_prompt, W, None, wkv_zero, shift_zero, conv_zero)
    y_s, k_s, v_s, va_s, wkv_s, shift_s, conv_s = _trunk(
        x_sample, p_sample, W, (cache_sb_k, cache_sb_v), state_rwkv_wkv, state_rwkv_shift, state_ffn_conv)
    return (y_p, y_s, k_p, v_p, wkv_p, shift_p, conv_p, k_s, v_s, va_s, wkv_s, shift_s, conv_s)
```

```python
import functools

import jax
import jax.numpy as jnp
from jax import lax
from jax.experimental import pallas as pl
from jax.experimental.pallas import tpu as pltpu

F32 = jnp.float32
BF = jnp.bfloat16

D_MODEL = 2048
DEPTH = 4
CHUNK = 64
A_WIDTH = D_MODEL // 2
A_HEADS = 8
A_DIM = A_WIDTH // A_HEADS
A_BLOCK = 128
B_HEAD_DIM = 64
B_WIDTH = D_MODEL // 2
B_HEADS = B_WIDTH // B_HEAD_DIM
C_HEAD_DIM = 64
C_HEADS = D_MODEL // C_HEAD_DIM
C_GN_EPS = 64e-5
D_FF = 5632
CONV_W = 3
LN_EPS = 1e-5
DN_ALPHA = (2 * DEPTH) ** 0.25

MXU_WIDTH = 256
LN_ROW_CHUNK = 256
LANES = 128
SUBLANES = 8
HEAD_PAIR = 2 * C_HEAD_DIM
N_PAIRS = D_MODEL // HEAD_PAIR
WKV_CHUNK = 64
WKV_PAIRS_PER_STEP = 16
SB_BLOCK = 256
SB_SUBTILE = 256
SB_CHAIN_LAG = 0
SB_CUMSUM_PASSES = 1
VMEM_CAP = 56 * 1024 * 1024

NT_DIMS = (((1,), (1,)), ((), ()))
TN_DIMS = (((0,), (0,)), ((), ()))


def _tile(m, pref):
    return pref if m % pref == 0 else m


def _params(sem, est_bytes):
    limit = int(min(max(est_bytes * 5 // 4 + (4 << 20), 16 << 20), VMEM_CAP))
    return pltpu.CompilerParams(dimension_semantics=sem, vmem_limit_bytes=limit)


def _nbytes(shape, dtype):
    n = 1
    for s in shape:
        n *= s
    return n * jnp.dtype(dtype).itemsize


def _layernorm(x, g, b, eps=LN_EPS):
    mu = jnp.mean(x, axis=-1, keepdims=True)
    d = x - mu
    var = jnp.mean(d * d, axis=-1, keepdims=True)
    return d * lax.rsqrt(var + eps) * g + b


def _bdot(a, b):
    return jnp.dot(a.astype(BF), b.astype(BF), preferred_element_type=F32)


def _run_in_lockstep(generators, lag=0):
    results = [None] * len(generators)
    started, live, tick = 0, [], 0
    while started < len(generators) or live:
        while started < len(generators) and tick >= started * lag:
            live.append(started)
            started += 1
        for idx in list(live):
            try:
                next(generators[idx])
            except StopIteration as done:
                results[idx] = done.value
                live.remove(idx)
        tick += 1
    return results


def _mm_body(*refs, nprod, ntile, nrow, nout, nk, epilogue, col_chunk, row_chunk, cast_weights):
    xs = refs[:nprod]
    ws = refs[nprod:2 * nprod]
    ts = refs[2 * nprod:2 * nprod + ntile]
    rs = refs[2 * nprod + ntile:2 * nprod + ntile + nrow]
    outs = refs[2 * nprod + ntile + nrow:2 * nprod + ntile + nrow + nout]
    accs = refs[2 * nprod + ntile + nrow + nout:]
    if cast_weights:
        @pl.when(pl.program_id(1) == 0)
        def _():
            for scratch, w in zip(accs, ws):
                scratch[...] = w[...].astype(BF)
        ws = accs

    def finish(vals, cols=slice(None), rws=slice(None)):
        res = epilogue(vals, [t[rws, cols] for t in ts], [r[:, cols] for r in rs])
        for o_ref, o in zip(outs, res):
            o_ref[rws, cols] = o.astype(o_ref.dtype)

    tm, tn = outs[0].shape
    if nk == 1 and row_chunk is not None and tm > row_chunk and tm % row_chunk == 0:
        pending = None
        for ch in range(tm // row_chunk):
            rws = slice(ch * row_chunk, (ch + 1) * row_chunk)
            cur = [jnp.dot(x[rws, :].astype(BF), w[...], preferred_element_type=F32) for x, w in zip(xs, ws)]
            if pending is not None:
                finish(pending[0], rws=pending[1])
            pending = (cur, rws)
        finish(pending[0], rws=pending[1])
        return
    if nk == 1 and col_chunk is not None and tn > col_chunk and tn % col_chunk == 0:
        lhs = [x[...].astype(BF) for x in xs]
        pending = None
        for ch in range(tn // col_chunk):
            cols = slice(ch * col_chunk, (ch + 1) * col_chunk)
            cur = [jnp.dot(a, w[:, cols], preferred_element_type=F32) for a, w in zip(lhs, ws)]
            if pending is not None:
                finish(*pending)
            pending = (cur, cols)
        finish(*pending)
        return

    prods = [jnp.dot(x[...].astype(BF), w[...], preferred_element_type=F32) for x, w in zip(xs, ws)]
    if nk == 1:
        finish(prods)
        return
    k = pl.program_id(2)

    @pl.when(k == 0)
    def _():
        for a, p in zip(accs, prods):
            a[...] = p

    @pl.when(k > 0)
    def _():
        for a, p in zip(accs, prods):
            a[...] += p

    @pl.when(k == nk - 1)
    def _():
        finish([a[...] for a in accs])


def _mm(prods, epilogue, out_dtypes, *, n, tm, tn, tk=None, tiles=(), rows=(), name="mm",
        col_chunk=MXU_WIDTH, row_chunk=None, resident_weights=False):
    prods = [tuple(p) + (0,) * (4 - len(p)) for p in prods]
    m = prods[0][0].shape[0]
    tm = _tile(m, tm)
    ks = [p[0].shape[1] for p in prods]
    nk = 1 if tk is None else ks[0] // tk
    if nk > 1:
        assert all(k == ks[0] for k in ks) and ks[0] % tk == 0
    assert n % tn == 0 and m % tm == 0
    grid = (n // tn, m // tm, nk)
    in_specs, est = [], 0
    for k_dim in ks:
        bk = k_dim if nk == 1 else tk
        in_specs.append(pl.BlockSpec((tm, bk), lambda j, i, k: (i, k)))
    weights = []
    for (x, w, off, row_off), k_dim in zip(prods, ks):
        bk = k_dim if nk == 1 else tk
        assert off % tn == 0 and row_off % bk == 0
        if resident_weights:
            assert nk == 1 and n == tn
        spec, operand = _weight_spec(w, bk, tn, lambda k, rb=row_off // bk: k + rb,
                                     lambda j, ob=off // tn: j + ob, single_buffer=resident_weights)
        in_specs.append(spec)
        weights.append(operand)
        est += 2 * _nbytes((tm, bk), x.dtype) + (1 if resident_weights else 2) * _nbytes((bk, tn), w.dtype)
    for t in tiles:
        in_specs.append(pl.BlockSpec((tm, tn), lambda j, i, k: (i, j)))
        est += 2 * _nbytes((tm, tn), t.dtype)
    for r in rows:
        in_specs.append(pl.BlockSpec((1, tn), lambda j, i, k: (0, j)))
    out_specs = [pl.BlockSpec((tm, tn), lambda j, i, k: (i, j)) for _ in out_dtypes]
    out_shape = [jax.ShapeDtypeStruct((m, n), dt) for dt in out_dtypes]
    est += sum(2 * _nbytes((tm, tn), dt) for dt in out_dtypes)
    est += (len(prods) + 2) * _nbytes((tm, tn), F32)
    scratch = [pltpu.VMEM((tm, tn), F32) for _ in prods] if nk > 1 else []
    cast_weights = any(p[1].dtype != BF for p in prods)
    if cast_weights:
        assert nk == 1 and all(p[1].dtype == F32 for p in prods)
        scratch = [pltpu.VMEM((k_dim, tn), BF) for k_dim in ks]
        est += sum(_nbytes((k_dim, tn), BF) for k_dim in ks)
    body = functools.partial(_mm_body, nprod=len(prods), ntile=len(tiles), nrow=len(rows),
                             nout=len(out_dtypes), nk=nk, epilogue=epilogue, col_chunk=col_chunk,
                             row_chunk=row_chunk, cast_weights=cast_weights)
    return pl.pallas_call(
        body, name=name, grid=grid, in_specs=in_specs, out_specs=out_specs, out_shape=out_shape,
        scratch_shapes=scratch,
        compiler_params=_params(("parallel", "arbitrary", "arbitrary"), est),
    )(*[p[0] for p in prods], *weights, *tiles, *rows)


def _row(v):
    return v.reshape(1, -1).astype(F32)


class _LayerOf:
    def __init__(self, stack, layer):
        self.stack, self.layer = stack, layer
        self.shape, self.dtype = stack.shape[1:], stack.dtype


class _Layers:
    def __init__(self, stack):
        self.stack = stack

    def __getitem__(self, layer):
        return _LayerOf(self.stack, layer)


def _weight_spec(w, rows, cols, row_block, col_block, single_buffer=False):
    mode = dict(pipeline_mode=pl.Buffered(1)) if single_buffer else {}
    if isinstance(w, _LayerOf):
        return (pl.BlockSpec((None, rows, cols), lambda j, i, k=0: (w.layer, row_block(k), col_block(j)),
                             **mode), w.stack)
    return pl.BlockSpec((rows, cols), lambda j, i, k=0: (row_block(k), col_block(j)), **mode), w


def _gate_body(u_ref, va_ref, ws_ref, bs_ref, o_ref, *, blk, nblk):
    ti = lax.broadcasted_iota(jnp.int32, (blk, blk), 0) // CHUNK
    si = lax.broadcasted_iota(jnp.int32, (blk, blk), 1) // CHUNK
    allowed = ti >= si
    for h in range(A_HEADS):
        w = jnp.where(allowed, ws_ref[h], 0.0).astype(BF)
        bias = bs_ref[h]
        for nb in range(nblk):
            rs = slice(nb * blk, (nb + 1) * blk)
            cs = slice(h * A_DIM, (h + 1) * A_DIM)
            mixed = jnp.dot(w, va_ref[rs, cs], preferred_element_type=F32) + bias
            o_ref[rs, cs] = (u_ref[rs, cs].astype(F32) * mixed).astype(o_ref.dtype)


def _spatial_gate(u, va, w_s, b_s, blk):
    m = u.shape[0]
    tm = _tile(m, 512)
    nblk = tm // blk
    est = 2 * 3 * _nbytes((tm, A_WIDTH), BF) + 2 * _nbytes((A_HEADS, blk, blk), F32)
    return pl.pallas_call(
        functools.partial(_gate_body, blk=blk, nblk=nblk),
        name="spatial_gate",
        grid=(m // tm,),
        in_specs=[pl.BlockSpec((tm, A_WIDTH), lambda i: (i, 0)),
                  pl.BlockSpec((tm, A_WIDTH), lambda i: (i, 0)),
                  pl.BlockSpec((A_HEADS, blk, blk), lambda i: (0, 0, 0)),
                  pl.BlockSpec((A_HEADS, blk, 1), lambda i: (0, 0, 0))],
        out_specs=pl.BlockSpec((tm, A_WIDTH), lambda i: (i, 0)),
        out_shape=jax.ShapeDtypeStruct((m, A_WIDTH), BF),
        compiler_params=_params(("parallel",), est),
    )(u, va, w_s.astype(F32), b_s.astype(F32)[:, :, None])


def _later_keys(nkeys):
    jj = lax.broadcasted_iota(jnp.int32, (nkeys, nkeys), 0)
    ss = lax.broadcasted_iota(jnp.int32, (nkeys, nkeys), 1)
    return jnp.where(jj > ss, 1.0, 0.0).astype(BF)


def _sb_chain(qh, kb, valid, later, carry, keys_on_lanes=False):
    if keys_on_lanes:
        z = jnp.dot(qh, kb, preferred_element_type=F32)
    else:
        z = lax.dot_general(qh, kb, NT_DIMS, preferred_element_type=F32)
    yield
    neg_z = -z
    log_fail = jnp.minimum(neg_z, 0.0) - jnp.log(1.0 + jnp.exp(jnp.minimum(z, neg_z)))
    log_hit = z + log_fail
    if valid is not None:
        log_fail = jnp.where(valid, log_fail, 0.0)
    part = log_fail.astype(BF)
    after = jnp.dot(part, later, preferred_element_type=F32)
    rest = log_fail
    for _ in range(SB_CUMSUM_PASSES - 1):
        rest = rest - part.astype(F32)
        part = rest.astype(BF)
        after = after + jnp.dot(part, later, preferred_element_type=F32)
    yield
    total = after[:, 0:1] + log_fail[:, 0:1]
    if carry is None:
        return log_hit + after, total
    w = jnp.exp(log_hit + after + carry)
    if valid is not None:
        w = jnp.where(valid, w, 0.0)
    return w.astype(BF), total


def _sb_cache_body(q_ref, kn_ref, vn_ref, kc_ref, vc_ref, o_ref, c_ref, acc_ref, *, n_steps):
    step = pl.program_id(1)
    t = q_ref.shape[0]
    head_cols = lambda h: slice(h * B_HEAD_DIM, (h + 1) * B_HEAD_DIM)

    def attend(key_of, val_of, nkeys, valid, keys_on_lanes):
        later = _later_keys(nkeys)
        gens = [_sb_chain(q_ref[:, head_cols(h)], key_of(h).astype(BF), valid, later, c_ref[h], keys_on_lanes)
                for h in range(B_HEADS)]
        for h, (w, total) in enumerate(_run_in_lockstep(gens)):
            vals = val_of(h).astype(BF)
            if keys_on_lanes:
                out = lax.dot_general(w, vals, NT_DIMS, preferred_element_type=F32)
            else:
                out = jnp.dot(w, vals, preferred_element_type=F32)
            acc_ref[h] = acc_ref[h] + out
            c_ref[h] = c_ref[h] + total

    @pl.when(step == 0)
    def _():
        c_ref[...] = jnp.zeros_like(c_ref)
        acc_ref[...] = jnp.zeros_like(acc_ref)
        tt = lax.broadcasted_iota(jnp.int32, (t, t), 0)
        sk = lax.broadcasted_iota(jnp.int32, (t, t), 1)
        attend(lambda h: kn_ref[:, head_cols(h)], lambda h: vn_ref[:, head_cols(h)], t, sk < tt, False)

    attend(lambda h: kc_ref[h], lambda h: vc_ref[h], kc_ref.shape[2], None, True)

    @pl.when(step == n_steps - 1)
    def _():
        for h in range(B_HEADS):
            o_ref[:, head_cols(h)] = acc_ref[h].astype(o_ref.dtype)


def _stick_breaking_cache(q, k_new, v_new, k_cache, v_cache, layer, tk):
    bsz, t, _ = q.shape
    p_len = k_cache.shape[4]
    n_steps = p_len // tk
    new_spec = pl.BlockSpec((None, t, B_WIDTH), lambda b, s: (b, 0, 0))
    cache_spec = pl.BlockSpec((None, None, B_HEADS, B_HEAD_DIM, tk),
                              lambda b, s: (layer, b, 0, 0, n_steps - 1 - s))
    est = (2 * 2 * _nbytes((tk, B_HEADS, LANES), F32) + 8 * _nbytes((t, B_WIDTH), F32)
           + 12 * B_HEADS * _nbytes((t, tk), F32))
    return pl.pallas_call(
        functools.partial(_sb_cache_body, n_steps=n_steps),
        name="stick_breaking_cache",
        grid=(bsz, n_steps),
        in_specs=[new_spec, new_spec, new_spec, cache_spec, cache_spec],
        out_specs=new_spec,
        out_shape=jax.ShapeDtypeStruct((bsz, t, B_WIDTH), BF),
        scratch_shapes=[pltpu.VMEM((B_HEADS, t, 1), F32), pltpu.VMEM((B_HEADS, t, B_HEAD_DIM), F32)],
        compiler_params=_params(("parallel", "arbitrary"), est),
    )(q, k_new, v_new, k_cache, v_cache)


def _sb_body(q_ref, kd_ref, vd_ref, kp_ref, vp_ref, o_ref, *, sub, nq, tk, blocks_per_iter, past_iters,
             unroll):
    head_a_q = lax.broadcasted_iota(jnp.int32, (sub, LANES), 1) < B_HEAD_DIM
    q_subs = []
    for i in range(nq):
        q = q_ref[i * sub:(i + 1) * sub, :]
        zero_q = jnp.zeros_like(q)
        q_subs.append((jnp.where(head_a_q, q, zero_q), jnp.where(head_a_q, zero_q, q)))

    def attend(k_blocks, v_blocks, masks, carries, accs):
        kbs = [k.astype(BF) for k in k_blocks]
        laters, v_stacks = {}, []
        for kb, v_blk in zip(kbs, v_blocks):
            nkeys = kb.shape[0]
            if nkeys not in laters:
                laters[nkeys] = _later_keys(nkeys)
            vb = v_blk.astype(BF)
            head_a_k = lax.broadcasted_iota(jnp.int32, (nkeys, LANES), 1) < B_HEAD_DIM
            zero_v = jnp.zeros_like(vb)
            v_stacks.append(jnp.concatenate([jnp.where(head_a_k, vb, zero_v), jnp.where(head_a_k, zero_v, vb)],
                                            axis=0))
        early = len(kbs) == 1
        keys, gens = [], []
        for i in range(nq):
            for j, kb in enumerate(kbs):
                if isinstance(masks[i][j], str):
                    continue
                for h in range(2):
                    keys.append((i, j, h))
                    gens.append(_sb_chain(q_subs[i][h], kb, masks[i][j], laters[kb.shape[0]],
                                          carries[i][h] if early else None))
        scores = dict(zip(keys, _run_in_lockstep(gens, lag=SB_CHAIN_LAG)))
        new_carries, new_accs = [], []
        for i in range(nq):
            cs = list(carries[i])
            wts, vals = [], []
            for j in range(len(kbs)):
                if isinstance(masks[i][j], str):
                    continue
                for h in range(2):
                    w, total = scores[(i, j, h)]
                    if not early:
                        w = jnp.exp(w + cs[h])
                        if masks[i][j] is not None:
                            w = jnp.where(masks[i][j], w, 0.0)
                        w = w.astype(BF)
                    cs[h] = cs[h] + total
                    wts.append(w)
                vals.append(v_stacks[j])
            new_carries.append(cs)
            new_accs.append(accs[i] + jnp.dot(jnp.concatenate(wts, axis=1), jnp.concatenate(vals, axis=0),
                                              preferred_element_type=F32))
        return new_carries, new_accs

    k_diag = [kd_ref[j * sub:(j + 1) * sub, :] for j in reversed(range(nq))]
    v_diag = [vd_ref[j * sub:(j + 1) * sub, :] for j in reversed(range(nq))]
    tt = lax.broadcasted_iota(jnp.int32, (sub, sub), 0)
    sk = lax.broadcasted_iota(jnp.int32, (sub, sub), 1)
    causal = sk < tt
    masks = [[causal if j == i else (None if j < i else "skip") for j in reversed(range(nq))]
             for i in range(nq)]
    zeros_c = jnp.zeros((sub, 1), F32)
    carries, accs = attend(k_diag, v_diag, masks, [[zeros_c, zeros_c] for _ in range(nq)],
                           [jnp.zeros((sub, LANES), F32) for _ in range(nq)])
    n_iters = past_iters(pl.program_id(2))
    open_masks = [[None] * blocks_per_iter for _ in range(nq)]

    def step(it, state):
        carries = [list(state[2 * i:2 * i + 2]) for i in range(nq)]
        accs = list(state[2 * nq:])
        newest = n_iters - 1 - it
        offs = [pl.multiple_of((newest * blocks_per_iter + j) * tk, tk)
                for j in reversed(range(blocks_per_iter))]
        carries, accs = attend([kp_ref[pl.ds(o, tk), :] for o in offs], [vp_ref[pl.ds(o, tk), :] for o in offs],
                               open_masks, carries, accs)
        return tuple(c for pair in carries for c in pair) + tuple(accs)

    state = lax.fori_loop(0, n_iters, step, tuple(c for pair in carries for c in pair) + tuple(accs),
                          unroll=unroll)
    for i in range(nq):
        o_ref[i * sub:(i + 1) * sub, :] = state[2 * nq + i].astype(o_ref.dtype)


def _stick_breaking(q, k, v, *, tq, tk):
    bsz, t, _ = q.shape
    k_new, v_new, k_past, v_past = k, v, k, v
    past_spec = pl.BlockSpec((None, t, LANES), lambda b, p, i: (b, 0, p))
    sub = min(tq, SB_SUBTILE)
    nq = tq // sub
    blocks_per_iter = 1
    past_iters = lambda qi: qi * (tq // tk)
    est = (2 * 2 * _nbytes((t, LANES), k.dtype) + 2 * 4 * _nbytes((tq, LANES), F32)
           + 12 * 2 * nq * blocks_per_iter * _nbytes((sub, max(sub, tk)), F32))
    return pl.pallas_call(
        functools.partial(_sb_body, sub=sub, nq=nq, tk=tk, blocks_per_iter=blocks_per_iter,
                          past_iters=past_iters, unroll=None),
        name="stick_breaking_past",
        grid=(bsz, B_WIDTH // LANES, t // tq),
        in_specs=[pl.BlockSpec((None, tq, LANES), lambda b, p, i: (b, i, p)),
                  pl.BlockSpec((None, tq, LANES), lambda b, p, i: (b, i, p)),
                  pl.BlockSpec((None, tq, LANES), lambda b, p, i: (b, i, p)),
                  past_spec, past_spec],
        out_specs=pl.BlockSpec((None, tq, LANES), lambda b, p, i: (b, i, p)),
        out_shape=jax.ShapeDtypeStruct((bsz, t, B_WIDTH), BF),
        compiler_params=_params(("parallel", "parallel", "arbitrary"), est),
    )(q, k_new, v_new, k_past, v_past)


def _shift_body(x_ref, last_ref, mu_ref, *rest, tiles_per_seq):
    outs, carry = rest[:6], rest[6]
    i = pl.program_id(1)

    @pl.when(i % tiles_per_seq == 0)
    def _():
        carry[...] = jnp.broadcast_to(last_ref[...], carry.shape)

    x = x_ref[...]
    rows = lax.broadcasted_iota(jnp.int32, x.shape, 0)
    x_prev = jnp.where(rows == 0, carry[SUBLANES - 1:SUBLANES, :], pltpu.roll(x, 1, 0))
    carry[...] = x[x.shape[0] - SUBLANES:, :]
    xx = x_prev - x
    for j, o_ref in enumerate(outs):
        o_ref[...] = (x + xx * mu_ref[j:j + 1, :]).astype(o_ref.dtype)


def _token_shift(x, x_last, mu):
    bsz, t, d = x.shape
    tm = _tile(t, 256)
    est = 2 * _nbytes((tm, d), F32) + 12 * _nbytes((tm, d), BF) + 4 * _nbytes((tm, d), F32)
    return pl.pallas_call(
        functools.partial(_shift_body, tiles_per_seq=t // tm),
        name="token_shift",
        grid=(bsz, t // tm),
        in_specs=[pl.BlockSpec((None, tm, d), lambda b, i: (b, i, 0)),
                  pl.BlockSpec((None, 1, d), lambda b, i: (b, 0, 0)),
                  pl.BlockSpec((6, d), lambda b, i: (0, 0))],
        out_specs=[pl.BlockSpec((None, tm, d), lambda b, i: (b, i, 0)) for _ in range(6)],
        out_shape=[jax.ShapeDtypeStruct((bsz, t, d), BF) for _ in range(6)],
        scratch_shapes=[pltpu.VMEM((SUBLANES, d), F32)],
        compiler_params=_params(("parallel", "arbitrary"), est),
    )(x, x_last[:, None, :].astype(F32), mu.astype(F32))


def _wkv_body(r_ref, k_ref, v_ref, lw_ref, a_ref, g_ref, kk_ref, ka_ref, rk_ref, gg_ref, gb_ref,
              s0_ref, y_ref, s_ref):
    c = WKV_CHUNK
    n = 2 * c

    @pl.when(pl.program_id(1) == 0)
    def _():
        s_ref[...] = s0_ref[...]

    head_a = lax.broadcasted_iota(jnp.int32, (c, LANES), 1) < C_HEAD_DIM
    ri = lax.broadcasted_iota(jnp.int32, (n, n), 0)
    ci = lax.broadcasted_iota(jnp.int32, (n, n), 1)
    same_head = (ri // c) == (ci // c)
    strict = same_head & (ci < ri)
    inclusive = same_head & (ci <= ri)
    diag16 = (ri // 16) == (ci // 16)
    diag32 = (ri // 32) == (ci // 32)
    eye = jnp.where(ri == ci, 1.0, 0.0).astype(F32)
    seg_ones = jnp.where(same_head, 1.0, 0.0).astype(BF)
    ti = lax.broadcasted_iota(jnp.int32, (c, c), 0)
    si = lax.broadcasted_iota(jnp.int32, (c, c), 1)
    prefix = jnp.where(si <= ti, 1.0, 0.0).astype(BF)

    def split2(x):
        hi = x.astype(BF)
        return hi, (x - hi.astype(F32)).astype(BF)

    def seg_sum(x):
        hi, lo = split2(x)
        return (jnp.dot(hi, seg_ones, preferred_element_type=F32)
                + jnp.dot(lo, seg_ones, preferred_element_type=F32))

    def cumsum_time(x):
        hi, rest = split2(x)
        rest_f = x - hi.astype(F32)
        mid = rest_f.astype(BF)
        lo = (rest_f - mid.astype(F32)).astype(BF)
        return (jnp.dot(prefix, hi, preferred_element_type=F32)
                + jnp.dot(prefix, mid, preferred_element_type=F32)
                + jnp.dot(prefix, lo, preferred_element_type=F32))

    def stack(x):
        zero = jnp.zeros_like(x)
        return jnp.concatenate([jnp.where(head_a, x, zero), jnp.where(head_a, zero, x)], axis=0)

    def pair(p, s):
        sl = pl.ds(pl.multiple_of(p * LANES, LANES), LANES)
        r = r_ref[:, sl]
        k = k_ref[:, sl]
        v = v_ref[:, sl]
        lw = lw_ref[:, sl]
        a = a_ref[:, sl]
        kkp = k * kk_ref[:, sl]
        norm = jnp.sqrt(seg_sum(kkp * kkp))
        yield
        kk = kkp / jnp.maximum(norm, 1e-12)
        k2 = k * (1.0 + (a - 1.0) * ka_ref[:, sl])
        bonus = seg_sum(r * k2 * rk_ref[:, sl]) * v
        yield
        cs = cumsum_time(lw)
        yield
        e_pos = jnp.exp(cs)
        e_neg = jnp.exp(-cs)
        a_t = -kk * e_pos * jnp.exp(-lw)
        b_t = kk * a * e_neg
        k_t = k2 * e_neg
        r_t = r * e_pos
        la, lr = stack(a_t), stack(r_t)
        rb, rk = stack(b_t), stack(k_t)
        vs = stack(v)
        q = lax.dot_general(jnp.concatenate([la, lr], axis=0).astype(BF),
                            jnp.concatenate([rb, rk], axis=0).astype(BF),
                            NT_DIMS, preferred_element_type=F32)
        yield
        p_ab = jnp.where(strict, q[:n, :n], 0.0)
        a_ak = jnp.where(strict, q[:n, n:], 0.0)
        a_rb = jnp.where(inclusive, q[n:, :n], 0.0)
        a_rk = jnp.where(inclusive, q[n:, n:], 0.0)
        akv = _bdot(a_ak, vs)
        yield
        pd = jnp.where(diag16, p_ab, 0.0)
        inv = eye + pd
        pw = pd
        for _ in range(3):
            pw = _bdot(pw, pw)
            yield
            inv = inv + _bdot(pw, inv)
            yield
        for off_diag in (jnp.where(diag32 & (~diag16), p_ab, 0.0), jnp.where(diag32, 0.0, p_ab)):
            right = _bdot(off_diag, inv)
            yield
            inv = inv + _bdot(inv, right)
            yield
        wu = _bdot(inv, jnp.concatenate([la, akv], axis=1))
        yield
        w_m, u_m = wu[:, :LANES], wu[:, LANES:]
        wr = lax.dot_general(jnp.concatenate([w_m, lr], axis=0).astype(BF), s.astype(BF),
                             NT_DIMS, preferred_element_type=F32)
        yield
        sa = wr[:n] + u_m
        sv = jnp.concatenate([sa, vs], axis=0)
        y_st = wr[n:] + _bdot(jnp.concatenate([a_rb, a_rk], axis=1), sv)
        g_end = e_pos[c - 1:c, :]
        rg = jnp.concatenate([rb, rk], axis=0) * g_end
        s_new = s * g_end + lax.dot_general(sv.astype(BF), rg.astype(BF), TN_DIMS,
                                            preferred_element_type=F32)
        yield
        y = y_st[:c] + y_st[c:]
        mean = seg_sum(y) * (1.0 / C_HEAD_DIM)
        yield
        d = y - mean
        var = seg_sum(d * d) * (1.0 / C_HEAD_DIM)
        yield
        yn = d * lax.rsqrt(var + C_GN_EPS) * gg_ref[:, sl] + gb_ref[:, sl]
        return s_new, ((yn + bonus) * g_ref[:, sl]).astype(y_ref.dtype)

    def group(it, carry):
        pairs = [it * WKV_PAIRS_PER_STEP + u for u in range(WKV_PAIRS_PER_STEP)]
        states = [s_ref[p] for p in pairs]
        results = _run_in_lockstep([pair(p, s) for p, s in zip(pairs, states)])
        for p, (s_new, y_out) in zip(pairs, results):
            s_ref[p] = s_new
            y_ref[:, pl.ds(pl.multiple_of(p * LANES, LANES), LANES)] = y_out
        return carry

    lax.fori_loop(0, N_PAIRS // WKV_PAIRS_PER_STEP, group, 0)


def _pair_state(s):
    bsz = s.shape[0]
    s = s.reshape(bsz, N_PAIRS, 2, C_HEAD_DIM, C_HEAD_DIM).astype(F32)
    z = jnp.zeros_like(s[:, :, 0])
    top = jnp.concatenate([s[:, :, 0], z], axis=-1)
    bot = jnp.concatenate([z, s[:, :, 1]], axis=-1)
    return jnp.concatenate([top, bot], axis=-2)


def _unpair_state(sp):
    bsz = sp.shape[0]
    a = sp[:, :, :C_HEAD_DIM, :C_HEAD_DIM]
    b = sp[:, :, C_HEAD_DIM:, C_HEAD_DIM:]
    return jnp.stack([a, b], axis=2).reshape(bsz, C_HEADS, C_HEAD_DIM, C_HEAD_DIM)


def _wkv(r, k, v, lw, a, g, s0, k_k, k_a, r_k, gn_g, gn_b):
    bsz, t, d = r.shape
    c = WKV_CHUNK
    seq = pl.BlockSpec((None, c, d), lambda b, i: (b, i, 0))
    vec = pl.BlockSpec((1, d), lambda b, i: (0, 0))
    st = pl.BlockSpec((None, N_PAIRS, HEAD_PAIR, HEAD_PAIR), lambda b, i: (b, 0, 0, 0))
    est = 2 * 6 * _nbytes((c, d), F32) + 4 * _nbytes((N_PAIRS, HEAD_PAIR, HEAD_PAIR), F32) + (8 << 20)
    y, s_new = pl.pallas_call(
        _wkv_body,
        name="wkv7_chunked",
        grid=(bsz, t // c),
        in_specs=[seq] * 6 + [vec] * 5 + [st],
        out_specs=[seq, st],
        out_shape=[jax.ShapeDtypeStruct((bsz, t, d), BF),
                   jax.ShapeDtypeStruct((bsz, N_PAIRS, HEAD_PAIR, HEAD_PAIR), F32)],
        compiler_params=_params(("parallel", "arbitrary"), est),
    )(r, k, v, lw, a, g, _row(k_k), _row(k_a), _row(r_k), _row(gn_g), _row(gn_b), _pair_state(s0))
    return y, _unpair_state(s_new)


def _ffn_in_body(x_ref, wg_f32_ref, wu_f32_ref, cw_ref, cb_ref, prev_ref, act_ref, st_ref, tail_ref,
                 wg_ref, wu_ref, *, seq_len, tm):
    @pl.when(pl.program_id(1) == 0)
    def _():
        wg_ref[...] = wg_f32_ref[...].astype(BF)
        wu_ref[...] = wu_f32_ref[...].astype(BF)

    x = x_ref[...]
    tn = act_ref.shape[1]
    cn = min(tn, MXU_WIDTH)
    last2, last1 = SUBLANES - 2, SUBLANES - 1
    carried = seq_len >= tm
    if carried:
        @pl.when(pl.program_id(1) % (seq_len // tm) == 0)
        def _():
            tail_ref[...] = prev_ref[0]

    def conv_gate(hg, hu, cols):
        rows = lax.broadcasted_iota(jnp.int32, hg.shape, 0)
        if carried:
            tail = tail_ref[:, cols]
            p0, p1 = tail[last2:last2 + 1, :], tail[last1:last1 + 1, :]
            pos = rows
            new_tail = hg[tm - SUBLANES:, :]
            tail_ref[:, cols] = new_tail
            st_ref[0, :, cols] = new_tail
        else:
            nseq = tm // seq_len
            prev = prev_ref[:, :, cols]
            spread = lambda rw: jnp.broadcast_to(rw, (nseq, seq_len, cn)).reshape(tm, cn)
            p0, p1 = spread(prev[:, last2:last2 + 1, :]), spread(prev[:, last1:last1 + 1, :])
            pos = rows % seq_len
            st_ref[:, :, cols] = hg.reshape(nseq, seq_len, cn)[:, seq_len - SUBLANES:, :]
        s1 = jnp.where(pos == 0, p1, pltpu.roll(hg, 1, 0))
        s2 = jnp.where(pos == 0, p0, jnp.where(pos == 1, p1, pltpu.roll(hg, 2, 0)))
        cw = cw_ref[:, cols]
        hc = cb_ref[:, cols] + s2 * cw[0:1, :] + s1 * cw[1:2, :] + hg * cw[2:3, :]
        act_ref[:, cols] = (jax.nn.gelu(hc) * hu).astype(act_ref.dtype)

    pending = None
    for ch in range(tn // cn):
        cols = slice(ch * cn, (ch + 1) * cn)
        hg = jnp.dot(x, wg_ref[:, cols], preferred_element_type=F32)
        hu = jnp.dot(x, wu_ref[:, cols], preferred_element_type=F32)
        if pending is not None:
            conv_gate(*pending)
        pending = (hg, hu, cols)
    conv_gate(*pending)


def _ffn_in(xb, w_in, conv_w, conv_b, conv_prev, seq_len):
    m, d = xb.shape
    nseq_total = m // seq_len
    tn = 512
    tm = _tile(m, 1024)
    nj = D_FF // tn
    prev8 = jnp.pad(conv_prev.astype(F32), ((0, 0), (SUBLANES - (CONV_W - 1), 0), (0, 0)))
    if seq_len >= tm:
        per = seq_len // tm
        nb = 1
        st_map = lambda j, i: (i // per, 0, j)
    else:
        nb = tm // seq_len
        st_map = lambda j, i: (i, 0, j)
    st_spec = pl.BlockSpec((nb, SUBLANES, tn), st_map)
    assert w_in.dtype == F32
    est = (2 * _nbytes((tm, d), BF) + 4 * _nbytes((d, tn), F32) + 2 * _nbytes((d, tn), BF)
           + 2 * _nbytes((tm, tn), BF) + 8 * _nbytes((tm, MXU_WIDTH), F32))
    gate_spec, w_operand = _weight_spec(w_in, d, tn, lambda k: 0, lambda j: j)
    up_spec, _ = _weight_spec(w_in, d, tn, lambda k: 0, lambda j: j + nj)
    act, tails = pl.pallas_call(
        functools.partial(_ffn_in_body, seq_len=seq_len, tm=tm),
        name="ffn_in_conv",
        grid=(nj, m // tm),
        in_specs=[pl.BlockSpec((tm, d), lambda j, i: (i, 0)),
                  gate_spec,
                  up_spec,
                  pl.BlockSpec((CONV_W, tn), lambda j, i: (0, j)),
                  pl.BlockSpec((1, tn), lambda j, i: (0, j)),
                  st_spec],
        out_specs=[pl.BlockSpec((tm, tn), lambda j, i: (i, j)), st_spec],
        out_shape=[jax.ShapeDtypeStruct((m, D_FF), BF),
                   jax.ShapeDtypeStruct((nseq_total, SUBLANES, D_FF), F32)],
        scratch_shapes=[pltpu.VMEM((SUBLANES, tn), F32), pltpu.VMEM((d, tn), BF), pltpu.VMEM((d, tn), BF)],
        compiler_params=_params(("parallel", "arbitrary"), est),
    )(xb, w_operand, w_operand, conv_w.astype(F32), _row(conv_b), prev8)
    return act, tails[:, SUBLANES - (CONV_W - 1):, :]


def _ln_residual_epilogue(accs, tiles, rows):
    mix = accs[0]
    for extra in accs[1:]:
        mix = mix + extra
    y = _layernorm(DN_ALPHA * tiles[0] + mix, rows[0], rows[1])
    return y, y


def _identity_epilogue(accs, tiles, rows):
    return (accs[0],)


def _even_mixer(x, xb, seq_len, w_in, lnv_g, lnv_b, w_s, b_s, w_o, ln_g, ln_b, cache):
    m = x.shape[0]
    bsz = m // seq_len
    proj = lambda name, off, ep, dts, rows=(), col_chunk=MXU_WIDTH: _mm(
        [(xb, w_in, off)], ep, dts, n=1024, tm=512, tn=1024, rows=rows, name=name, col_chunk=col_chunk)
    (u,) = proj("even_u", 0, lambda a, t, r: (jax.nn.gelu(a[0]),), [BF])

    def va_epilogue(a, t, r):
        va = _layernorm(jax.nn.gelu(a[0]), r[0], r[1])
        return va, va
    va, vab = proj("even_va", A_WIDTH, va_epilogue, [F32, BF], rows=(_row(lnv_g), _row(lnv_b)),
                   col_chunk=None)
    o = 2 * A_WIDTH
    (q,) = proj("even_q", o, lambda a, t, r: (a[0] * (B_HEAD_DIM ** -0.5),), [BF])
    (k,) = proj("even_k", o + B_WIDTH, _identity_epilogue, [F32])
    (v,) = proj("even_v", o + 2 * B_WIDTH, _identity_epilogue, [F32])

    blk = min(seq_len, A_BLOCK)
    a_out = _spatial_gate(u, vab, w_s[:, :blk, :blk], b_s[:, :blk], blk)

    q3 = q.reshape(bsz, seq_len, B_WIDTH)
    k3 = k.reshape(bsz, seq_len, B_WIDTH)
    v3 = v.reshape(bsz, seq_len, B_WIDTH)
    if cache is None:
        tq = _tile(seq_len, 2 * SB_BLOCK)
        b_out = _stick_breaking(q3, k3, v3, tq=tq, tk=min(tq, SB_BLOCK))
    else:
        k_cache, v_cache, layer = cache
        b_out = _stick_breaking_cache(q3, k3, v3, k_cache, v_cache, layer, _tile(k_cache.shape[4], SB_BLOCK))
    b_out = b_out.reshape(m, B_WIDTH)

    x1, x1b = _mm([(a_out, w_o, 0), (b_out, w_o, 0, A_WIDTH)], _ln_residual_epilogue, [F32, BF],
                  n=D_MODEL, tm=2 * LN_ROW_CHUNK, tn=D_MODEL, tiles=(x,), rows=(_row(ln_g), _row(ln_b)),
                  name="even_out_ln", col_chunk=None, row_chunk=LN_ROW_CHUNK, resident_weights=True)
    return x1, x1b, k, v, va


def _rwkv_mixer(x, seq_len, x_last, s0, v_first, P, ln_g, ln_b):
    m = x.shape[0]
    bsz = m // seq_len
    mixes = _token_shift(x.reshape(bsz, seq_len, D_MODEL), x_last, P['mu'])
    xr, xw, xk, xv, xa, xg = [a.reshape(m, D_MODEL) for a in mixes]
    big = lambda xin, w: _mm([(xin, w, 0)], _identity_epilogue, [F32], n=D_MODEL, tm=512, tn=1024,
                             name="rwkv_proj")[0]
    r = big(xr, P['w_r'])
    k = big(xk, P['w_k'])

    def lora_in(xin, w, act):
        n = w.shape[1]
        return _mm([(xin, w, 0)], lambda a, t, r_: (act(a[0]),), [BF], n=n, tm=512, tn=n,
                   name="rwkv_lora_in")[0]

    def lora_out(h, w, bias, act):
        return _mm([(h, w, 0)], lambda a, t, r_: (act(r_[0] + a[0]),), [F32], n=D_MODEL, tm=512,
                   tn=1024, rows=(_row(bias),), name="rwkv_lora_out")[0]

    hw = lora_in(xw, P['w1'], jnp.tanh)
    lw = lora_out(hw, P['w2'], P['w0'],
                  lambda pre: -jnp.exp(-(jnp.maximum(-pre, 0.0) + jnp.log(1.0 + jnp.exp(-jnp.abs(pre)))) - 0.5))
    ha = lora_in(xa, P['a1'], lambda z: z)
    a = lora_out(ha, P['a2'], P['a0'], jax.nn.sigmoid)
    hg = lora_in(xg, P['g1'], jax.nn.sigmoid)
    g = _mm([(hg, P['g2'], 0)], _identity_epilogue, [F32], n=D_MODEL, tm=512, tn=1024, name="rwkv_gate")[0]
    if P['v1'] is None:
        v = big(xv, P['w_v'])
        v_first = v
    else:
        hv = lora_in(xv, P['v1'], lambda z: z)

        def v_epilogue(accs, tiles, rows):
            v_new = accs[0]
            return (v_new + (tiles[0] - v_new) * jax.nn.sigmoid(rows[0] + accs[1]),)
        v = _mm([(xv, P['w_v'], 0), (hv, P['v2'], 0)], v_epilogue, [F32], n=D_MODEL, tm=512, tn=1024,
                tiles=(v_first,), rows=(_row(P['v0']),), name="rwkv_v_residual")[0]
    to3 = lambda z: z.reshape(bsz, seq_len, D_MODEL)
    yg, s_new = _wkv(to3(r), to3(k), to3(v), to3(lw), to3(a), to3(g), s0,
                     P['k_k'], P['k_a'], P['r_k'], P['gn_g'], P['gn_b'])
    x1, x1b = _mm([(yg.reshape(m, D_MODEL), P['w_o'], 0)], _ln_residual_epilogue, [F32, BF],
                  n=D_MODEL, tm=2 * LN_ROW_CHUNK, tn=D_MODEL, tiles=(x,), rows=(_row(ln_g), _row(ln_b)),
                  name="rwkv_out_ln", col_chunk=None, row_chunk=LN_ROW_CHUNK, resident_weights=True)
    return x1, x1b, s_new, v_first


def _ffn_ple(x1, x1b, seq_len, p, conv_prev, w_in, conv_w, conv_b, w_out, ln_g, ln_b, ple_gate, ple_proj):
    act, conv_rows = _ffn_in(x1b, w_in, conv_w, conv_b, conv_prev, seq_len)
    x2, x2b = _mm([(act, w_out, 0)], _ln_residual_epilogue, [F32, BF], n=D_MODEL, tm=512, tn=D_MODEL,
                  tk=D_FF // 4, tiles=(x1,), rows=(_row(ln_g), _row(ln_b)), name="ffn_out_ln", col_chunk=None)

    def ple_epilogue(accs, tiles, rows):
        y = tiles[0] + jax.nn.sigmoid(accs[0]) * accs[1]
        return y, y
    x3, x3b = _mm([(x2b, ple_gate, 0), (p, ple_proj, 0)], ple_epilogue, [F32, BF], n=D_MODEL, tm=512,
                  tn=1024, tiles=(x2,), name="ple")
    return x3, x3b, conv_rows


def _trunk(x, p, W, caches, wkv0, shift0, conv0):
    bsz, seq_len, _ = x.shape
    m = bsz * seq_len
    x = x.reshape(m, D_MODEL).astype(F32)
    xb = x.astype(BF)
    new_k, new_v, new_va, new_wkv, new_shift, new_conv = [], [], [], [], [], []
    v_first = None
    for i in range(DEPTH):
        j = i // 2
        if i % 2 == 0:
            cache = None if caches is None else (caches[0], caches[1], j)
            x1, x1b, k, v, va = _even_mixer(
                x, xb, seq_len, W['even_w_in'][j], W['even_lnv_g'][j], W['even_lnv_b'][j],
                W['even_w_s'][j], W['even_b_s'][j], W['even_w_o'][j], W['ln1_g'][i], W['ln1_b'][i], cache)
            new_k.append(k.reshape(bsz, seq_len, B_HEADS, B_HEAD_DIM))
            new_v.append(v.reshape(bsz, seq_len, B_HEADS, B_HEAD_DIM))
            new_va.append(va.reshape(bsz, seq_len, A_HEADS, A_DIM))
        else:
            P = dict(mu=W['c_mu'][j], w_r=W['c_w_r'][j], w_k=W['c_w_k'][j], w_v=W['c_w_v'][j],
                     w_o=W['c_w_o'][j], w0=W['c_w0'][j], w1=W['c_w1'][j], w2=W['c_w2'][j],
                     a0=W['c_a0'][j], a1=W['c_a1'][j], a2=W['c_a2'][j], g1=W['c_g1'][j], g2=W['c_g2'][j],
                     k_k=W['c_k_k'][j], k_a=W['c_k_a'][j], r_k=W['c_r_k'][j], gn_g=W['c_gn_g'][j],
                     gn_b=W['c_gn_b'][j],
                     v0=None if j == 0 else W['c_v0'][j - 1], v1=None if j == 0 else W['c_v1'][j - 1],
                     v2=None if j == 0 else W['c_v2'][j - 1])
            x1, x1b, s_new, v_first = _rwkv_mixer(x, seq_len, shift0[j], wkv0[j], v_first, P,
                                                  W['ln1_g'][i], W['ln1_b'][i])
            new_wkv.append(s_new)
            new_shift.append(x.reshape(bsz, seq_len, D_MODEL)[:, -1])
        x, xb, conv_rows = _ffn_ple(
            x1, x1b, seq_len, p[i].reshape(m, -1), conv0[i], W['ffn_w_in'][i], W['ffn_conv_w'][i],
            W['ffn_conv_b'][i], W['ffn_w_out'][i], W['ln2_g'][i], W['ln2_b'][i], W['ple_gate'][i],
            W['ple_proj'][i])
        new_conv.append(conv_rows)
    st = lambda l: jnp.stack(l) if l else None
    return (x.reshape(bsz, seq_len, D_MODEL), st(new_k), st(new_v), st(new_va), st(new_wkv),
            st(new_shift), st(new_conv))


_MATRICES = ('even_w_in', 'even_w_o', 'c_w_r', 'c_w_k', 'c_w_v', 'c_w_o', 'c_w1', 'c_w2', 'c_a1', 'c_a2',
             'c_v1', 'c_v2', 'c_g1', 'c_g2', 'ffn_w_in', 'ffn_w_out', 'ple_proj', 'ple_gate')
_PRECAST = ('even_w_o', 'c_w_o', 'ffn_w_out')


def kernel(x_prompt, x_sample, cache_sb_k, cache_sb_v, state_rwkv_wkv, state_rwkv_shift, state_ffn_conv, p_prompt, p_sample, even_w_in, even_lnv_g, even_lnv_b, even_w_s, even_b_s, even_w_o, c_mu, c_w_r, c_w_k, c_w_v, c_w_o, c_w0, c_w1, c_w2, c_a0, c_a1, c_a2, c_v0, c_v1, c_v2, c_g1, c_g2, c_k_k, c_k_a, c_r_k, c_gn_g, c_gn_b, ffn_w_in, ffn_conv_w, ffn_conv_b, ffn_w_out, ln1_g, ln1_b, ln2_g, ln2_b, ple_proj, ple_gate):
    W = dict(even_w_in=even_w_in, even_lnv_g=even_lnv_g, even_lnv_b=even_lnv_b, even_w_s=even_w_s,
             even_b_s=even_b_s, even_w_o=even_w_o, c_mu=c_mu, c_w_r=c_w_r, c_w_k=c_w_k,
             c_w_v=c_w_v, c_w_o=c_w_o, c_w0=c_w0, c_w1=c_w1, c_w2=c_w2, c_a0=c_a0, c_a1=c_a1,
             c_a2=c_a2, c_v0=c_v0, c_v1=c_v1, c_v2=c_v2, c_g1=c_g1, c_g2=c_g2, c_k_k=c_k_k,
             c_k_a=c_k_a, c_r_k=c_r_k, c_gn_g=c_gn_g, c_gn_b=c_gn_b, ffn_w_in=ffn_w_in,
             ffn_conv_w=ffn_conv_w, ffn_conv_b=ffn_conv_b, ffn_w_out=ffn_w_out, ln1_g=ln1_g,
             ln1_b=ln1_b, ln2_g=ln2_g, ln2_b=ln2_b, ple_proj=ple_proj, ple_gate=ple_gate)
    for name in _MATRICES:
        W[name] = _Layers(W[name].astype(BF) if name in _PRECAST else W[name])
    bp = x_prompt.shape[0]
    n_odd, n_even = state_rwkv_wkv.shape[0], cache_sb_k.shape[0]
    cache_sb_k = jnp.transpose(cache_sb_k, (0, 1, 3, 4, 2))
    cache_sb_v = jnp.transpose(cache_sb_v, (0, 1, 3, 4, 2))
    wkv_zero = jnp.zeros((n_odd, bp, C_HEADS, C_HEAD_DIM, C_HEAD_DIM), F32)
    shift_zero = jnp.zeros((n_odd, bp, D_MODEL), F32)
    conv_zero = jnp.zeros((DEPTH, bp, CONV_W - 1, D_FF), F32)
    y_p, k_p, v_p, _, wkv_p, shift_p, conv_p = _trunk(
        x_prompt, p_prompt, W, None, wkv_zero, shift_zero, conv_zero)
    y_s, k_s, v_s, va_s, wkv_s, shift_s, conv_s = _trunk(
        x_sample, p_sample, W, (cache_sb_k, cache_sb_v), state_rwkv_wkv, state_rwkv_shift, state_ffn_conv)
    return (y_p, y_s, k_p, v_p, wkv_p, shift_p, conv_p, k_s, v_s, va_s, wkv_s, shift_s, conv_s)
```

```python
import functools

import jax
import jax.numpy as jnp
from jax import lax
from jax.experimental import pallas as pl
from jax.experimental.pallas import tpu as pltpu

F32 = jnp.float32
BF = jnp.bfloat16

D_MODEL = 2048
DEPTH = 4
CHUNK = 64
A_WIDTH = D_MODEL // 2
A_HEADS = 8
A_DIM = A_WIDTH // A_HEADS
A_BLOCK = 128
B_HEAD_DIM = 64
B_WIDTH = D_MODEL // 2
B_HEADS = B_WIDTH // B_HEAD_DIM
C_HEAD_DIM = 64
C_HEADS = D_MODEL // C_HEAD_DIM
C_GN_EPS = 64e-5
D_FF = 5632
CONV_W = 3
LN_EPS = 1e-5
DN_ALPHA = (2 * DEPTH) ** 0.25

MXU_WIDTH = 256
LN_ROW_CHUNK = 256
LANES = 128
SUBLANES = 8
HEAD_PAIR = 2 * C_HEAD_DIM
N_PAIRS = D_MODEL // HEAD_PAIR
WKV_CHUNK = 64
WKV_PAIRS_PER_STEP = 16
SB_BLOCK = 256
SB_SUBTILE = 256
SB_CHAIN_LAG = 0
SB_CUMSUM_PASSES = 1
VMEM_CAP = 56 * 1024 * 1024

NT_DIMS = (((1,), (1,)), ((), ()))
TN_DIMS = (((0,), (0,)), ((), ()))


def _tile(m, pref):
    return pref if m % pref == 0 else m


def _params(sem, est_bytes):
    limit = int(min(max(est_bytes * 5 // 4 + (4 << 20), 16 << 20), VMEM_CAP))
    return pltpu.CompilerParams(dimension_semantics=sem, vmem_limit_bytes=limit)


def _nbytes(shape, dtype):
    n = 1
    for s in shape:
        n *= s
    return n * jnp.dtype(dtype).itemsize


def _layernorm(x, g, b, eps=LN_EPS):
    mu = jnp.mean(x, axis=-1, keepdims=True)
    d = x - mu
    var = jnp.mean(d * d, axis=-1, keepdims=True)
    return d * lax.rsqrt(var + eps) * g + b


def _bdot(a, b):
    return jnp.dot(a.astype(BF), b.astype(BF), preferred_element_type=F32)


def _run_in_lockstep(generators, lag=0):
    results = [None] * len(generators)
    started, live, tick = 0, [], 0
    while started < len(generators) or live:
        while started < len(generators) and tick >= started * lag:
            live.append(started)
            started += 1
        for idx in list(live):
            try:
                next(generators[idx])
            except StopIteration as done:
                results[idx] = done.value
                live.remove(idx)
        tick += 1
    return results


def _mm_body(*refs, nprod, ntile, nrow, nout, nk, epilogue, col_chunk, row_chunk, cast_weights):
    xs = refs[:nprod]
    ws = refs[nprod:2 * nprod]
    ts = refs[2 * nprod:2 * nprod + ntile]
    rs = refs[2 * nprod + ntile:2 * nprod + ntile + nrow]
    outs = refs[2 * nprod + ntile + nrow:2 * nprod + ntile + nrow + nout]
    accs = refs[2 * nprod + ntile + nrow + nout:]
    if cast_weights:
        @pl.when(pl.program_id(1) == 0)
        def _():
            for scratch, w in zip(accs, ws):
                scratch[...] = w[...].astype(BF)
        ws = accs

    def finish(vals, cols=slice(None), rws=slice(None)):
        res = epilogue(vals, [t[rws, cols] for t in ts], [r[:, cols] for r in rs])
        for o_ref, o in zip(outs, res):
            o_ref[rws, cols] = o.astype(o_ref.dtype)

    tm, tn = outs[0].shape
    if nk == 1 and row_chunk is not None and tm > row_chunk and tm % row_chunk == 0:
        pending = None
        for ch in range(tm // row_chunk):
            rws = slice(ch * row_chunk, (ch + 1) * row_chunk)
            cur = [jnp.dot(x[rws, :].astype(BF), w[...], preferred_element_type=F32) for x, w in zip(xs, ws)]
            if pending is not None:
                finish(pending[0], rws=pending[1])
            pending = (cur, rws)
        finish(pending[0], rws=pending[1])
        return
    if nk == 1 and col_chunk is not None and tn > col_chunk and tn % col_chunk == 0:
        lhs = [x[...].astype(BF) for x in xs]
        pending = None
        for ch in range(tn // col_chunk):
            cols = slice(ch * col_chunk, (ch + 1) * col_chunk)
            cur = [jnp.dot(a, w[:, cols], preferred_element_type=F32) for a, w in zip(lhs, ws)]
            if pending is not None:
                finish(*pending)
            pending = (cur, cols)
        finish(*pending)
        return

    prods = [jnp.dot(x[...].astype(BF), w[...], preferred_element_type=F32) for x, w in zip(xs, ws)]
    if nk == 1:
        finish(prods)
        return
    k = pl.program_id(2)

    @pl.when(k == 0)
    def _():
        for a, p in zip(accs, prods):
            a[...] = p

    @pl.when(k > 0)
    def _():
        for a, p in zip(accs, prods):
            a[...] += p

    @pl.when(k == nk - 1)
    def _():
        finish([a[...] for a in accs])


def _mm(prods, epilogue, out_dtypes, *, n, tm, tn, tk=None, tiles=(), rows=(), name="mm",
        col_chunk=MXU_WIDTH, row_chunk=None, resident_weights=False):
    prods = [tuple(p) + (0,) * (4 - len(p)) for p in prods]
    m = prods[0][0].shape[0]
    tm = _tile(m, tm)
    ks = [p[0].shape[1] for p in prods]
    nk = 1 if tk is None else ks[0] // tk
    if nk > 1:
        assert all(k == ks[0] for k in ks) and ks[0] % tk == 0
    assert n % tn == 0 and m % tm == 0
    grid = (n // tn, m // tm, nk)
    in_specs, est = [], 0
    for k_dim in ks:
        bk = k_dim if nk == 1 else tk
        in_specs.append(pl.BlockSpec((tm, bk), lambda j, i, k: (i, k)))
    weights = []
    for (x, w, off, row_off), k_dim in zip(prods, ks):
        bk = k_dim if nk == 1 else tk
        assert off % tn == 0 and row_off % bk == 0
        if resident_weights:
            assert nk == 1 and n == tn
        spec, operand = _weight_spec(w, bk, tn, lambda k, rb=row_off // bk: k + rb,
                                     lambda j, ob=off // tn: j + ob, single_buffer=resident_weights)
        in_specs.append(spec)
        weights.append(operand)
        est += 2 * _nbytes((tm, bk), x.dtype) + (1 if resident_weights else 2) * _nbytes((bk, tn), w.dtype)
    for t in tiles:
        in_specs.append(pl.BlockSpec((tm, tn), lambda j, i, k: (i, j)))
        est += 2 * _nbytes((tm, tn), t.dtype)
    for r in rows:
        in_specs.append(pl.BlockSpec((1, tn), lambda j, i, k: (0, j)))
    out_specs = [pl.BlockSpec((tm, tn), lambda j, i, k: (i, j)) for _ in out_dtypes]
    out_shape = [jax.ShapeDtypeStruct((m, n), dt) for dt in out_dtypes]
    est += sum(2 * _nbytes((tm, tn), dt) for dt in out_dtypes)
    est += (len(prods) + 2) * _nbytes((tm, tn), F32)
    scratch = [pltpu.VMEM((tm, tn), F32) for _ in prods] if nk > 1 else []
    cast_weights = any(p[1].dtype != BF for p in prods)
    if cast_weights:
        assert nk == 1 and all(p[1].dtype == F32 for p in prods)
        scratch = [pltpu.VMEM((k_dim, tn), BF) for k_dim in ks]
        est += sum(_nbytes((k_dim, tn), BF) for k_dim in ks)
    body = functools.partial(_mm_body, nprod=len(prods), ntile=len(tiles), nrow=len(rows),
                             nout=len(out_dtypes), nk=nk, epilogue=epilogue, col_chunk=col_chunk,
                             row_chunk=row_chunk, cast_weights=cast_weights)
    return pl.pallas_call(
        body, name=name, grid=grid, in_specs=in_specs, out_specs=out_specs, out_shape=out_shape,
        scratch_shapes=scratch,
        compiler_params=_params(("parallel", "arbitrary", "arbitrary"), est),
    )(*[p[0] for p in prods], *weights, *tiles, *rows)


def _row(v):
    return v.reshape(1, -1).astype(F32)


class _LayerOf:
    def __init__(self, stack, layer):
        self.stack, self.layer = stack, layer
        self.shape, self.dtype = stack.shape[1:], stack.dtype


class _Layers:
    def __init__(self, stack):
        self.stack = stack

    def __getitem__(self, layer):
        return _LayerOf(self.stack, layer)


def _weight_spec(w, rows, cols, row_block, col_block, single_buffer=False):
    mode = dict(pipeline_mode=pl.Buffered(1)) if single_buffer else {}
    if isinstance(w, _LayerOf):
        return (pl.BlockSpec((None, rows, cols), lambda j, i, k=0: (w.layer, row_block(k), col_block(j)),
                             **mode), w.stack)
    return pl.BlockSpec((rows, cols), lambda j, i, k=0: (row_block(k), col_block(j)), **mode), w


def _gate_body(u_ref, va_ref, ws_ref, bs_ref, o_ref, *, blk, nblk):
    ti = lax.broadcasted_iota(jnp.int32, (blk, blk), 0) // CHUNK
    si = lax.broadcasted_iota(jnp.int32, (blk, blk), 1) // CHUNK
    allowed = ti >= si
    for h in range(A_HEADS):
        w = jnp.where(allowed, ws_ref[h], 0.0).astype(BF)
        bias = bs_ref[h]
        for nb in range(nblk):
            rs = slice(nb * blk, (nb + 1) * blk)
            cs = slice(h * A_DIM, (h + 1) * A_DIM)
            mixed = jnp.dot(w, va_ref[rs, cs], preferred_element_type=F32) + bias
            o_ref[rs, cs] = (u_ref[rs, cs].astype(F32) * mixed).astype(o_ref.dtype)


def _spatial_gate(u, va, w_s, b_s, blk):
    m = u.shape[0]
    tm = _tile(m, 512)
    nblk = tm // blk
    est = 2 * 3 * _nbytes((tm, A_WIDTH), BF) + 2 * _nbytes((A_HEADS, blk, blk), F32)
    return pl.pallas_call(
        functools.partial(_gate_body, blk=blk, nblk=nblk),
        name="spatial_gate",
        grid=(m // tm,),
        in_specs=[pl.BlockSpec((tm, A_WIDTH), lambda i: (i, 0)),
                  pl.BlockSpec((tm, A_WIDTH), lambda i: (i, 0)),
                  pl.BlockSpec((A_HEADS, blk, blk), lambda i: (0, 0, 0)),
                  pl.BlockSpec((A_HEADS, blk, 1), lambda i: (0, 0, 0))],
        out_specs=pl.BlockSpec((tm, A_WIDTH), lambda i: (i, 0)),
        out_shape=jax.ShapeDtypeStruct((m, A_WIDTH), BF),
        compiler_params=_params(("parallel",), est),
    )(u, va, w_s.astype(F32), b_s.astype(F32)[:, :, None])


def _later_keys(nkeys):
    jj = lax.broadcasted_iota(jnp.int32, (nkeys, nkeys), 0)
    ss = lax.broadcasted_iota(jnp.int32, (nkeys, nkeys), 1)
    return jnp.where(jj > ss, 1.0, 0.0).astype(BF)


def _sb_chain(qh, kb, valid, later, carry, keys_on_lanes=False):
    if keys_on_lanes:
        z = jnp.dot(qh, kb, preferred_element_type=F32)
    else:
        z = lax.dot_general(qh, kb, NT_DIMS, preferred_element_type=F32)
    yield
    neg_z = -z
    log_fail = jnp.minimum(neg_z, 0.0) - jnp.log(1.0 + jnp.exp(jnp.minimum(z, neg_z)))
    log_hit = z + log_fail
    if valid is not None:
        log_fail = jnp.where(valid, log_fail, 0.0)
    part = log_fail.astype(BF)
    after = jnp.dot(part, later, preferred_element_type=F32)
    rest = log_fail
    for _ in range(SB_CUMSUM_PASSES - 1):
        rest = rest - part.astype(F32)
        part = rest.astype(BF)
        after = after + jnp.dot(part, later, preferred_element_type=F32)
    yield
    total = after[:, 0:1] + log_fail[:, 0:1]
    if carry is None:
        return log_hit + after, total
    w = jnp.exp(log_hit + after + carry)
    if valid is not None:
        w = jnp.where(valid, w, 0.0)
    return w.astype(BF), total


def _sb_cache_body(q_ref, kn_ref, vn_ref, kc_ref, vc_ref, o_ref, c_ref, acc_ref, *, n_steps):
    step = pl.program_id(1)
    t = q_ref.shape[0]
    head_cols = lambda h: slice(h * B_HEAD_DIM, (h + 1) * B_HEAD_DIM)

    def attend(key_of, val_of, nkeys, valid, keys_on_lanes):
        later = _later_keys(nkeys)
        gens = [_sb_chain(q_ref[:, head_cols(h)], key_of(h).astype(BF), valid, later, c_ref[h], keys_on_lanes)
                for h in range(B_HEADS)]
        for h, (w, total) in enumerate(_run_in_lockstep(gens)):
            vals = val_of(h).astype(BF)
            if keys_on_lanes:
                out = lax.dot_general(w, vals, NT_DIMS, preferred_element_type=F32)
            else:
                out = jnp.dot(w, vals, preferred_element_type=F32)
            acc_ref[h] = acc_ref[h] + out
            c_ref[h] = c_ref[h] + total

    @pl.when(step == 0)
    def _():
        c_ref[...] = jnp.zeros_like(c_ref)
        acc_ref[...] = jnp.zeros_like(acc_ref)
        tt = lax.broadcasted_iota(jnp.int32, (t, t), 0)
        sk = lax.broadcasted_iota(jnp.int32, (t, t), 1)
        attend(lambda h: kn_ref[:, head_cols(h)], lambda h: vn_ref[:, head_cols(h)], t, sk < tt, False)

    attend(lambda h: kc_ref[h], lambda h: vc_ref[h], kc_ref.shape[2], None, True)

    @pl.when(step == n_steps - 1)
    def _():
        for h in range(B_HEADS):
            o_ref[:, head_cols(h)] = acc_ref[h].astype(o_ref.dtype)


def _stick_breaking_cache(q, k_new, v_new, k_cache, v_cache, layer, tk):
    bsz, t, _ = q.shape
    p_len = k_cache.shape[4]
    n_steps = p_len // tk
    new_spec = pl.BlockSpec((None, t, B_WIDTH), lambda b, s: (b, 0, 0))
    cache_spec = pl.BlockSpec((None, None, B_HEADS, B_HEAD_DIM, tk),
                              lambda b, s: (layer, b, 0, 0, n_steps - 1 - s))
    est = (2 * 2 * _nbytes((tk, B_HEADS, LANES), F32) + 8 * _nbytes((t, B_WIDTH), F32)
           + 12 * B_HEADS * _nbytes((t, tk), F32))
    return pl.pallas_call(
        functools.partial(_sb_cache_body, n_steps=n_steps),
        name="stick_breaking_cache",
        grid=(bsz, n_steps),
        in_specs=[new_spec, new_spec, new_spec, cache_spec, cache_spec],
        out_specs=new_spec,
        out_shape=jax.ShapeDtypeStruct((bsz, t, B_WIDTH), BF),
        scratch_shapes=[pltpu.VMEM((B_HEADS, t, 1), F32), pltpu.VMEM((B_HEADS, t, B_HEAD_DIM), F32)],
        compiler_params=_params(("parallel", "arbitrary"), est),
    )(q, k_new, v_new, k_cache, v_cache)


def _sb_body(q_ref, kd_ref, vd_ref, kp_ref, vp_ref, o_ref, *, sub, nq, tk, blocks_per_iter, past_iters,
             unroll):
    head_a_q = lax.broadcasted_iota(jnp.int32, (sub, LANES), 1) < B_HEAD_DIM
    q_subs = []
    for i in range(nq):
        q = q_ref[i * sub:(i + 1) * sub, :]
        zero_q = jnp.zeros_like(q)
        q_subs.append((jnp.where(head_a_q, q, zero_q), jnp.where(head_a_q, zero_q, q)))

    def attend(k_blocks, v_blocks, masks, carries, accs):
        kbs = [k.astype(BF) for k in k_blocks]
        laters, v_stacks = {}, []
        for kb, v_blk in zip(kbs, v_blocks):
            nkeys = kb.shape[0]
            if nkeys not in laters:
                laters[nkeys] = _later_keys(nkeys)
            vb = v_blk.astype(BF)
            head_a_k = lax.broadcasted_iota(jnp.int32, (nkeys, LANES), 1) < B_HEAD_DIM
            zero_v = jnp.zeros_like(vb)
            v_stacks.append(jnp.concatenate([jnp.where(head_a_k, vb, zero_v), jnp.where(head_a_k, zero_v, vb)],
                                            axis=0))
        early = len(kbs) == 1
        keys, gens = [], []
        for i in range(nq):
            for j, kb in enumerate(kbs):
                if isinstance(masks[i][j], str):
                    continue
                for h in range(2):
                    keys.append((i, j, h))
                    gens.append(_sb_chain(q_subs[i][h], kb, masks[i][j], laters[kb.shape[0]],
                                          carries[i][h] if early else None))
        scores = dict(zip(keys, _run_in_lockstep(gens, lag=SB_CHAIN_LAG)))
        new_carries, new_accs = [], []
        for i in range(nq):
            cs = list(carries[i])
            wts, vals = [], []
            for j in range(len(kbs)):
                if isinstance(masks[i][j], str):
                    continue
                for h in range(2):
                    w, total = scores[(i, j, h)]
                    if not early:
                        w = jnp.exp(w + cs[h])
                        if masks[i][j] is not None:
                            w = jnp.where(masks[i][j], w, 0.0)
                        w = w.astype(BF)
                    cs[h] = cs[h] + total
                    wts.append(w)
                vals.append(v_stacks[j])
            new_carries.append(cs)
            new_accs.append(accs[i] + jnp.dot(jnp.concatenate(wts, axis=1), jnp.concatenate(vals, axis=0),
                                              preferred_element_type=F32))
        return new_carries, new_accs

    k_diag = [kd_ref[j * sub:(j + 1) * sub, :] for j in reversed(range(nq))]
    v_diag = [vd_ref[j * sub:(j + 1) * sub, :] for j in reversed(range(nq))]
    tt = lax.broadcasted_iota(jnp.int32, (sub, sub), 0)
    sk = lax.broadcasted_iota(jnp.int32, (sub, sub), 1)
    causal = sk < tt
    masks = [[causal if j == i else (None if j < i else "skip") for j in reversed(range(nq))]
             for i in range(nq)]
    zeros_c = jnp.zeros((sub, 1), F32)
    carries, accs = attend(k_diag, v_diag, masks, [[zeros_c, zeros_c] for _ in range(nq)],
                           [jnp.zeros((sub, LANES), F32) for _ in range(nq)])
    n_iters = past_iters(pl.program_id(2))
    open_masks = [[None] * blocks_per_iter for _ in range(nq)]

    def step(it, state):
        carries = [list(state[2 * i:2 * i + 2]) for i in range(nq)]
        accs = list(state[2 * nq:])
        newest = n_iters - 1 - it
        offs = [pl.multiple_of((newest * blocks_per_iter + j) * tk, tk)
                for j in reversed(range(blocks_per_iter))]
        carries, accs = attend([kp_ref[pl.ds(o, tk), :] for o in offs], [vp_ref[pl.ds(o, tk), :] for o in offs],
                               open_masks, carries, accs)
        return tuple(c for pair in carries for c in pair) + tuple(accs)

    state = lax.fori_loop(0, n_iters, step, tuple(c for pair in carries for c in pair) + tuple(accs),
                          unroll=unroll)
    for i in range(nq):
        o_ref[i * sub:(i + 1) * sub, :] = state[2 * nq + i].astype(o_ref.dtype)


def _stick_breaking(q, k, v, *, tq, tk):
    bsz, t, _ = q.shape
    k_new, v_new, k_past, v_past = k, v, k, v
    past_spec = pl.BlockSpec((None, t, LANES), lambda b, p, i: (b, 0, p))
    sub = min(tq, SB_SUBTILE)
    nq = tq // sub
    blocks_per_iter = 1
    past_iters = lambda qi: qi * (tq // tk)
    est = (2 * 2 * _nbytes((t, LANES), k.dtype) + 2 * 4 * _nbytes((tq, LANES), F32)
           + 12 * 2 * nq * blocks_per_iter * _nbytes((sub, max(sub, tk)), F32))
    return pl.pallas_call(
        functools.partial(_sb_body, sub=sub, nq=nq, tk=tk, blocks_per_iter=blocks_per_iter,
                          past_iters=past_iters, unroll=None),
        name="stick_breaking_past",
        grid=(bsz, B_WIDTH // LANES, t // tq),
        in_specs=[pl.BlockSpec((None, tq, LANES), lambda b, p, i: (b, i, p)),
                  pl.BlockSpec((None, tq, LANES), lambda b, p, i: (b, i, p)),
                  pl.BlockSpec((None, tq, LANES), lambda b, p, i: (b, i, p)),
                  past_spec, past_spec],
        out_specs=pl.BlockSpec((None, tq, LANES), lambda b, p, i: (b, i, p)),
        out_shape=jax.ShapeDtypeStruct((bsz, t, B_WIDTH), BF),
        compiler_params=_params(("parallel", "parallel", "arbitrary"), est),
    )(q, k_new, v_new, k_past, v_past)


def _shift_body(x_ref, last_ref, mu_ref, *rest, tiles_per_seq):
    outs, carry = rest[:6], rest[6]
    i = pl.program_id(1)

    @pl.when(i % tiles_per_seq == 0)
    def _():
        carry[...] = jnp.broadcast_to(last_ref[...], carry.shape)

    x = x_ref[...]
    rows = lax.broadcasted_iota(jnp.int32, x.shape, 0)
    x_prev = jnp.where(rows == 0, carry[SUBLANES - 1:SUBLANES, :], pltpu.roll(x, 1, 0))
    carry[...] = x[x.shape[0] - SUBLANES:, :]
    xx = x_prev - x
    for j, o_ref in enumerate(outs):
        o_ref[...] = (x + xx * mu_ref[j:j + 1, :]).astype(o_ref.dtype)


def _token_shift(x, x_last, mu):
    bsz, t, d = x.shape
    tm = _tile(t, 256)
    est = 2 * _nbytes((tm, d), F32) + 12 * _nbytes((tm, d), BF) + 4 * _nbytes((tm, d), F32)
    return pl.pallas_call(
        functools.partial(_shift_body, tiles_per_seq=t // tm),
        name="token_shift",
        grid=(bsz, t // tm),
        in_specs=[pl.BlockSpec((None, tm, d), lambda b, i: (b, i, 0)),
                  pl.BlockSpec((None, 1, d), lambda b, i: (b, 0, 0)),
                  pl.BlockSpec((6, d), lambda b, i: (0, 0))],
        out_specs=[pl.BlockSpec((None, tm, d), lambda b, i: (b, i, 0)) for _ in range(6)],
        out_shape=[jax.ShapeDtypeStruct((bsz, t, d), BF) for _ in range(6)],
        scratch_shapes=[pltpu.VMEM((SUBLANES, d), F32)],
        compiler_params=_params(("parallel", "arbitrary"), est),
    )(x, x_last[:, None, :].astype(F32), mu.astype(F32))


def _wkv_body(r_ref, k_ref, v_ref, lw_ref, a_ref, g_ref, kk_ref, ka_ref, rk_ref, gg_ref, gb_ref,
              s0_ref, y_ref, s_ref):
    c = WKV_CHUNK
    n = 2 * c

    @pl.when(pl.program_id(1) == 0)
    def _():
        s_ref[...] = s0_ref[...]

    head_a = lax.broadcasted_iota(jnp.int32, (c, LANES), 1) < C_HEAD_DIM
    ri = lax.broadcasted_iota(jnp.int32, (n, n), 0)
    ci = lax.broadcasted_iota(jnp.int32, (n, n), 1)
    same_head = (ri // c) == (ci // c)
    strict = same_head & (ci < ri)
    inclusive = same_head & (ci <= ri)
    diag16 = (ri // 16) == (ci // 16)
    diag32 = (ri // 32) == (ci // 32)
    eye = jnp.where(ri == ci, 1.0, 0.0).astype(F32)
    seg_ones = jnp.where(same_head, 1.0, 0.0).astype(BF)
    ti = lax.broadcasted_iota(jnp.int32, (c, c), 0)
    si = lax.broadcasted_iota(jnp.int32, (c, c), 1)
    prefix = jnp.where(si <= ti, 1.0, 0.0).astype(BF)

    def split2(x):
        hi = x.astype(BF)
        return hi, (x - hi.astype(F32)).astype(BF)

    def seg_sum(x):
        hi, lo = split2(x)
        return (jnp.dot(hi, seg_ones, preferred_element_type=F32)
                + jnp.dot(lo, seg_ones, preferred_element_type=F32))

    def cumsum_time(x):
        hi, rest = split2(x)
        rest_f = x - hi.astype(F32)
        mid = rest_f.astype(BF)
        lo = (rest_f - mid.astype(F32)).astype(BF)
        return (jnp.dot(prefix, hi, preferred_element_type=F32)
                + jnp.dot(prefix, mid, preferred_element_type=F32)
                + jnp.dot(prefix, lo, preferred_element_type=F32))

    def stack(x):
        zero = jnp.zeros_like(x)
        return jnp.concatenate([jnp.where(head_a, x, zero), jnp.where(head_a, zero, x)], axis=0)

    def pair(p, s):
        sl = pl.ds(pl.multiple_of(p * LANES, LANES), LANES)
        r = r_ref[:, sl]
        k = k_ref[:, sl]
        v = v_ref[:, sl]
        lw = lw_ref[:, sl]
        a = a_ref[:, sl]
        kkp = k * kk_ref[:, sl]
        norm = jnp.sqrt(seg_sum(kkp * kkp))
        yield
        kk = kkp / jnp.maximum(norm, 1e-12)
        k2 = k * (1.0 + (a - 1.0) * ka_ref[:, sl])
        bonus = seg_sum(r * k2 * rk_ref[:, sl]) * v
        yield
        cs = cumsum_time(lw)
        yield
        e_pos = jnp.exp(cs)
        e_neg = jnp.exp(-cs)
        a_t = -kk * e_pos * jnp.exp(-lw)
        b_t = kk * a * e_neg
        k_t = k2 * e_neg
        r_t = r * e_pos
        la, lr = stack(a_t), stack(r_t)
        rb, rk = stack(b_t), stack(k_t)
        vs = stack(v)
        q = lax.dot_general(jnp.concatenate([la, lr], axis=0).astype(BF),
                            jnp.concatenate([rb, rk], axis=0).astype(BF),
                            NT_DIMS, preferred_element_type=F32)
        yield
        p_ab = jnp.where(strict, q[:n, :n], 0.0)
        a_ak = jnp.where(strict, q[:n, n:], 0.0)
        a_rb = jnp.where(inclusive, q[n:, :n], 0.0)
        a_rk = jnp.where(inclusive, q[n:, n:], 0.0)
        akv = _bdot(a_ak, vs)
        yield
        pd = jnp.where(diag16, p_ab, 0.0)
        inv = eye + pd
        pw = pd
        for _ in range(3):
            pw = _bdot(pw, pw)
            yield
            inv = inv + _bdot(pw, inv)
            yield
        for off_diag in (jnp.where(diag32 & (~diag16), p_ab, 0.0), jnp.where(diag32, 0.0, p_ab)):
            right = _bdot(off_diag, inv)
            yield
            inv = inv + _bdot(inv, right)
            yield
        wu = _bdot(inv, jnp.concatenate([la, akv], axis=1))
        yield
        w_m, u_m = wu[:, :LANES], wu[:, LANES:]
        wr = lax.dot_general(jnp.concatenate([w_m, lr], axis=0).astype(BF), s.astype(BF),
                             NT_DIMS, preferred_element_type=F32)
        yield
        sa = wr[:n] + u_m
        sv = jnp.concatenate([sa, vs], axis=0)
        y_st = wr[n:] + _bdot(jnp.concatenate([a_rb, a_rk], axis=1), sv)
        g_end = e_pos[c - 1:c, :]
        rg = jnp.concatenate([rb, rk], axis=0) * g_end
        s_new = s * g_end + lax.dot_general(sv.astype(BF), rg.astype(BF), TN_DIMS,
                                            preferred_element_type=F32)
        yield
        y = y_st[:c] + y_st[c:]
        mean = seg_sum(y) * (1.0 / C_HEAD_DIM)
        yield
        d = y - mean
        var = seg_sum(d * d) * (1.0 / C_HEAD_DIM)
        yield
        yn = d * lax.rsqrt(var + C_GN_EPS) * gg_ref[:, sl] + gb_ref[:, sl]
        return s_new, ((yn + bonus) * g_ref[:, sl]).astype(y_ref.dtype)

    def group(it, carry):
        pairs = [it * WKV_PAIRS_PER_STEP + u for u in range(WKV_PAIRS_PER_STEP)]
        states = [s_ref[p] for p in pairs]
        results = _run_in_lockstep([pair(p, s) for p, s in zip(pairs, states)])
        for p, (s_new, y_out) in zip(pairs, results):
            s_ref[p] = s_new
            y_ref[:, pl.ds(pl.multiple_of(p * LANES, LANES), LANES)] = y_out
        return carry

    lax.fori_loop(0, N_PAIRS // WKV_PAIRS_PER_STEP, group, 0)


def _pair_state(s):
    bsz = s.shape[0]
    s = s.reshape(bsz, N_PAIRS, 2, C_HEAD_DIM, C_HEAD_DIM).astype(F32)
    z = jnp.zeros_like(s[:, :, 0])
    top = jnp.concatenate([s[:, :, 0], z], axis=-1)
    bot = jnp.concatenate([z, s[:, :, 1]], axis=-1)
    return jnp.concatenate([top, bot], axis=-2)


def _unpair_state(sp):
    bsz = sp.shape[0]
    a = sp[:, :, :C_HEAD_DIM, :C_HEAD_DIM]
    b = sp[:, :, C_HEAD_DIM:, C_HEAD_DIM:]
    return jnp.stack([a, b], axis=2).reshape(bsz, C_HEADS, C_HEAD_DIM, C_HEAD_DIM)


def _wkv(r, k, v, lw, a, g, s0, k_k, k_a, r_k, gn_g, gn_b):
    bsz, t, d = r.shape
    c = WKV_CHUNK
    seq = pl.BlockSpec((None, c, d), lambda b, i: (b, i, 0))
    vec = pl.BlockSpec((1, d), lambda b, i: (0, 0))
    st = pl.BlockSpec((None, N_PAIRS, HEAD_PAIR, HEAD_PAIR), lambda b, i: (b, 0, 0, 0))
    est = 2 * 6 * _nbytes((c, d), F32) + 4 * _nbytes((N_PAIRS, HEAD_PAIR, HEAD_PAIR), F32) + (8 << 20)
    y, s_new = pl.pallas_call(
        _wkv_body,
        name="wkv7_chunked",
        grid=(bsz, t // c),
        in_specs=[seq] * 6 + [vec] * 5 + [st],
        out_specs=[seq, st],
        out_shape=[jax.ShapeDtypeStruct((bsz, t, d), BF),
                   jax.ShapeDtypeStruct((bsz, N_PAIRS, HEAD_PAIR, HEAD_PAIR), F32)],
        compiler_params=_params(("parallel", "arbitrary"), est),
    )(r, k, v, lw, a, g, _row(k_k), _row(k_a), _row(r_k), _row(gn_g), _row(gn_b), _pair_state(s0))
    return y, _unpair_state(s_new)


def _ffn_in_body(x_ref, wg_f32_ref, wu_f32_ref, cw_ref, cb_ref, prev_ref, act_ref, st_ref, tail_ref,
                 wg_ref, wu_ref, *, seq_len, tm):
    @pl.when(pl.program_id(1) == 0)
    def _():
        wg_ref[...] = wg_f32_ref[...].astype(BF)
        wu_ref[...] = wu_f32_ref[...].astype(BF)

    x = x_ref[...]
    tn = act_ref.shape[1]
    cn = min(tn, MXU_WIDTH)
    last2, last1 = SUBLANES - 2, SUBLANES - 1
    carried = seq_len >= tm
    if carried:
        @pl.when(pl.program_id(1) % (seq_len // tm) == 0)
        def _():
            tail_ref[...] = prev_ref[0]

    def conv_gate(hg, hu, cols):
        rows = lax.broadcasted_iota(jnp.int32, hg.shape, 0)
        if carried:
            tail = tail_ref[:, cols]
            p0, p1 = tail[last2:last2 + 1, :], tail[last1:last1 + 1, :]
            pos = rows
            new_tail = hg[tm - SUBLANES:, :]
            tail_ref[:, cols] = new_tail
            st_ref[0, :, cols] = new_tail
        else:
            nseq = tm // seq_len
            prev = prev_ref[:, :, cols]
            spread = lambda rw: jnp.broadcast_to(rw, (nseq, seq_len, cn)).reshape(tm, cn)
            p0, p1 = spread(prev[:, last2:last2 + 1, :]), spread(prev[:, last1:last1 + 1, :])
            pos = rows % seq_len
            st_ref[:, :, cols] = hg.reshape(nseq, seq_len, cn)[:, seq_len - SUBLANES:, :]
        s1 = jnp.where(pos == 0, p1, pltpu.roll(hg, 1, 0))
        s2 = jnp.where(pos == 0, p0, jnp.where(pos == 1, p1, pltpu.roll(hg, 2, 0)))
        cw = cw_ref[:, cols]
        hc = cb_ref[:, cols] + s2 * cw[0:1, :] + s1 * cw[1:2, :] + hg * cw[2:3, :]
        act_ref[:, cols] = (jax.nn.gelu(hc) * hu).astype(act_ref.dtype)

    pending = None
    for ch in range(tn // cn):
        cols = slice(ch * cn, (ch + 1) * cn)
        hg = jnp.dot(x, wg_ref[:, cols], preferred_element_type=F32)
        hu = jnp.dot(x, wu_ref[:, cols], preferred_element_type=F32)
        if pending is not None:
            conv_gate(*pending)
        pending = (hg, hu, cols)
    conv_gate(*pending)


def _ffn_in(xb, w_in, conv_w, conv_b, conv_prev, seq_len):
    m, d = xb.shape
    nseq_total = m // seq_len
    tn = 512
    tm = _tile(m, 1024)
    nj = D_FF // tn
    prev8 = jnp.pad(conv_prev.astype(F32), ((0, 0), (SUBLANES - (CONV_W - 1), 0), (0, 0)))
    if seq_len >= tm:
        per = seq_len // tm
        nb = 1
        st_map = lambda j, i: (i // per, 0, j)
    else:
        nb = tm // seq_len
        st_map = lambda j, i: (i, 0, j)
    st_spec = pl.BlockSpec((nb, SUBLANES, tn), st_map)
    assert w_in.dtype == F32
    est = (2 * _nbytes((tm, d), BF) + 4 * _nbytes((d, tn), F32) + 2 * _nbytes((d, tn), BF)
           + 2 * _nbytes((tm, tn), BF) + 8 * _nbytes((tm, MXU_WIDTH), F32))
    gate_spec, w_operand = _weight_spec(w_in, d, tn, lambda k: 0, lambda j: j)
    up_spec, _ = _weight_spec(w_in, d, tn, lambda k: 0, lambda j: j + nj)
    act, tails = pl.pallas_call(
        functools.partial(_ffn_in_body, seq_len=seq_len, tm=tm),
        name="ffn_in_conv",
        grid=(nj, m // tm),
        in_specs=[pl.BlockSpec((tm, d), lambda j, i: (i, 0)),
                  gate_spec,
                  up_spec,
                  pl.BlockSpec((CONV_W, tn), lambda j, i: (0, j)),
                  pl.BlockSpec((1, tn), lambda j, i: (0, j)),
                  st_spec],
        out_specs=[pl.BlockSpec((tm, tn), lambda j, i: (i, j)), st_spec],
        out_shape=[jax.ShapeDtypeStruct((m, D_FF), BF),
                   jax.ShapeDtypeStruct((nseq_total, SUBLANES, D_FF), F32)],
        scratch_shapes=[pltpu.VMEM((SUBLANES, tn), F32), pltpu.VMEM((d, tn), BF), pltpu.VMEM((d, tn), BF)],
        compiler_params=_params(("parallel", "arbitrary"), est),
    )(xb, w_operand, w_operand, conv_w.astype(F32), _row(conv_b), prev8)
    return act, tails[:, SUBLANES - (CONV_W - 1):, :]


def _ln_residual_epilogue(accs, tiles, rows):
    mix = accs[0]
    for extra in accs[1:]:
        mix = mix + extra
    y = _layernorm(DN_ALPHA * tiles[0] + mix, rows[0], rows[1])
    return y, y


def _identity_epilogue(accs, tiles, rows):
    return (accs[0],)


def _even_mixer(x, xb, seq_len, w_in, lnv_g, lnv_b, w_s, b_s, w_o, ln_g, ln_b, cache):
    m = x.shape[0]
    bsz = m // seq_len
    proj = lambda name, off, ep, dts, rows=(), col_chunk=MXU_WIDTH: _mm(
        [(xb, w_in, off)], ep, dts, n=1024, tm=512, tn=1024, rows=rows, name=name, col_chunk=col_chunk)
    (u,) = proj("even_u", 0, lambda a, t, r: (jax.nn.gelu(a[0]),), [BF])

    def va_epilogue(a, t, r):
        va = _layernorm(jax.nn.gelu(a[0]), r[0], r[1])
        return va, va
    va, vab = proj("even_va", A_WIDTH, va_epilogue, [F32, BF], rows=(_row(lnv_g), _row(lnv_b)),
                   col_chunk=None)
    o = 2 * A_WIDTH
    (q,) = proj("even_q", o, lambda a, t, r: (a[0] * (B_HEAD_DIM ** -0.5),), [BF])
    (k,) = proj("even_k", o + B_WIDTH, _identity_epilogue, [F32])
    (v,) = proj("even_v", o + 2 * B_WIDTH, _identity_epilogue, [F32])

    blk = min(seq_len, A_BLOCK)
    a_out = _spatial_gate(u, vab, w_s[:, :blk, :blk], b_s[:, :blk], blk)

    q3 = q.reshape(bsz, seq_len, B_WIDTH)
    k3 = k.reshape(bsz, seq_len, B_WIDTH)
    v3 = v.reshape(bsz, seq_len, B_WIDTH)
    if cache is None:
        tq = _tile(seq_len, 4 * SB_BLOCK)
        b_out = _stick_breaking(q3, k3, v3, tq=tq, tk=min(tq, SB_BLOCK))
    else:
        k_cache, v_cache, layer = cache
        b_out = _stick_breaking_cache(q3, k3, v3, k_cache, v_cache, layer, _tile(k_cache.shape[4], SB_BLOCK))
    b_out = b_out.reshape(m, B_WIDTH)

    x1, x1b = _mm([(a_out, w_o, 0), (b_out, w_o, 0, A_WIDTH)], _ln_residual_epilogue, [F32, BF],
                  n=D_MODEL, tm=2 * LN_ROW_CHUNK, tn=D_MODEL, tiles=(x,), rows=(_row(ln_g), _row(ln_b)),
                  name="even_out_ln", col_chunk=None, row_chunk=LN_ROW_CHUNK, resident_weights=True)
    return x1, x1b, k, v, va


def _rwkv_mixer(x, seq_len, x_last, s0, v_first, P, ln_g, ln_b):
    m = x.shape[0]
    bsz = m // seq_len
    mixes = _token_shift(x.reshape(bsz, seq_len, D_MODEL), x_last, P['mu'])
    xr, xw, xk, xv, xa, xg = [a.reshape(m, D_MODEL) for a in mixes]
    big = lambda xin, w: _mm([(xin, w, 0)], _identity_epilogue, [F32], n=D_MODEL, tm=512, tn=1024,
                             name="rwkv_proj")[0]
    r = big(xr, P['w_r'])
    k = big(xk, P['w_k'])

    def lora_in(xin, w, act):
        n = w.shape[1]
        return _mm([(xin, w, 0)], lambda a, t, r_: (act(a[0]),), [BF], n=n, tm=512, tn=n,
                   name="rwkv_lora_in")[0]

    def lora_out(h, w, bias, act):
        return _mm([(h, w, 0)], lambda a, t, r_: (act(r_[0] + a[0]),), [F32], n=D_MODEL, tm=512,
                   tn=1024, rows=(_row(bias),), name="rwkv_lora_out")[0]

    hw = lora_in(xw, P['w1'], jnp.tanh)
    lw = lora_out(hw, P['w2'], P['w0'],
                  lambda pre: -jnp.exp(-(jnp.maximum(-pre, 0.0) + jnp.log(1.0 + jnp.exp(-jnp.abs(pre)))) - 0.5))
    ha = lora_in(xa, P['a1'], lambda z: z)
    a = lora_out(ha, P['a2'], P['a0'], jax.nn.sigmoid)
    hg = lora_in(xg, P['g1'], jax.nn.sigmoid)
    g = _mm([(hg, P['g2'], 0)], _identity_epilogue, [F32], n=D_MODEL, tm=512, tn=1024, name="rwkv_gate")[0]
    if P['v1'] is None:
        v = big(xv, P['w_v'])
        v_first = v
    else:
        hv = lora_in(xv, P['v1'], lambda z: z)

        def v_epilogue(accs, tiles, rows):
            v_new = accs[0]
            return (v_new + (tiles[0] - v_new) * jax.nn.sigmoid(rows[0] + accs[1]),)
        v = _mm([(xv, P['w_v'], 0), (hv, P['v2'], 0)], v_epilogue, [F32], n=D_MODEL, tm=512, tn=1024,
                tiles=(v_first,), rows=(_row(P['v0']),), name="rwkv_v_residual")[0]
    to3 = lambda z: z.reshape(bsz, seq_len, D_MODEL)
    yg, s_new = _wkv(to3(r), to3(k), to3(v), to3(lw), to3(a), to3(g), s0,
                     P['k_k'], P['k_a'], P['r_k'], P['gn_g'], P['gn_b'])
    x1, x1b = _mm([(yg.reshape(m, D_MODEL), P['w_o'], 0)], _ln_residual_epilogue, [F32, BF],
                  n=D_MODEL, tm=2 * LN_ROW_CHUNK, tn=D_MODEL, tiles=(x,), rows=(_row(ln_g), _row(ln_b)),
                  name="rwkv_out_ln", col_chunk=None, row_chunk=LN_ROW_CHUNK, resident_weights=True)
    return x1, x1b, s_new, v_first


def _ffn_ple(x1, x1b, seq_len, p, conv_prev, w_in, conv_w, conv_b, w_out, ln_g, ln_b, ple_gate, ple_proj):
    act, conv_rows = _ffn_in(x1b, w_in, conv_w, conv_b, conv_prev, seq_len)
    x2, x2b = _mm([(act, w_out, 0)], _ln_residual_epilogue, [F32, BF], n=D_MODEL, tm=512, tn=D_MODEL,
                  tk=D_FF // 4, tiles=(x1,), rows=(_row(ln_g), _row(ln_b)), name="ffn_out_ln", col_chunk=None)

    def ple_epilogue(accs, tiles, rows):
        y = tiles[0] + jax.nn.sigmoid(accs[0]) * accs[1]
        return y, y
    x3, x3b = _mm([(x2b, ple_gate, 0), (p, ple_proj, 0)], ple_epilogue, [F32, BF], n=D_MODEL, tm=512,
                  tn=1024, tiles=(x2,), name="ple")
    return x3, x3b, conv_rows


def _trunk(x, p, W, caches, wkv0, shift0, conv0):
    bsz, seq_len, _ = x.shape
    m = bsz * seq_len
    x = x.reshape(m, D_MODEL).astype(F32)
    xb = x.astype(BF)
    new_k, new_v, new_va, new_wkv, new_shift, new_conv = [], [], [], [], [], []
    v_first = None
    for i in range(DEPTH):
        j = i // 2
        if i % 2 == 0:
            cache = None if caches is None else (caches[0], caches[1], j)
            x1, x1b, k, v, va = _even_mixer(
                x, xb, seq_len, W['even_w_in'][j], W['even_lnv_g'][j], W['even_lnv_b'][j],
                W['even_w_s'][j], W['even_b_s'][j], W['even_w_o'][j], W['ln1_g'][i], W['ln1_b'][i], cache)
            new_k.append(k.reshape(bsz, seq_len, B_HEADS, B_HEAD_DIM))
            new_v.append(v.reshape(bsz, seq_len, B_HEADS, B_HEAD_DIM))
            new_va.append(va.reshape(bsz, seq_len, A_HEADS, A_DIM))
        else:
            P = dict(mu=W['c_mu'][j], w_r=W['c_w_r'][j], w_k=W['c_w_k'][j], w_v=W['c_w_v'][j],
                     w_o=W['c_w_o'][j], w0=W['c_w0'][j], w1=W['c_w1'][j], w2=W['c_w2'][j],
                     a0=W['c_a0'][j], a1=W['c_a1'][j], a2=W['c_a2'][j], g1=W['c_g1'][j], g2=W['c_g2'][j],
                     k_k=W['c_k_k'][j], k_a=W['c_k_a'][j], r_k=W['c_r_k'][j], gn_g=W['c_gn_g'][j],
                     gn_b=W['c_gn_b'][j],
                     v0=None if j == 0 else W['c_v0'][j - 1], v1=None if j == 0 else W['c_v1'][j - 1],
                     v2=None if j == 0 else W['c_v2'][j - 1])
            x1, x1b, s_new, v_first = _rwkv_mixer(x, seq_len, shift0[j], wkv0[j], v_first, P,
                                                  W['ln1_g'][i], W['ln1_b'][i])
            new_wkv.append(s_new)
            new_shift.append(x.reshape(bsz, seq_len, D_MODEL)[:, -1])
        x, xb, conv_rows = _ffn_ple(
            x1, x1b, seq_len, p[i].reshape(m, -1), conv0[i], W['ffn_w_in'][i], W['ffn_conv_w'][i],
            W['ffn_conv_b'][i], W['ffn_w_out'][i], W['ln2_g'][i], W['ln2_b'][i], W['ple_gate'][i],
            W['ple_proj'][i])
        new_conv.append(conv_rows)
    st = lambda l: jnp.stack(l) if l else None
    return (x.reshape(bsz, seq_len, D_MODEL), st(new_k), st(new_v), st(new_va), st(new_wkv),
            st(new_shift), st(new_conv))


_MATRICES = ('even_w_in', 'even_w_o', 'c_w_r', 'c_w_k', 'c_w_v', 'c_w_o', 'c_w1', 'c_w2', 'c_a1', 'c_a2',
             'c_v1', 'c_v2', 'c_g1', 'c_g2', 'ffn_w_in', 'ffn_w_out', 'ple_proj', 'ple_gate')
_PRECAST = ('even_w_o', 'c_w_o', 'ffn_w_out')


def kernel(x_prompt, x_sample, cache_sb_k, cache_sb_v, state_rwkv_wkv, state_rwkv_shift, state_ffn_conv, p_prompt, p_sample, even_w_in, even_lnv_g, even_lnv_b, even_w_s, even_b_s, even_w_o, c_mu, c_w_r, c_w_k, c_w_v, c_w_o, c_w0, c_w1, c_w2, c_a0, c_a1, c_a2, c_v0, c_v1, c_v2, c_g1, c_g2, c_k_k, c_k_a, c_r_k, c_gn_g, c_gn_b, ffn_w_in, ffn_conv_w, ffn_conv_b, ffn_w_out, ln1_g, ln1_b, ln2_g, ln2_b, ple_proj, ple_gate):
    W = dict(even_w_in=even_w_in, even_lnv_g=even_lnv_g, even_lnv_b=even_lnv_b, even_w_s=even_w_s,
             even_b_s=even_b_s, even_w_o=even_w_o, c_mu=c_mu, c_w_r=c_w_r, c_w_k=c_w_k,
             c_w_v=c_w_v, c_w_o=c_w_o, c_w0=c_w0, c_w1=c_w1, c_w2=c_w2, c_a0=c_a0, c_a1=c_a1,
             c_a2=c_a2, c_v0=c_v0, c_v1=c_v1, c_v2=c_v2, c_g1=c_g1, c_g2=c_g2, c_k_k=c_k_k,
             c_k_a=c_k_a, c_r_k=c_r_k, c_gn_g=c_gn_g, c_gn_b=c_gn_b, ffn_w_in=ffn_w_in,
             ffn_conv_w=ffn_conv_w, ffn_conv_b=ffn_conv_b, ffn_w_out=ffn_w_out, ln1_g=ln1_g,
             ln1_b=ln1_b, ln2_g=ln2_g, ln2_b=ln2_b, ple_proj=ple_proj, ple_gate=ple_gate)
    for name in _MATRICES:
        W[name] = _Layers(W[name].astype(BF) if name in _PRECAST else W[name])
    bp = x_prompt.shape[0]
    n_odd, n_even = state_rwkv_wkv.shape[0], cache_sb_k.shape[0]
    cache_sb_k = jnp.transpose(cache_sb_k, (0, 1, 3, 4, 2))
    cache_sb_v = jnp.transpose(cache_sb_v, (0, 1, 3, 4, 2))
    wkv_zero = jnp.zeros((n_odd, bp, C_HEADS, C_HEAD_DIM, C_HEAD_DIM), F32)
    shift_zero = jnp.zeros((n_odd, bp, D_MODEL), F32)
    conv_zero = jnp.zeros((DEPTH, bp, CONV_W - 1, D_FF), F32)
    y_p, k_p, v_p, _, wkv_p, shift_p, conv_p = _trunk(
        x_prompt, p_prompt, W, None, wkv_zero, shift_zero, conv_zero)
    y_s, k_s, v_s, va_s, wkv_s, shift_s, conv_s = _trunk(
        x_sample, p_sample, W, (cache_sb_k, cache_sb_v), state_rwkv_wkv, state_rwkv_shift, state_ffn_conv)
    return (y_p, y_s, k_p, v_p, wkv_p, shift_p, conv_p, k_s, v_s, va_s, wkv_s, shift_s, conv_s)
```

```python
import functools

import jax
import jax.numpy as jnp
from jax import lax
from jax.experimental import pallas as pl
from jax.experimental.pallas import tpu as pltpu

F32 = jnp.float32
BF = jnp.bfloat16

D_MODEL = 2048
DEPTH = 4
CHUNK = 64
A_WIDTH = D_MODEL // 2
A_HEADS = 8
A_DIM = A_WIDTH // A_HEADS
A_BLOCK = 128
B_HEAD_DIM = 64
B_WIDTH = D_MODEL // 2
B_HEADS = B_WIDTH // B_HEAD_DIM
C_HEAD_DIM = 64
C_HEADS = D_MODEL // C_HEAD_DIM
C_GN_EPS = 64e-5
D_FF = 5632
CONV_W = 3
LN_EPS = 1e-5
DN_ALPHA = (2 * DEPTH) ** 0.25

MXU_WIDTH = 256
LN_ROW_CHUNK = 256
LANES = 128
SUBLANES = 8
HEAD_PAIR = 2 * C_HEAD_DIM
N_PAIRS = D_MODEL // HEAD_PAIR
WKV_CHUNK = 64
WKV_PAIRS_PER_STEP = 16
SB_BLOCK = 256
SB_SUBTILE = 256
SB_CHAIN_LAG = 0
SB_CUMSUM_PASSES = 1
VMEM_CAP = 56 * 1024 * 1024

NT_DIMS = (((1,), (1,)), ((), ()))
TN_DIMS = (((0,), (0,)), ((), ()))


def _tile(m, pref):
    return pref if m % pref == 0 else m


def _params(sem, est_bytes):
    limit = int(min(max(est_bytes * 5 // 4 + (4 << 20), 16 << 20), VMEM_CAP))
    return pltpu.CompilerParams(dimension_semantics=sem, vmem_limit_bytes=limit)


def _nbytes(shape, dtype):
    n = 1
    for s in shape:
        n *= s
    return n * jnp.dtype(dtype).itemsize


def _layernorm(x, g, b, eps=LN_EPS):
    mu = jnp.mean(x, axis=-1, keepdims=True)
    d = x - mu
    var = jnp.mean(d * d, axis=-1, keepdims=True)
    return d * lax.rsqrt(var + eps) * g + b


def _bdot(a, b):
    return jnp.dot(a.astype(BF), b.astype(BF), preferred_element_type=F32)


def _run_in_lockstep(generators, lag=0):
    results = [None] * len(generators)
    started, live, tick = 0, [], 0
    while started < len(generators) or live:
        while started < len(generators) and tick >= started * lag:
            live.append(started)
            started += 1
        for idx in list(live):
            try:
                next(generators[idx])
            except StopIteration as done:
                results[idx] = done.value
                live.remove(idx)
        tick += 1
    return results


def _mm_body(*refs, nprod, ntile, nrow, nout, nk, epilogue, col_chunk, row_chunk, cast_weights):
    xs = refs[:nprod]
    ws = refs[nprod:2 * nprod]
    ts = refs[2 * nprod:2 * nprod + ntile]
    rs = refs[2 * nprod + ntile:2 * nprod + ntile + nrow]
    outs = refs[2 * nprod + ntile + nrow:2 * nprod + ntile + nrow + nout]
    accs = refs[2 * nprod + ntile + nrow + nout:]
    if cast_weights:
        @pl.when(pl.program_id(1) == 0)
        def _():
            for scratch, w in zip(accs, ws):
                scratch[...] = w[...].astype(BF)
        ws = accs

    def finish(vals, cols=slice(None), rws=slice(None)):
        res = epilogue(vals, [t[rws, cols] for t in ts], [r[:, cols] for r in rs])
        for o_ref, o in zip(outs, res):
            o_ref[rws, cols] = o.astype(o_ref.dtype)

    tm, tn = outs[0].shape
    if nk == 1 and row_chunk is not None and tm > row_chunk and tm % row_chunk == 0:
        pending = None
        for ch in range(tm // row_chunk):
            rws = slice(ch * row_chunk, (ch + 1) * row_chunk)
            cur = [jnp.dot(x[rws, :].astype(BF), w[...], preferred_element_type=F32) for x, w in zip(xs, ws)]
            if pending is not None:
                finish(pending[0], rws=pending[1])
            pending = (cur, rws)
        finish(pending[0], rws=pending[1])
        return
    if nk == 1 and col_chunk is not None and tn > col_chunk and tn % col_chunk == 0:
        lhs = [x[...].astype(BF) for x in xs]
        pending = None
        for ch in range(tn // col_chunk):
            cols = slice(ch * col_chunk, (ch + 1) * col_chunk)
            cur = [jnp.dot(a, w[:, cols], preferred_element_type=F32) for a, w in zip(lhs, ws)]
            if pending is not None:
                finish(*pending)
            pending = (cur, cols)
        finish(*pending)
        return

    prods = [jnp.dot(x[...].astype(BF), w[...], preferred_element_type=F32) for x, w in zip(xs, ws)]
    if nk == 1:
        finish(prods)
        return
    k = pl.program_id(2)

    @pl.when(k == 0)
    def _():
        for a, p in zip(accs, prods):
            a[...] = p

    @pl.when(k > 0)
    def _():
        for a, p in zip(accs, prods):
            a[...] += p

    @pl.when(k == nk - 1)
    def _():
        finish([a[...] for a in accs])


def _mm(prods, epilogue, out_dtypes, *, n, tm, tn, tk=None, tiles=(), rows=(), name="mm",
        col_chunk=MXU_WIDTH, row_chunk=None, resident_weights=False):
    prods = [tuple(p) + (0,) * (4 - len(p)) for p in prods]
    m = prods[0][0].shape[0]
    tm = _tile(m, tm)
    ks = [p[0].shape[1] for p in prods]
    nk = 1 if tk is None else ks[0] // tk
    if nk > 1:
        assert all(k == ks[0] for k in ks) and ks[0] % tk == 0
    assert n % tn == 0 and m % tm == 0
    grid = (n // tn, m // tm, nk)
    in_specs, est = [], 0
    for k_dim in ks:
        bk = k_dim if nk == 1 else tk
        in_specs.append(pl.BlockSpec((tm, bk), lambda j, i, k: (i, k)))
    weights = []
    for (x, w, off, row_off), k_dim in zip(prods, ks):
        bk = k_dim if nk == 1 else tk
        assert off % tn == 0 and row_off % bk == 0
        if resident_weights:
            assert nk == 1 and n == tn
        spec, operand = _weight_spec(w, bk, tn, lambda k, rb=row_off // bk: k + rb,
                                     lambda j, ob=off // tn: j + ob, single_buffer=resident_weights)
        in_specs.append(spec)
        weights.append(operand)
        est += 2 * _nbytes((tm, bk), x.dtype) + (1 if resident_weights else 2) * _nbytes((bk, tn), w.dtype)
    for t in tiles:
        in_specs.append(pl.BlockSpec((tm, tn), lambda j, i, k: (i, j)))
        est += 2 * _nbytes((tm, tn), t.dtype)
    for r in rows:
        in_specs.append(pl.BlockSpec((1, tn), lambda j, i, k: (0, j)))
    out_specs = [pl.BlockSpec((tm, tn), lambda j, i, k: (i, j)) for _ in out_dtypes]
    out_shape = [jax.ShapeDtypeStruct((m, n), dt) for dt in out_dtypes]
    est += sum(2 * _nbytes((tm, tn), dt) for dt in out_dtypes)
    est += (len(prods) + 2) * _nbytes((tm, tn), F32)
    scratch = [pltpu.VMEM((tm, tn), F32) for _ in prods] if nk > 1 else []
    cast_weights = any(p[1].dtype != BF for p in prods)
    if cast_weights:
        assert nk == 1 and all(p[1].dtype == F32 for p in prods)
        scratch = [pltpu.VMEM((k_dim, tn), BF) for k_dim in ks]
        est += sum(_nbytes((k_dim, tn), BF) for k_dim in ks)
    body = functools.partial(_mm_body, nprod=len(prods), ntile=len(tiles), nrow=len(rows),
                             nout=len(out_dtypes), nk=nk, epilogue=epilogue, col_chunk=col_chunk,
                             row_chunk=row_chunk, cast_weights=cast_weights)
    return pl.pallas_call(
        body, name=name, grid=grid, in_specs=in_specs, out_specs=out_specs, out_shape=out_shape,
        scratch_shapes=scratch,
        compiler_params=_params(("parallel", "arbitrary", "arbitrary"), est),
    )(*[p[0] for p in prods], *weights, *tiles, *rows)


def _row(v):
    return v.reshape(1, -1).astype(F32)


class _LayerOf:
    def __init__(self, stack, layer):
        self.stack, self.layer = stack, layer
        self.shape, self.dtype = stack.shape[1:], stack.dtype


class _Layers:
    def __init__(self, stack):
        self.stack = stack

    def __getitem__(self, layer):
        return _LayerOf(self.stack, layer)


def _weight_spec(w, rows, cols, row_block, col_block, single_buffer=False):
    mode = dict(pipeline_mode=pl.Buffered(1)) if single_buffer else {}
    if isinstance(w, _LayerOf):
        return (pl.BlockSpec((None, rows, cols), lambda j, i, k=0: (w.layer, row_block(k), col_block(j)),
                             **mode), w.stack)
    return pl.BlockSpec((rows, cols), lambda j, i, k=0: (row_block(k), col_block(j)), **mode), w


def _gate_body(u_ref, va_ref, ws_ref, bs_ref, o_ref, *, blk, nblk):
    ti = lax.broadcasted_iota(jnp.int32, (blk, blk), 0) // CHUNK
    si = lax.broadcasted_iota(jnp.int32, (blk, blk), 1) // CHUNK
    allowed = ti >= si
    for h in range(A_HEADS):
        w = jnp.where(allowed, ws_ref[h], 0.0).astype(BF)
        bias = bs_ref[h]
        for nb in range(nblk):
            rs = slice(nb * blk, (nb + 1) * blk)
            cs = slice(h * A_DIM, (h + 1) * A_DIM)
            mixed = jnp.dot(w, va_ref[rs, cs], preferred_element_type=F32) + bias
            o_ref[rs, cs] = (u_ref[rs, cs].astype(F32) * mixed).astype(o_ref.dtype)


def _spatial_gate(u, va, w_s, b_s, blk):
    m = u.shape[0]
    tm = _tile(m, 512)
    nblk = tm // blk
    est = 2 * 3 * _nbytes((tm, A_WIDTH), BF) + 2 * _nbytes((A_HEADS, blk, blk), F32)
    return pl.pallas_call(
        functools.partial(_gate_body, blk=blk, nblk=nblk),
        name="spatial_gate",
        grid=(m // tm,),
        in_specs=[pl.BlockSpec((tm, A_WIDTH), lambda i: (i, 0)),
                  pl.BlockSpec((tm, A_WIDTH), lambda i: (i, 0)),
                  pl.BlockSpec((A_HEADS, blk, blk), lambda i: (0, 0, 0)),
                  pl.BlockSpec((A_HEADS, blk, 1), lambda i: (0, 0, 0))],
        out_specs=pl.BlockSpec((tm, A_WIDTH), lambda i: (i, 0)),
        out_shape=jax.ShapeDtypeStruct((m, A_WIDTH), BF),
        compiler_params=_params(("parallel",), est),
    )(u, va, w_s.astype(F32), b_s.astype(F32)[:, :, None])


def _later_keys(nkeys):
    jj = lax.broadcasted_iota(jnp.int32, (nkeys, nkeys), 0)
    ss = lax.broadcasted_iota(jnp.int32, (nkeys, nkeys), 1)
    return jnp.where(jj > ss, 1.0, 0.0).astype(BF)


def _sb_chain(qh, kb, valid, later, carry, keys_on_lanes=False):
    if keys_on_lanes:
        z = jnp.dot(qh, kb, preferred_element_type=F32)
    else:
        z = lax.dot_general(qh, kb, NT_DIMS, preferred_element_type=F32)
    yield
    neg_z = -z
    log_fail = jnp.minimum(neg_z, 0.0) - jnp.log(1.0 + jnp.exp(jnp.minimum(z, neg_z)))
    log_hit = z + log_fail
    if valid is not None:
        log_fail = jnp.where(valid, log_fail, 0.0)
    part = log_fail.astype(BF)
    after = jnp.dot(part, later, preferred_element_type=F32)
    rest = log_fail
    for _ in range(SB_CUMSUM_PASSES - 1):
        rest = rest - part.astype(F32)
        part = rest.astype(BF)
        after = after + jnp.dot(part, later, preferred_element_type=F32)
    yield
    total = after[:, 0:1] + log_fail[:, 0:1]
    if carry is None:
        return log_hit + after, total
    w = jnp.exp(log_hit + after + carry)
    if valid is not None:
        w = jnp.where(valid, w, 0.0)
    return w.astype(BF), total


def _sb_cache_body(q_ref, kn_ref, vn_ref, kc_ref, vc_ref, o_ref, c_ref, acc_ref, *, n_steps):
    step = pl.program_id(1)
    t = q_ref.shape[0]
    head_cols = lambda h: slice(h * B_HEAD_DIM, (h + 1) * B_HEAD_DIM)

    def attend(key_of, val_of, nkeys, valid, keys_on_lanes):
        later = _later_keys(nkeys)
        gens = [_sb_chain(q_ref[:, head_cols(h)], key_of(h).astype(BF), valid, later, c_ref[h], keys_on_lanes)
                for h in range(B_HEADS)]
        for h, (w, total) in enumerate(_run_in_lockstep(gens)):
            vals = val_of(h).astype(BF)
            if keys_on_lanes:
                out = lax.dot_general(w, vals, NT_DIMS, preferred_element_type=F32)
            else:
                out = jnp.dot(w, vals, preferred_element_type=F32)
            acc_ref[h] = acc_ref[h] + out
            c_ref[h] = c_ref[h] + total

    @pl.when(step == 0)
    def _():
        c_ref[...] = jnp.zeros_like(c_ref)
        acc_ref[...] = jnp.zeros_like(acc_ref)
        tt = lax.broadcasted_iota(jnp.int32, (t, t), 0)
        sk = lax.broadcasted_iota(jnp.int32, (t, t), 1)
        attend(lambda h: kn_ref[:, head_cols(h)], lambda h: vn_ref[:, head_cols(h)], t, sk < tt, False)

    attend(lambda h: kc_ref[h], lambda h: vc_ref[h], kc_ref.shape[2], None, True)

    @pl.when(step == n_steps - 1)
    def _():
        for h in range(B_HEADS):
            o_ref[:, head_cols(h)] = acc_ref[h].astype(o_ref.dtype)


def _stick_breaking_cache(q, k_new, v_new, k_cache, v_cache, layer, tk):
    bsz, t, _ = q.shape
    p_len = k_cache.shape[4]
    n_steps = p_len // tk
    new_spec = pl.BlockSpec((None, t, B_WIDTH), lambda b, s: (b, 0, 0))
    cache_spec = pl.BlockSpec((None, None, B_HEADS, B_HEAD_DIM, tk),
                              lambda b, s: (layer, b, 0, 0, n_steps - 1 - s))
    est = (2 * 2 * _nbytes((tk, B_HEADS, LANES), F32) + 8 * _nbytes((t, B_WIDTH), F32)
           + 12 * B_HEADS * _nbytes((t, tk), F32))
    return pl.pallas_call(
        functools.partial(_sb_cache_body, n_steps=n_steps),
        name="stick_breaking_cache",
        grid=(bsz, n_steps),
        in_specs=[new_spec, new_spec, new_spec, cache_spec, cache_spec],
        out_specs=new_spec,
        out_shape=jax.ShapeDtypeStruct((bsz, t, B_WIDTH), BF),
        scratch_shapes=[pltpu.VMEM((B_HEADS, t, 1), F32), pltpu.VMEM((B_HEADS, t, B_HEAD_DIM), F32)],
        compiler_params=_params(("parallel", "arbitrary"), est),
    )(q, k_new, v_new, k_cache, v_cache)


def _sb_body(q_ref, kd_ref, vd_ref, kp_ref, vp_ref, o_ref, *, sub, nq, tk, blocks_per_iter, past_iters,
             unroll):
    head_a_q = lax.broadcasted_iota(jnp.int32, (sub, LANES), 1) < B_HEAD_DIM
    q_subs = []
    for i in range(nq):
        q = q_ref[i * sub:(i + 1) * sub, :]
        zero_q = jnp.zeros_like(q)
        q_subs.append((jnp.where(head_a_q, q, zero_q), jnp.where(head_a_q, zero_q, q)))

    def attend(k_blocks, v_blocks, masks, carries, accs):
        kbs = [k.astype(BF) for k in k_blocks]
        laters, v_stacks = {}, []
        for kb, v_blk in zip(kbs, v_blocks):
            nkeys = kb.shape[0]
            if nkeys not in laters:
                laters[nkeys] = _later_keys(nkeys)
            vb = v_blk.astype(BF)
            head_a_k = lax.broadcasted_iota(jnp.int32, (nkeys, LANES), 1) < B_HEAD_DIM
            zero_v = jnp.zeros_like(vb)
            v_stacks.append(jnp.concatenate([jnp.where(head_a_k, vb, zero_v), jnp.where(head_a_k, zero_v, vb)],
                                            axis=0))
        early = len(kbs) == 1
        keys, gens = [], []
        for i in range(nq):
            for j, kb in enumerate(kbs):
                if isinstance(masks[i][j], str):
                    continue
                for h in range(2):
                    keys.append((i, j, h))
                    gens.append(_sb_chain(q_subs[i][h], kb, masks[i][j], laters[kb.shape[0]],
                                          carries[i][h] if early else None))
        scores = dict(zip(keys, _run_in_lockstep(gens, lag=SB_CHAIN_LAG)))
        new_carries, new_accs = [], []
        for i in range(nq):
            cs = list(carries[i])
            wts, vals = [], []
            for j in range(len(kbs)):
                if isinstance(masks[i][j], str):
                    continue
                for h in range(2):
                    w, total = scores[(i, j, h)]
                    if not early:
                        w = jnp.exp(w + cs[h])
                        if masks[i][j] is not None:
                            w = jnp.where(masks[i][j], w, 0.0)
                        w = w.astype(BF)
                    cs[h] = cs[h] + total
                    wts.append(w)
                vals.append(v_stacks[j])
            new_carries.append(cs)
            new_accs.append(accs[i] + jnp.dot(jnp.concatenate(wts, axis=1), jnp.concatenate(vals, axis=0),
                                              preferred_element_type=F32))
        return new_carries, new_accs

    k_diag = [kd_ref[j * sub:(j + 1) * sub, :] for j in reversed(range(nq))]
    v_diag = [vd_ref[j * sub:(j + 1) * sub, :] for j in reversed(range(nq))]
    tt = lax.broadcasted_iota(jnp.int32, (sub, sub), 0)
    sk = lax.broadcasted_iota(jnp.int32, (sub, sub), 1)
    causal = sk < tt
    masks = [[causal if j == i else (None if j < i else "skip") for j in reversed(range(nq))]
             for i in range(nq)]
    zeros_c = jnp.zeros((sub, 1), F32)
    carries, accs = attend(k_diag, v_diag, masks, [[zeros_c, zeros_c] for _ in range(nq)],
                           [jnp.zeros((sub, LANES), F32) for _ in range(nq)])
    n_iters = past_iters(pl.program_id(2))
    open_masks = [[None] * blocks_per_iter for _ in range(nq)]

    def step(it, state):
        carries = [list(state[2 * i:2 * i + 2]) for i in range(nq)]
        accs = list(state[2 * nq:])
        newest = n_iters - 1 - it
        offs = [pl.multiple_of((newest * blocks_per_iter + j) * tk, tk)
                for j in reversed(range(blocks_per_iter))]
        carries, accs = attend([kp_ref[pl.ds(o, tk), :] for o in offs], [vp_ref[pl.ds(o, tk), :] for o in offs],
                               open_masks, carries, accs)
        return tuple(c for pair in carries for c in pair) + tuple(accs)

    state = lax.fori_loop(0, n_iters, step, tuple(c for pair in carries for c in pair) + tuple(accs),
                          unroll=unroll)
    for i in range(nq):
        o_ref[i * sub:(i + 1) * sub, :] = state[2 * nq + i].astype(o_ref.dtype)


def _stick_breaking(q, k, v, *, tq, tk):
    bsz, t, _ = q.shape
    k_new, v_new, k_past, v_past = k, v, k, v
    past_spec = pl.BlockSpec((None, t, LANES), lambda b, p, i: (b, 0, p))
    sub = min(tq, SB_SUBTILE)
    nq = tq // sub
    blocks_per_iter = 2 if (tq // tk) % 2 == 0 else 1
    past_iters = lambda qi: qi * (tq // tk) // blocks_per_iter
    est = (2 * 2 * _nbytes((t, LANES), k.dtype) + 2 * 4 * _nbytes((tq, LANES), F32)
           + 12 * 2 * nq * blocks_per_iter * _nbytes((sub, max(sub, tk)), F32))
    return pl.pallas_call(
        functools.partial(_sb_body, sub=sub, nq=nq, tk=tk, blocks_per_iter=blocks_per_iter,
                          past_iters=past_iters, unroll=None),
        name="stick_breaking_past",
        grid=(bsz, B_WIDTH // LANES, t // tq),
        in_specs=[pl.BlockSpec((None, tq, LANES), lambda b, p, i: (b, i, p)),
                  pl.BlockSpec((None, tq, LANES), lambda b, p, i: (b, i, p)),
                  pl.BlockSpec((None, tq, LANES), lambda b, p, i: (b, i, p)),
                  past_spec, past_spec],
        out_specs=pl.BlockSpec((None, tq, LANES), lambda b, p, i: (b, i, p)),
        out_shape=jax.ShapeDtypeStruct((bsz, t, B_WIDTH), BF),
        compiler_params=_params(("parallel", "parallel", "arbitrary"), est),
    )(q, k_new, v_new, k_past, v_past)


def _shift_body(x_ref, last_ref, mu_ref, *rest, tiles_per_seq):
    outs, carry = rest[:6], rest[6]
    i = pl.program_id(1)

    @pl.when(i % tiles_per_seq == 0)
    def _():
        carry[...] = jnp.broadcast_to(last_ref[...], carry.shape)

    x = x_ref[...]
    rows = lax.broadcasted_iota(jnp.int32, x.shape, 0)
    x_prev = jnp.where(rows == 0, carry[SUBLANES - 1:SUBLANES, :], pltpu.roll(x, 1, 0))
    carry[...] = x[x.shape[0] - SUBLANES:, :]
    xx = x_prev - x
    for j, o_ref in enumerate(outs):
        o_ref[...] = (x + xx * mu_ref[j:j + 1, :]).astype(o_ref.dtype)


def _token_shift(x, x_last, mu):
    bsz, t, d = x.shape
    tm = _tile(t, 256)
    est = 2 * _nbytes((tm, d), F32) + 12 * _nbytes((tm, d), BF) + 4 * _nbytes((tm, d), F32)
    return pl.pallas_call(
        functools.partial(_shift_body, tiles_per_seq=t // tm),
        name="token_shift",
        grid=(bsz, t // tm),
        in_specs=[pl.BlockSpec((None, tm, d), lambda b, i: (b, i, 0)),
                  pl.BlockSpec((None, 1, d), lambda b, i: (b, 0, 0)),
                  pl.BlockSpec((6, d), lambda b, i: (0, 0))],
        out_specs=[pl.BlockSpec((None, tm, d), lambda b, i: (b, i, 0)) for _ in range(6)],
        out_shape=[jax.ShapeDtypeStruct((bsz, t, d), BF) for _ in range(6)],
        scratch_shapes=[pltpu.VMEM((SUBLANES, d), F32)],
        compiler_params=_params(("parallel", "arbitrary"), est),
    )(x, x_last[:, None, :].astype(F32), mu.astype(F32))


def _wkv_body(r_ref, k_ref, v_ref, lw_ref, a_ref, g_ref, kk_ref, ka_ref, rk_ref, gg_ref, gb_ref,
              s0_ref, y_ref, s_ref):
    c = WKV_CHUNK
    n = 2 * c

    @pl.when(pl.program_id(1) == 0)
    def _():
        s_ref[...] = s0_ref[...]

    head_a = lax.broadcasted_iota(jnp.int32, (c, LANES), 1) < C_HEAD_DIM
    ri = lax.broadcasted_iota(jnp.int32, (n, n), 0)
    ci = lax.broadcasted_iota(jnp.int32, (n, n), 1)
    same_head = (ri // c) == (ci // c)
    strict = same_head & (ci < ri)
    inclusive = same_head & (ci <= ri)
    diag16 = (ri // 16) == (ci // 16)
    diag32 = (ri // 32) == (ci // 32)
    eye = jnp.where(ri == ci, 1.0, 0.0).astype(F32)
    seg_ones = jnp.where(same_head, 1.0, 0.0).astype(BF)
    ti = lax.broadcasted_iota(jnp.int32, (c, c), 0)
    si = lax.broadcasted_iota(jnp.int32, (c, c), 1)
    prefix = jnp.where(si <= ti, 1.0, 0.0).astype(BF)

    def split2(x):
        hi = x.astype(BF)
        return hi, (x - hi.astype(F32)).astype(BF)

    def seg_sum(x):
        hi, lo = split2(x)
        return (jnp.dot(hi, seg_ones, preferred_element_type=F32)
                + jnp.dot(lo, seg_ones, preferred_element_type=F32))

    def cumsum_time(x):
        hi, rest = split2(x)
        rest_f = x - hi.astype(F32)
        mid = rest_f.astype(BF)
        lo = (rest_f - mid.astype(F32)).astype(BF)
        return (jnp.dot(prefix, hi, preferred_element_type=F32)
                + jnp.dot(prefix, mid, preferred_element_type=F32)
                + jnp.dot(prefix, lo, preferred_element_type=F32))

    def stack(x):
        zero = jnp.zeros_like(x)
        return jnp.concatenate([jnp.where(head_a, x, zero), jnp.where(head_a, zero, x)], axis=0)

    def pair(p, s):
        sl = pl.ds(pl.multiple_of(p * LANES, LANES), LANES)
        r = r_ref[:, sl]
        k = k_ref[:, sl]
        v = v_ref[:, sl]
        lw = lw_ref[:, sl]
        a = a_ref[:, sl]
        kkp = k * kk_ref[:, sl]
        norm = jnp.sqrt(seg_sum(kkp * kkp))
        yield
        kk = kkp / jnp.maximum(norm, 1e-12)
        k2 = k * (1.0 + (a - 1.0) * ka_ref[:, sl])
        bonus = seg_sum(r * k2 * rk_ref[:, sl]) * v
        yield
        cs = cumsum_time(lw)
        yield
        e_pos = jnp.exp(cs)
        e_neg = jnp.exp(-cs)
        a_t = -kk * e_pos * jnp.exp(-lw)
        b_t = kk * a * e_neg
        k_t = k2 * e_neg
        r_t = r * e_pos
        la, lr = stack(a_t), stack(r_t)
        rb, rk = stack(b_t), stack(k_t)
        vs = stack(v)
        q = lax.dot_general(jnp.concatenate([la, lr], axis=0).astype(BF),
                            jnp.concatenate([rb, rk], axis=0).astype(BF),
                            NT_DIMS, preferred_element_type=F32)
        yield
        p_ab = jnp.where(strict, q[:n, :n], 0.0)
        a_ak = jnp.where(strict, q[:n, n:], 0.0)
        a_rb = jnp.where(inclusive, q[n:, :n], 0.0)
        a_rk = jnp.where(inclusive, q[n:, n:], 0.0)
        akv = _bdot(a_ak, vs)
        yield
        pd = jnp.where(diag16, p_ab, 0.0)
        inv = eye + pd
        pw = pd
        for _ in range(3):
            pw = _bdot(pw, pw)
            yield
            inv = inv + _bdot(pw, inv)
            yield
        for off_diag in (jnp.where(diag32 & (~diag16), p_ab, 0.0), jnp.where(diag32, 0.0, p_ab)):
            right = _bdot(off_diag, inv)
            yield
            inv = inv + _bdot(inv, right)
            yield
        wu = _bdot(inv, jnp.concatenate([la, akv], axis=1))
        yield
        w_m, u_m = wu[:, :LANES], wu[:, LANES:]
        wr = lax.dot_general(jnp.concatenate([w_m, lr], axis=0).astype(BF), s.astype(BF),
                             NT_DIMS, preferred_element_type=F32)
        yield
        sa = wr[:n] + u_m
        sv = jnp.concatenate([sa, vs], axis=0)
        y_st = wr[n:] + _bdot(jnp.concatenate([a_rb, a_rk], axis=1), sv)
        g_end = e_pos[c - 1:c, :]
        rg = jnp.concatenate([rb, rk], axis=0) * g_end
        s_new = s * g_end + lax.dot_general(sv.astype(BF), rg.astype(BF), TN_DIMS,
                                            preferred_element_type=F32)
        yield
        y = y_st[:c] + y_st[c:]
        mean = seg_sum(y) * (1.0 / C_HEAD_DIM)
        yield
        d = y - mean
        var = seg_sum(d * d) * (1.0 / C_HEAD_DIM)
        yield
        yn = d * lax.rsqrt(var + C_GN_EPS) * gg_ref[:, sl] + gb_ref[:, sl]
        return s_new, ((yn + bonus) * g_ref[:, sl]).astype(y_ref.dtype)

    def group(it, carry):
        pairs = [it * WKV_PAIRS_PER_STEP + u for u in range(WKV_PAIRS_PER_STEP)]
        states = [s_ref[p] for p in pairs]
        results = _run_in_lockstep([pair(p, s) for p, s in zip(pairs, states)])
        for p, (s_new, y_out) in zip(pairs, results):
            s_ref[p] = s_new
            y_ref[:, pl.ds(pl.multiple_of(p * LANES, LANES), LANES)] = y_out
        return carry

    lax.fori_loop(0, N_PAIRS // WKV_PAIRS_PER_STEP, group, 0)


def _pair_state(s):
    bsz = s.shape[0]
    s = s.reshape(bsz, N_PAIRS, 2, C_HEAD_DIM, C_HEAD_DIM).astype(F32)
    z = jnp.zeros_like(s[:, :, 0])
    top = jnp.concatenate([s[:, :, 0], z], axis=-1)
    bot = jnp.concatenate([z, s[:, :, 1]], axis=-1)
    return jnp.concatenate([top, bot], axis=-2)


def _unpair_state(sp):
    bsz = sp.shape[0]
    a = sp[:, :, :C_HEAD_DIM, :C_HEAD_DIM]
    b = sp[:, :, C_HEAD_DIM:, C_HEAD_DIM:]
    return jnp.stack([a, b], axis=2).reshape(bsz, C_HEADS, C_HEAD_DIM, C_HEAD_DIM)


def _wkv(r, k, v, lw, a, g, s0, k_k, k_a, r_k, gn_g, gn_b):
    bsz, t, d = r.shape
    c = WKV_CHUNK
    seq = pl.BlockSpec((None, c, d), lambda b, i: (b, i, 0))
    vec = pl.BlockSpec((1, d), lambda b, i: (0, 0))
    st = pl.BlockSpec((None, N_PAIRS, HEAD_PAIR, HEAD_PAIR), lambda b, i: (b, 0, 0, 0))
    est = 2 * 6 * _nbytes((c, d), F32) + 4 * _nbytes((N_PAIRS, HEAD_PAIR, HEAD_PAIR), F32) + (8 << 20)
    y, s_new = pl.pallas_call(
        _wkv_body,
        name="wkv7_chunked",
        grid=(bsz, t // c),
        in_specs=[seq] * 6 + [vec] * 5 + [st],
        out_specs=[seq, st],
        out_shape=[jax.ShapeDtypeStruct((bsz, t, d), BF),
                   jax.ShapeDtypeStruct((bsz, N_PAIRS, HEAD_PAIR, HEAD_PAIR), F32)],
        compiler_params=_params(("parallel", "arbitrary"), est),
    )(r, k, v, lw, a, g, _row(k_k), _row(k_a), _row(r_k), _row(gn_g), _row(gn_b), _pair_state(s0))
    return y, _unpair_state(s_new)


def _ffn_in_body(x_ref, wg_f32_ref, wu_f32_ref, cw_ref, cb_ref, prev_ref, act_ref, st_ref, tail_ref,
                 wg_ref, wu_ref, *, seq_len, tm):
    @pl.when(pl.program_id(1) == 0)
    def _():
        wg_ref[...] = wg_f32_ref[...].astype(BF)
        wu_ref[...] = wu_f32_ref[...].astype(BF)

    x = x_ref[...]
    tn = act_ref.shape[1]
    cn = min(tn, MXU_WIDTH)
    last2, last1 = SUBLANES - 2, SUBLANES - 1
    carried = seq_len >= tm
    if carried:
        @pl.when(pl.program_id(1) % (seq_len // tm) == 0)
        def _():
            tail_ref[...] = prev_ref[0]

    def conv_gate(hg, hu, cols):
        rows = lax.broadcasted_iota(jnp.int32, hg.shape, 0)
        if carried:
            tail = tail_ref[:, cols]
            p0, p1 = tail[last2:last2 + 1, :], tail[last1:last1 + 1, :]
            pos = rows
            new_tail = hg[tm - SUBLANES:, :]
            tail_ref[:, cols] = new_tail
            st_ref[0, :, cols] = new_tail
        else:
            nseq = tm // seq_len
            prev = prev_ref[:, :, cols]
            spread = lambda rw: jnp.broadcast_to(rw, (nseq, seq_len, cn)).reshape(tm, cn)
            p0, p1 = spread(prev[:, last2:last2 + 1, :]), spread(prev[:, last1:last1 + 1, :])
            pos = rows % seq_len
            st_ref[:, :, cols] = hg.reshape(nseq, seq_len, cn)[:, seq_len - SUBLANES:, :]
        s1 = jnp.where(pos == 0, p1, pltpu.roll(hg, 1, 0))
        s2 = jnp.where(pos == 0, p0, jnp.where(pos == 1, p1, pltpu.roll(hg, 2, 0)))
        cw = cw_ref[:, cols]
        hc = cb_ref[:, cols] + s2 * cw[0:1, :] + s1 * cw[1:2, :] + hg * cw[2:3, :]
        act_ref[:, cols] = (jax.nn.gelu(hc) * hu).astype(act_ref.dtype)

    pending = None
    for ch in range(tn // cn):
        cols = slice(ch * cn, (ch + 1) * cn)
        hg = jnp.dot(x, wg_ref[:, cols], preferred_element_type=F32)
        hu = jnp.dot(x, wu_ref[:, cols], preferred_element_type=F32)
        if pending is not None:
            conv_gate(*pending)
        pending = (hg, hu, cols)
    conv_gate(*pending)


def _ffn_in(xb, w_in, conv_w, conv_b, conv_prev, seq_len):
    m, d = xb.shape
    nseq_total = m // seq_len
    tn = 512
    tm = _tile(m, 1024)
    nj = D_FF // tn
    prev8 = jnp.pad(conv_prev.astype(F32), ((0, 0), (SUBLANES - (CONV_W - 1), 0), (0, 0)))
    if seq_len >= tm:
        per = seq_len // tm
        nb = 1
        st_map = lambda j, i: (i // per, 0, j)
    else:
        nb = tm // seq_len
        st_map = lambda j, i: (i, 0, j)
    st_spec = pl.BlockSpec((nb, SUBLANES, tn), st_map)
    assert w_in.dtype == F32
    est = (2 * _nbytes((tm, d), BF) + 4 * _nbytes((d, tn), F32) + 2 * _nbytes((d, tn), BF)
           + 2 * _nbytes((tm, tn), BF) + 8 * _nbytes((tm, MXU_WIDTH), F32))
    gate_spec, w_operand = _weight_spec(w_in, d, tn, lambda k: 0, lambda j: j)
    up_spec, _ = _weight_spec(w_in, d, tn, lambda k: 0, lambda j: j + nj)
    act, tails = pl.pallas_call(
        functools.partial(_ffn_in_body, seq_len=seq_len, tm=tm),
        name="ffn_in_conv",
        grid=(nj, m // tm),
        in_specs=[pl.BlockSpec((tm, d), lambda j, i: (i, 0)),
                  gate_spec,
                  up_spec,
                  pl.BlockSpec((CONV_W, tn), lambda j, i: (0, j)),
                  pl.BlockSpec((1, tn), lambda j, i: (0, j)),
                  st_spec],
        out_specs=[pl.BlockSpec((tm, tn), lambda j, i: (i, j)), st_spec],
        out_shape=[jax.ShapeDtypeStruct((m, D_FF), BF),
                   jax.ShapeDtypeStruct((nseq_total, SUBLANES, D_FF), F32)],
        scratch_shapes=[pltpu.VMEM((SUBLANES, tn), F32), pltpu.VMEM((d, tn), BF), pltpu.VMEM((d, tn), BF)],
        compiler_params=_params(("parallel", "arbitrary"), est),
    )(xb, w_operand, w_operand, conv_w.astype(F32), _row(conv_b), prev8)
    return act, tails[:, SUBLANES - (CONV_W - 1):, :]


def _ln_residual_epilogue(accs, tiles, rows):
    mix = accs[0]
    for extra in accs[1:]:
        mix = mix + extra
    y = _layernorm(DN_ALPHA * tiles[0] + mix, rows[0], rows[1])
    return y, y


def _identity_epilogue(accs, tiles, rows):
    return (accs[0],)


def _even_mixer(x, xb, seq_len, w_in, lnv_g, lnv_b, w_s, b_s, w_o, ln_g, ln_b, cache):
    m = x.shape[0]
    bsz = m // seq_len
    proj = lambda name, off, ep, dts, rows=(), col_chunk=MXU_WIDTH: _mm(
        [(xb, w_in, off)], ep, dts, n=1024, tm=512, tn=1024, rows=rows, name=name, col_chunk=col_chunk)
    (u,) = proj("even_u", 0, lambda a, t, r: (jax.nn.gelu(a[0]),), [BF])

    def va_epilogue(a, t, r):
        va = _layernorm(jax.nn.gelu(a[0]), r[0], r[1])
        return va, va
    va, vab = proj("even_va", A_WIDTH, va_epilogue, [F32, BF], rows=(_row(lnv_g), _row(lnv_b)),
                   col_chunk=None)
    o = 2 * A_WIDTH
    (q,) = proj("even_q", o, lambda a, t, r: (a[0] * (B_HEAD_DIM ** -0.5),), [BF])
    (k,) = proj("even_k", o + B_WIDTH, _identity_epilogue, [F32])
    (v,) = proj("even_v", o + 2 * B_WIDTH, _identity_epilogue, [F32])

    blk = min(seq_len, A_BLOCK)
    a_out = _spatial_gate(u, vab, w_s[:, :blk, :blk], b_s[:, :blk], blk)

    q3 = q.reshape(bsz, seq_len, B_WIDTH)
    k3 = k.reshape(bsz, seq_len, B_WIDTH)
    v3 = v.reshape(bsz, seq_len, B_WIDTH)
    if cache is None:
        tq = _tile(seq_len, 4 * SB_BLOCK)
        b_out = _stick_breaking(q3, k3, v3, tq=tq, tk=min(tq, SB_BLOCK))
    else:
        k_cache, v_cache, layer = cache
        b_out = _stick_breaking_cache(q3, k3, v3, k_cache, v_cache, layer, _tile(k_cache.shape[4], SB_BLOCK))
    b_out = b_out.reshape(m, B_WIDTH)

    x1, x1b = _mm([(a_out, w_o, 0), (b_out, w_o, 0, A_WIDTH)], _ln_residual_epilogue, [F32, BF],
                  n=D_MODEL, tm=2 * LN_ROW_CHUNK, tn=D_MODEL, tiles=(x,), rows=(_row(ln_g), _row(ln_b)),
                  name="even_out_ln", col_chunk=None, row_chunk=LN_ROW_CHUNK, resident_weights=True)
    return x1, x1b, k, v, va


def _rwkv_mixer(x, seq_len, x_last, s0, v_first, P, ln_g, ln_b):
    m = x.shape[0]
    bsz = m // seq_len
    mixes = _token_shift(x.reshape(bsz, seq_len, D_MODEL), x_last, P['mu'])
    xr, xw, xk, xv, xa, xg = [a.reshape(m, D_MODEL) for a in mixes]
    big = lambda xin, w: _mm([(xin, w, 0)], _identity_epilogue, [F32], n=D_MODEL, tm=512, tn=1024,
                             name="rwkv_proj")[0]
    r = big(xr, P['w_r'])
    k = big(xk, P['w_k'])

    def lora_in(xin, w, act):
        n = w.shape[1]
        return _mm([(xin, w, 0)], lambda a, t, r_: (act(a[0]),), [BF], n=n, tm=512, tn=n,
                   name="rwkv_lora_in")[0]

    def lora_out(h, w, bias, act):
        return _mm([(h, w, 0)], lambda a, t, r_: (act(r_[0] + a[0]),), [F32], n=D_MODEL, tm=512,
                   tn=1024, rows=(_row(bias),), name="rwkv_lora_out")[0]

    hw = lora_in(xw, P['w1'], jnp.tanh)
    lw = lora_out(hw, P['w2'], P['w0'],
                  lambda pre: -jnp.exp(-(jnp.maximum(-pre, 0.0) + jnp.log(1.0 + jnp.exp(-jnp.abs(pre)))) - 0.5))
    ha = lora_in(xa, P['a1'], lambda z: z)
    a = lora_out(ha, P['a2'], P['a0'], jax.nn.sigmoid)
    hg = lora_in(xg, P['g1'], jax.nn.sigmoid)
    g = _mm([(hg, P['g2'], 0)], _identity_epilogue, [F32], n=D_MODEL, tm=512, tn=1024, name="rwkv_gate")[0]
    if P['v1'] is None:
        v = big(xv, P['w_v'])
        v_first = v
    else:
        hv = lora_in(xv, P['v1'], lambda z: z)

        def v_epilogue(accs, tiles, rows):
            v_new = accs[0]
            return (v_new + (tiles[0] - v_new) * jax.nn.sigmoid(rows[0] + accs[1]),)
        v = _mm([(xv, P['w_v'], 0), (hv, P['v2'], 0)], v_epilogue, [F32], n=D_MODEL, tm=512, tn=1024,
                tiles=(v_first,), rows=(_row(P['v0']),), name="rwkv_v_residual")[0]
    to3 = lambda z: z.reshape(bsz, seq_len, D_MODEL)
    yg, s_new = _wkv(to3(r), to3(k), to3(v), to3(lw), to3(a), to3(g), s0,
                     P['k_k'], P['k_a'], P['r_k'], P['gn_g'], P['gn_b'])
    x1, x1b = _mm([(yg.reshape(m, D_MODEL), P['w_o'], 0)], _ln_residual_epilogue, [F32, BF],
                  n=D_MODEL, tm=2 * LN_ROW_CHUNK, tn=D_MODEL, tiles=(x,), rows=(_row(ln_g), _row(ln_b)),
                  name="rwkv_out_ln", col_chunk=None, row_chunk=LN_ROW_CHUNK, resident_weights=True)
    return x1, x1b, s_new, v_first


def _ffn_ple(x1, x1b, seq_len, p, conv_prev, w_in, conv_w, conv_b, w_out, ln_g, ln_b, ple_gate, ple_proj):
    act, conv_rows = _ffn_in(x1b, w_in, conv_w, conv_b, conv_prev, seq_len)
    x2, x2b = _mm([(act, w_out, 0)], _ln_residual_epilogue, [F32, BF], n=D_MODEL, tm=512, tn=D_MODEL,
                  tk=D_FF // 4, tiles=(x1,), rows=(_row(ln_g), _row(ln_b)), name="ffn_out_ln", col_chunk=None)

    def ple_epilogue(accs, tiles, rows):
        y = tiles[0] + jax.nn.sigmoid(accs[0]) * accs[1]
        return y, y
    x3, x3b = _mm([(x2b, ple_gate, 0), (p, ple_proj, 0)], ple_epilogue, [F32, BF], n=D_MODEL, tm=512,
                  tn=1024, tiles=(x2,), name="ple")
    return x3, x3b, conv_rows


def _trunk(x, p, W, caches, wkv0, shift0, conv0):
    bsz, seq_len, _ = x.shape
    m = bsz * seq_len
    x = x.reshape(m, D_MODEL).astype(F32)
    xb = x.astype(BF)
    new_k, new_v, new_va, new_wkv, new_shift, new_conv = [], [], [], [], [], []
    v_first = None
    for i in range(DEPTH):
        j = i // 2
        if i % 2 == 0:
            cache = None if caches is None else (caches[0], caches[1], j)
            x1, x1b, k, v, va = _even_mixer(
                x, xb, seq_len, W['even_w_in'][j], W['even_lnv_g'][j], W['even_lnv_b'][j],
                W['even_w_s'][j], W['even_b_s'][j], W['even_w_o'][j], W['ln1_g'][i], W['ln1_b'][i], cache)
            new_k.append(k.reshape(bsz, seq_len, B_HEADS, B_HEAD_DIM))
            new_v.append(v.reshape(bsz, seq_len, B_HEADS, B_HEAD_DIM))
            new_va.append(va.reshape(bsz, seq_len, A_HEADS, A_DIM))
        else:
            P = dict(mu=W['c_mu'][j], w_r=W['c_w_r'][j], w_k=W['c_w_k'][j], w_v=W['c_w_v'][j],
                     w_o=W['c_w_o'][j], w0=W['c_w0'][j], w1=W['c_w1'][j], w2=W['c_w2'][j],
                     a0=W['c_a0'][j], a1=W['c_a1'][j], a2=W['c_a2'][j], g1=W['c_g1'][j], g2=W['c_g2'][j],
                     k_k=W['c_k_k'][j], k_a=W['c_k_a'][j], r_k=W['c_r_k'][j], gn_g=W['c_gn_g'][j],
                     gn_b=W['c_gn_b'][j],
                     v0=None if j == 0 else W['c_v0'][j - 1], v1=None if j == 0 else W['c_v1'][j - 1],
                     v2=None if j == 0 else W['c_v2'][j - 1])
            x1, x1b, s_new, v_first = _rwkv_mixer(x, seq_len, shift0[j], wkv0[j], v_first, P,
                                                  W['ln1_g'][i], W['ln1_b'][i])
            new_wkv.append(s_new)
            new_shift.append(x.reshape(bsz, seq_len, D_MODEL)[:, -1])
        x, xb, conv_rows = _ffn_ple(
            x1, x1b, seq_len, p[i].reshape(m, -1), conv0[i], W['ffn_w_in'][i], W['ffn_conv_w'][i],
            W['ffn_conv_b'][i], W['ffn_w_out'][i], W['ln2_g'][i], W['ln2_b'][i], W['ple_gate'][i],
            W['ple_proj'][i])
        new_conv.append(conv_rows)
    st = lambda l: jnp.stack(l) if l else None
    return (x.reshape(bsz, seq_len, D_MODEL), st(new_k), st(new_v), st(new_va), st(new_wkv),
            st(new_shift), st(new_conv))


_MATRICES = ('even_w_in', 'even_w_o', 'c_w_r', 'c_w_k', 'c_w_v', 'c_w_o', 'c_w1', 'c_w2', 'c_a1', 'c_a2',
             'c_v1', 'c_v2', 'c_g1', 'c_g2', 'ffn_w_in', 'ffn_w_out', 'ple_proj', 'ple_gate')
_PRECAST = ('even_w_o', 'c_w_o', 'ffn_w_out')


def kernel(x_prompt, x_sample, cache_sb_k, cache_sb_v, state_rwkv_wkv, state_rwkv_shift, state_ffn_conv, p_prompt, p_sample, even_w_in, even_lnv_g, even_lnv_b, even_w_s, even_b_s, even_w_o, c_mu, c_w_r, c_w_k, c_w_v, c_w_o, c_w0, c_w1, c_w2, c_a0, c_a1, c_a2, c_v0, c_v1, c_v2, c_g1, c_g2, c_k_k, c_k_a, c_r_k, c_gn_g, c_gn_b, ffn_w_in, ffn_conv_w, ffn_conv_b, ffn_w_out, ln1_g, ln1_b, ln2_g, ln2_b, ple_proj, ple_gate):
    W = dict(even_w_in=even_w_in, even_lnv_g=even_lnv_g, even_lnv_b=even_lnv_b, even_w_s=even_w_s,
             even_b_s=even_b_s, even_w_o=even_w_o, c_mu=c_mu, c_w_r=c_w_r, c_w_k=c_w_k,
             c_w_v=c_w_v, c_w_o=c_w_o, c_w0=c_w0, c_w1=c_w1, c_w2=c_w2, c_a0=c_a0, c_a1=c_a1,
             c_a2=c_a2, c_v0=c_v0, c_v1=c_v1, c_v2=c_v2, c_g1=c_g1, c_g2=c_g2, c_k_k=c_k_k,
             c_k_a=c_k_a, c_r_k=c_r_k, c_gn_g=c_gn_g, c_gn_b=c_gn_b, ffn_w_in=ffn_w_in,
             ffn_conv_w=ffn_conv_w, ffn_conv_b=ffn_conv_b, ffn_w_out=ffn_w_out, ln1_g=ln1_g,
             ln1_b=ln1_b, ln2_g=ln2_g, ln2_b=ln2_b, ple_proj=ple_proj, ple_gate=ple_gate)
    for name in _MATRICES:
        W[name] = _Layers(W[name].astype(BF) if name in _PRECAST else W[name])
    bp = x_prompt.shape[0]
    n_odd, n_even = state_rwkv_wkv.shape[0], cache_sb_k.shape[0]
    cache_sb_k = jnp.transpose(cache_sb_k, (0, 1, 3, 4, 2))
    cache_sb_v = jnp.transpose(cache_sb_v, (0, 1, 3, 4, 2))
    wkv_zero = jnp.zeros((n_odd, bp, C_HEADS, C_HEAD_DIM, C_HEAD_DIM), F32)
    shift_zero = jnp.zeros((n_odd, bp, D_MODEL), F32)
    conv_zero = jnp.zeros((DEPTH, bp, CONV_W - 1, D_FF), F32)
    y_p, k_p, v_p, _, wkv_p, shift_p, conv_p = _trunk(
        x_prompt, p_prompt, W, None, wkv_zero, shift_zero, conv_zero)
    y_s, k_s, v_s, va_s, wkv_s, shift_s, conv_s = _trunk(
        x_sample, p_sample, W, (cache_sb_k, cache_sb_v), state_rwkv_wkv, state_rwkv_shift, state_ffn_conv)
    return (y_p, y_s, k_p, v_p, wkv_p, shift_p, conv_p, k_s, v_s, va_s, wkv_s, shift_s, conv_s)
```
